```python
import math
import jax, jax.numpy as jnp
from jax import lax
import numpy as np

D_MODEL = 2048
BATCH = 2
SEQ = 8192
DEPTH = 2

D_MIX = D_MODEL
D_POOL = D_MIX // 2
D_LRU = D_MIX // 2
POOL_WINDOWS = (2, 4, 8, 16)
POOL_GROUP = D_POOL // len(POOL_WINDOWS)
LRU_BLOCKS = 8
LRU_BLOCK = D_LRU // LRU_BLOCKS
CONV_WIDTH = 4
LRU_C = 8.0
N_HEADS_ATTN = 16
HEAD_DIM = D_MIX // N_HEADS_ATTN
Q_BLOCK = 128
PEER_HEADS = 8
PEER_TOPK = 16
N_KEYS = 128
N_EXPERTS = N_KEYS * N_KEYS
PEER_HALF = 128
PEER_QDIM = 2 * PEER_HALF
PEER_CHUNK = 128
N_EVEN = (DEPTH + 1) // 2
N_ODD = DEPTH // 2
ALPHA = (2.0 * DEPTH) ** 0.25
BETA = (8.0 * DEPTH) ** -0.25
LN_EPS = 1e-5

kernel_name = "hybrid_pool_rglru_fox_peer_deepnorm"


def layer_norm(x, g, b):
    xf = x.astype(jnp.float32)
    mu = jnp.mean(xf, axis=-1, keepdims=True)
    var = jnp.mean(jnp.square(xf - mu), axis=-1, keepdims=True)
    y = (xf - mu) * lax.rsqrt(var + LN_EPS)
    return (y * g.astype(jnp.float32) + b.astype(jnp.float32)).astype(x.dtype)


def multiscale_pool(u, w_pool, pool_scale):
    S = u.shape[1]
    pos = jnp.arange(S)
    uf = u.astype(jnp.float32)
    outs = []
    for g, w in enumerate(POOL_WINDOWS):
        ug = uf[..., g * POOL_GROUP:(g + 1) * POOL_GROUP]
        cs = jnp.cumsum(ug, axis=1)
        cs_lag = jnp.pad(cs, ((0, 0), (w, 0), (0, 0)))[:, :S]
        cnt = jnp.minimum(pos + 1, w).astype(jnp.float32)[None, :, None]
        pooled = (cs - cs_lag) / cnt - ug
        outs.append(jnp.einsum('bsc,cd->bsd', pooled.astype(u.dtype), w_pool[g]))
    return jnp.concatenate(outs, axis=-1) * pool_scale


def causal_conv(x, w, b):
    S = x.shape[1]
    xp = jnp.pad(x, ((0, 0), (CONV_WIDTH - 1, 0), (0, 0)))
    y = b
    for k in range(CONV_WIDTH):
        y = y + xp[:, k:k + S] * w[k]
    return y


def block_diag(x, w, b):
    B_, S_, _ = x.shape
    xb = x.reshape(B_, S_, LRU_BLOCKS, LRU_BLOCK)
    return jnp.einsum('bshi,hij->bshj', xb, w).reshape(B_, S_, D_LRU) + b


def rg_lru(x, w_a, b_a, w_x, b_x, lam):
    xf = x.astype(jnp.float32)
    r_gate = jax.nn.sigmoid(block_diag(x, w_a, b_a).astype(jnp.float32))
    i_gate = jax.nn.sigmoid(block_diag(x, w_x, b_x).astype(jnp.float32))
    log_a = LRU_C * r_gate * jax.nn.log_sigmoid(lam.astype(jnp.float32))
    a = jnp.exp(log_a)
    mult = jnp.sqrt(jnp.maximum(-jnp.expm1(2.0 * log_a), 0.0))
    drive = mult * i_gate * xf

    def combine(lhs, rhs):
        a1, b1 = lhs
        a2, b2 = rhs
        return a1 * a2, a2 * b1 + b2

    _, h = lax.associative_scan(combine, (a, drive), axis=1)
    return h.astype(x.dtype)


def forgetting_attention(q, k, v, log_f):
    B_, S_, H_, Dh = q.shape
    nb = S_ // Q_BLOCK
    F = jnp.cumsum(log_f.astype(jnp.float32), axis=1)
    F_k = jnp.transpose(F, (0, 2, 1))
    q_blocks = q.reshape(B_, nb, Q_BLOCK, H_, Dh).transpose(1, 0, 3, 2, 4)
    F_blocks = F_k.reshape(B_, H_, nb, Q_BLOCK).transpose(2, 0, 1, 3)
    kpos = jnp.arange(S_)
    scale = Dh ** -0.5

    def block(args):
        qi, Fi, bi = args
        s = jnp.einsum('bhqd,bshd->bhqs', qi, k).astype(jnp.float32) * scale
        s = s + Fi[..., None] - F_k[:, :, None, :]
        qpos = bi * Q_BLOCK + jnp.arange(Q_BLOCK)
        mask = kpos[None, :] <= qpos[:, None]
        s = jnp.where(mask, s, -jnp.inf)
        p = jax.nn.softmax(s, axis=-1).astype(v.dtype)
        return jnp.einsum('bhqs,bshd->bqhd', p, v)

    o = lax.map(block, (q_blocks, F_blocks, jnp.arange(nb)))
    return o.transpose(1, 0, 2, 3, 4).reshape(B_, S_, H_ * Dh)


def peer(h, w_q, sub_keys, u_tab, v_tab):
    B_, S_, D_ = h.shape
    q = jnp.einsum('bsd,de->bse', h, w_q).reshape(B_, S_, PEER_HEADS, 2, PEER_HALF)
    sc = jnp.einsum('bshpd,hpnd->bshpn', q, sub_keys).astype(jnp.float32)
    s1, i1 = lax.top_k(sc[..., 0, :], PEER_TOPK)
    s2, i2 = lax.top_k(sc[..., 1, :], PEER_TOPK)
    cand_s = (s1[..., :, None] + s2[..., None, :]).reshape(B_, S_, PEER_HEADS, PEER_TOPK * PEER_TOPK)
    cand_i = (i1[..., :, None] * N_KEYS + i2[..., None, :]).reshape(B_, S_, PEER_HEADS, PEER_TOPK * PEER_TOPK)
    top_s, top_j = lax.top_k(cand_s, PEER_TOPK)
    eidx = jnp.take_along_axis(cand_i, top_j, axis=-1)
    gates = jax.nn.softmax(top_s, axis=-1)
    n_act = PEER_HEADS * PEER_TOPK
    nc = S_ // PEER_CHUNK
    h_c = h.reshape(B_, nc, PEER_CHUNK, D_).transpose(1, 0, 2, 3)
    e_c = eidx.reshape(B_, nc, PEER_CHUNK, n_act).transpose(1, 0, 2, 3)
    g_c = gates.reshape(B_, nc, PEER_CHUNK, n_act).transpose(1, 0, 2, 3).astype(h.dtype)

    def chunk(args):
        hx, ei, gi = args
        u = u_tab[ei]
        vv = v_tab[ei]
        act = jax.nn.gelu(jnp.einsum('bcd,bckd->bck', hx, u)) * gi
        return jnp.einsum('bck,bckd->bcd', act, vv)

    out = lax.map(chunk, (h_c, e_c, g_c))
    return out.transpose(1, 0, 2, 3).reshape(B_, S_, D_)


def even_mixer(h, w_in, w_pool, pool_scale, conv_w, conv_b, w_a, b_a, w_x, b_x, lam, w_out):
    z = jnp.einsum('bsd,de->bse', h, w_in)
    u_pool = z[..., :D_POOL]
    u_lru = z[..., D_POOL:D_POOL + D_LRU]
    u_gate = z[..., D_POOL + D_LRU:]
    y_a = multiscale_pool(u_pool, w_pool, pool_scale)
    y_b = rg_lru(causal_conv(u_lru, conv_w, conv_b), w_a, b_a, w_x, b_x, lam) * jax.nn.gelu(u_gate)
    return jnp.einsum('bse,ed->bsd', jnp.concatenate([y_a, y_b], axis=-1), w_out)


def odd_mixer(h, w_in, b_f, w_out):
    B_, S_, _ = h.shape
    z = jnp.einsum('bsd,de->bse', h, w_in)
    q = z[..., :D_MIX].reshape(B_, S_, N_HEADS_ATTN, HEAD_DIM)
    k = z[..., D_MIX:2 * D_MIX].reshape(B_, S_, N_HEADS_ATTN, HEAD_DIM)
    v = z[..., 2 * D_MIX:3 * D_MIX].reshape(B_, S_, N_HEADS_ATTN, HEAD_DIM)
    log_f = jax.nn.log_sigmoid((z[..., 3 * D_MIX:] + b_f).astype(jnp.float32))
    o = forgetting_attention(q, k, v, log_f)
    return jnp.einsum('bse,ed->bsd', o, w_out)


def setup_inputs(seed: int = 0) -> dict:
    key = jax.random.key(seed)
    ks = jax.random.split(key, 32)
    nrm = lambda k, shape, s: jax.random.normal(k, shape, jnp.float32) * s
    D = D_MODEL
    x = nrm(ks[0], (BATCH, SEQ, D), 1.0)
    c = nrm(ks[1], (BATCH, D), 1.0)
    ada_w = nrm(ks[2], (DEPTH, D, 6 * D), 0.1 * D ** -0.5)
    ada_b = nrm(ks[3], (DEPTH, 6 * D), 0.01)
    ln_g = 1.0 + nrm(ks[4], (DEPTH, 2, D), 0.02)
    ln_b = nrm(ks[5], (DEPTH, 2, D), 0.02)
    peer_wq = nrm(ks[6], (DEPTH, D, PEER_HEADS * PEER_QDIM), D ** -0.5)
    peer_keys = nrm(ks[7], (DEPTH, PEER_HEADS, 2, N_KEYS, PEER_HALF), PEER_HALF ** -0.5)
    peer_u = nrm(ks[8], (DEPTH, N_EXPERTS, D), D ** -0.5)
    peer_v = nrm(ks[9], (DEPTH, N_EXPERTS, D), BETA)
    ev_w_in = nrm(ks[10], (N_EVEN, D, D_POOL + 2 * D_LRU), D ** -0.5)
    ev_w_pool = nrm(ks[11], (N_EVEN, len(POOL_WINDOWS), POOL_GROUP, POOL_GROUP), POOL_GROUP ** -0.5)
    ev_pool_scale = 1.0 + nrm(ks[12], (N_EVEN, D_POOL), 0.1)
    ev_conv_w = nrm(ks[13], (N_EVEN, CONV_WIDTH, D_LRU), CONV_WIDTH ** -0.5)
    ev_conv_b = nrm(ks[14], (N_EVEN, D_LRU), 0.01)
    ev_w_a = nrm(ks[15], (N_EVEN, LRU_BLOCKS, LRU_BLOCK, LRU_BLOCK), LRU_BLOCK ** -0.5)
    ev_b_a = nrm(ks[16], (N_EVEN, D_LRU), 0.01)
    ev_w_x = nrm(ks[17], (N_EVEN, LRU_BLOCKS, LRU_BLOCK, LRU_BLOCK), LRU_BLOCK ** -0.5)
    ev_b_x = nrm(ks[18], (N_EVEN, D_LRU), 0.01)
    a_c = jax.random.uniform(ks[19], (N_EVEN, D_LRU), jnp.float32, 0.9, 0.999)
    a0 = a_c ** (1.0 / LRU_C)
    ev_lam = jnp.log(a0) - jnp.log1p(-a0)
    ev_w_out = nrm(ks[20], (N_EVEN, D_POOL + D_LRU, D), (D_POOL + D_LRU) ** -0.5 * BETA)
    w_qk = nrm(ks[21], (N_ODD, D, 2 * D_MIX), D ** -0.5)
    w_v = nrm(ks[22], (N_ODD, D, D_MIX), D ** -0.5 * BETA)
    w_f = nrm(ks[23], (N_ODD, D, N_HEADS_ATTN), D ** -0.5)
    od_w_in = jnp.concatenate([w_qk, w_v, w_f], axis=-1)
    od_b_f = jax.random.uniform(ks[24], (N_ODD, N_HEADS_ATTN), jnp.float32, 1.0, 5.0)
    od_w_out = nrm(ks[25], (N_ODD, D_MIX, D), D_MIX ** -0.5 * BETA)
    return {"x": x, "c": c, "ada_w": ada_w, "ada_b": ada_b, "ln_g": ln_g, "ln_b": ln_b,
            "peer_wq": peer_wq, "peer_keys": peer_keys, "peer_u": peer_u, "peer_v": peer_v,
            "ev_w_in": ev_w_in, "ev_w_pool": ev_w_pool, "ev_pool_scale": ev_pool_scale,
            "ev_conv_w": ev_conv_w, "ev_conv_b": ev_conv_b, "ev_w_a": ev_w_a, "ev_b_a": ev_b_a,
            "ev_w_x": ev_w_x, "ev_b_x": ev_b_x, "ev_lam": ev_lam, "ev_w_out": ev_w_out,
            "od_w_in": od_w_in, "od_b_f": od_b_f, "od_w_out": od_w_out}


def reference(x, c, ada_w, ada_b, ln_g, ln_b, peer_wq, peer_keys, peer_u, peer_v,
              ev_w_in, ev_w_pool, ev_pool_scale, ev_conv_w, ev_conv_b, ev_w_a, ev_b_a,
              ev_w_x, ev_b_x, ev_lam, ev_w_out, od_w_in, od_b_f, od_w_out):
    c_act = jax.nn.silu(c)
    for l in range(DEPTH):
        mod = (c_act @ ada_w[l] + ada_b[l])[:, None, :]
        sh_m, sc_m, g_m, sh_f, sc_f, g_f = jnp.split(mod, 6, axis=-1)
        h = x * (1.0 + sc_m) + sh_m
        if l % 2 == 0:
            e = l // 2
            y = even_mixer(h, ev_w_in[e], ev_w_pool[e], ev_pool_scale[e], ev_conv_w[e],
                           ev_conv_b[e], ev_w_a[e], ev_b_a[e], ev_w_x[e], ev_b_x[e],
                           ev_lam[e], ev_w_out[e])
        else:
            o = l // 2
            y = odd_mixer(h, od_w_in[o], od_b_f[o], od_w_out[o])
        x = layer_norm(ALPHA * x + (1.0 + g_m) * y, ln_g[l, 0], ln_b[l, 0])
        h = x * (1.0 + sc_f) + sh_f
        y = peer(h, peer_wq[l], peer_keys[l], peer_u[l], peer_v[l])
        x = layer_norm(ALPHA * x + (1.0 + g_f) * y, ln_g[l, 1], ln_b[l, 1])
    return x
```

```python
import functools
import math

import jax
import jax.numpy as jnp
from jax import lax
from jax.experimental import pallas as pl
from jax.experimental.pallas import tpu as pltpu

F32 = jnp.float32
BF16 = jnp.bfloat16
I32 = jnp.int32

LN_EPS = 1e-5
POOL_WINDOWS = (2, 4, 8, 16)
CONV_WIDTH = 4
LRU_BLOCKS = 8
LRU_C = 8.0
N_HEADS_ATTN = 16
PEER_HEADS = 8
PEER_TOPK = 16
N_KEYS = 128

V7X_VMEM_BYTES = 64 * 1024 * 1024
VMEM_LIMIT = 56 * 1024 * 1024
NEG_INF = float("-inf")


def _params(sem):
    return pltpu.CompilerParams(dimension_semantics=sem, vmem_limit_bytes=VMEM_LIMIT)


def _resident(shape, index_map):
    return pl.BlockSpec(shape, index_map, pipeline_mode=pl.Buffered(1))


def _gelu(x):
    c = math.sqrt(2.0 / math.pi)
    return 0.5 * x * (1.0 + jnp.tanh(c * (x + 0.044715 * (x * x * x))))


def _log_sigmoid(x):
    return jnp.minimum(x, 0.0) - jnp.log1p(jnp.exp(-jnp.abs(x)))


def _sigmoid(x):
    return 1.0 / (1.0 + jnp.exp(-x))


def _layer_norm(r, g, b):
    mu = jnp.mean(r, axis=-1, keepdims=True)
    d = r - mu
    var = jnp.mean(d * d, axis=-1, keepdims=True)
    return d * lax.rsqrt(var + LN_EPS) * g + b


def _dot_nt(a, b):
    return lax.dot_general(a, b, (((1,), (1,)), ((), ())), preferred_element_type=F32)


def _ada_kernel(c_ref, w_ref, b_ref, o_ref):
    c = c_ref[...]
    ca = c * _sigmoid(c)
    o_ref[...] = jnp.dot(ca, w_ref[...], preferred_element_type=F32) + b_ref[...]


def ada_mod(c_pad, ada_w, ada_b):
    depth, d, n6 = ada_w.shape
    rows = c_pad.shape[0]
    tn = 1024
    return pl.pallas_call(
        _ada_kernel,
        grid=(depth, n6 // tn),
        in_specs=[
            pl.BlockSpec((rows, d), lambda l, j: (0, 0)),
            pl.BlockSpec((None, d, tn), lambda l, j: (l, 0, j)),
            pl.BlockSpec((None, 1, tn), lambda l, j: (l, 0, j)),
        ],
        out_specs=pl.BlockSpec((None, rows, tn), lambda l, j: (l, 0, j)),
        out_shape=jax.ShapeDtypeStruct((depth, rows, n6), F32),
        compiler_params=_params(("arbitrary", "arbitrary")),
        name="ada_mod",
    )(c_pad, ada_w, ada_b.reshape(depth, 1, n6))


def _mod_matmul_kernel(x_ref, mod_ref, w_ref, o_ref, h_ref, *, shift_row, scale_row):
    @pl.when(pl.program_id(1) == 0)
    def _():
        sh = mod_ref[shift_row:shift_row + 1, :]
        sc = mod_ref[scale_row:scale_row + 1, :]
        h_ref[...] = (x_ref[...] * (1.0 + sc) + sh).astype(BF16)

    o_ref[...] = jnp.dot(h_ref[...], w_ref[...], preferred_element_type=F32).astype(o_ref.dtype)


def mod_matmul(x2d, mod_l, w_bf16, *, seq, shift_row, scale_row, out_dtype, tm=1024, tn=1024):
    n, d = x2d.shape
    nout = w_bf16.shape[1]
    tm = min(tm, seq)
    blocks_per_batch = seq // tm
    kern = functools.partial(_mod_matmul_kernel, shift_row=shift_row, scale_row=scale_row)
    return pl.pallas_call(
        kern,
        grid=(n // tm, nout // tn),
        in_specs=[
            pl.BlockSpec((tm, d), lambda i, j: (i, 0)),
            pl.BlockSpec((None, 6, d), lambda i, j: (i // blocks_per_batch, 0, 0)),
            pl.BlockSpec((d, tn), lambda i, j: (0, j)),
        ],
        out_specs=pl.BlockSpec((tm, tn), lambda i, j: (i, j)),
        out_shape=jax.ShapeDtypeStruct((n, nout), out_dtype),
        scratch_shapes=[pltpu.VMEM((tm, d), BF16)],
        compiler_params=_params(("arbitrary", "arbitrary")),
        name="mod_matmul",
    )(x2d, mod_l, w_bf16)


def _cumsum_rows(x):
    rows = x.shape[0]
    ridx = lax.broadcasted_iota(I32, x.shape, 0)
    d = 1
    while d < rows:
        x = x + jnp.where(ridx >= d, pltpu.roll(x, d, 0), 0.0)
        d *= 2
    return x


def _odd_in_kernel(x_ref, mod_ref, w_ref, wfh_ref, wfl_ref, bf_ref, o_ref, f_ref,
                   h_ref, carry_ref, *, blocks_per_batch, d_mix, q_scale, tn):
    i = pl.program_id(0)
    j = pl.program_id(1)

    @pl.when(j == 0)
    def _():
        sh = mod_ref[0:1, :]
        sc = mod_ref[1:2, :]
        h = x_ref[...] * (1.0 + sc) + sh
        h_hi = h.astype(BF16)
        h_ref[...] = h_hi
        h_lo = (h - h_hi.astype(F32)).astype(BF16)
        zf = (jnp.dot(h_hi, wfh_ref[...], preferred_element_type=F32)
              + jnp.dot(h_lo, wfh_ref[...], preferred_element_type=F32)
              + jnp.dot(h_hi, wfl_ref[...], preferred_element_type=F32))
        logf = _log_sigmoid(zf + bf_ref[...])

        @pl.when(i % blocks_per_batch == 0)
        def _():
            carry_ref[...] = jnp.zeros_like(carry_ref)

        cs = _cumsum_rows(logf) + carry_ref[0:1, :]
        f_ref[...] = cs
        carry_ref[...] = jnp.broadcast_to(cs[cs.shape[0] - 1:, :], carry_ref.shape)

    z = jnp.dot(h_ref[...], w_ref[...], preferred_element_type=F32)
    scale = jnp.where(j * tn < d_mix, q_scale, 1.0)
    o_ref[...] = (z * scale).astype(o_ref.dtype)


def odd_in(x2d, mod_l, w_qkv, wf_hi, wf_lo, b_f_pad, *, seq, tm=1024, tn=1024):
    n, d = x2d.shape
    nout = w_qkv.shape[1]
    d_mix = nout // 3
    tm = min(tm, seq)
    blocks_per_batch = seq // tm
    head_dim = d_mix // N_HEADS_ATTN
    kern = functools.partial(_odd_in_kernel, blocks_per_batch=blocks_per_batch, d_mix=d_mix,
                             q_scale=head_dim ** -0.5, tn=tn)
    return pl.pallas_call(
        kern,
        grid=(n // tm, nout // tn),
        in_specs=[
            pl.BlockSpec((tm, d), lambda i, j: (i, 0)),
            pl.BlockSpec((None, 6, d), lambda i, j: (i // blocks_per_batch, 0, 0)),
            pl.BlockSpec((d, tn), lambda i, j: (0, j)),
            _resident((d, 128), lambda i, j: (0, 0)),
            _resident((d, 128), lambda i, j: (0, 0)),
            _resident((1, 128), lambda i, j: (0, 0)),
        ],
        out_specs=[
            pl.BlockSpec((tm, tn), lambda i, j: (i, j)),
            pl.BlockSpec((tm, 128), lambda i, j: (i, 0)),
        ],
        out_shape=[
            jax.ShapeDtypeStruct((n, nout), BF16),
            jax.ShapeDtypeStruct((n, 128), F32),
        ],
        scratch_shapes=[pltpu.VMEM((tm, d), BF16), pltpu.VMEM((8, 128), F32)],
        compiler_params=_params(("arbitrary", "arbitrary")),
        name="odd_in",
    )(x2d, mod_l, w_qkv, wf_hi, wf_lo, b_f_pad)


def _attn_kernel(q_ref, k_ref, v_ref, fk_ref, o_ref, *, tq):
    qi = pl.program_id(2)
    q = q_ref[...]
    f_ref0 = fk_ref[0:1, pl.ds(pl.multiple_of(qi * tq, tq), tq)][:, 0:1]
    dh = q.shape[1]

    def chunk(c, carry, masked):
        m, l, acc = carry
        start = pl.multiple_of(c * tq, tq)
        k = k_ref[pl.ds(start, tq), :]
        v = v_ref[pl.ds(start, tq), :]
        s = _dot_nt(q, k) + (f_ref0 - fk_ref[0:1, pl.ds(start, tq)])
        if masked:
            row = lax.broadcasted_iota(I32, s.shape, 0)
            col = lax.broadcasted_iota(I32, s.shape, 1)
            s = jnp.where(col <= row, s, NEG_INF)
        m_new = jnp.maximum(m, jnp.max(s, axis=-1, keepdims=True))
        alpha = jnp.exp(m - m_new)
        p = jnp.exp(s - m_new)
        l = alpha * l + jnp.sum(p, axis=-1, keepdims=True)
        acc = alpha * acc + jnp.dot(p.astype(BF16), v, preferred_element_type=F32)
        return m_new, l, acc

    init = (jnp.full((tq, 1), NEG_INF, F32), jnp.zeros((tq, 1), F32), jnp.zeros((tq, dh), F32))
    carry = lax.fori_loop(0, qi, lambda c, cr: chunk(c, cr, False), init)
    m, l, acc = chunk(qi, carry, True)
    o_ref[...] = (acc / l).astype(o_ref.dtype)


def attention(qkv, f_keys, *, batch, seq, tq=512):
    n = qkv.shape[0]
    d_mix = qkv.shape[1] // 3
    heads = N_HEADS_ATTN
    dh = d_mix // heads
    tq = min(tq, seq)
    nq = seq // tq
    kern = functools.partial(_attn_kernel, tq=tq)
    return pl.pallas_call(
        kern,
        grid=(batch, heads, nq),
        in_specs=[
            pl.BlockSpec((tq, dh), lambda b, h, i: (b * nq + i, h)),
            pl.BlockSpec((seq, dh), lambda b, h, i: (b, heads + h)),
            pl.BlockSpec((seq, dh), lambda b, h, i: (b, 2 * heads + h)),
            pl.BlockSpec((None, 1, seq), lambda b, h, i: (b * heads + h, 0, 0)),
        ],
        out_specs=pl.BlockSpec((tq, dh), lambda b, h, i: (b * nq + i, h)),
        out_shape=jax.ShapeDtypeStruct((n, d_mix), BF16),
        compiler_params=_params(("arbitrary", "arbitrary", "arbitrary")),
        name="attention",
    )(qkv, qkv, qkv, f_keys)


HALO = 16


def _neg_expm1(x):
    p = 1.0 + x / 10.0
    for k in range(9, 1, -1):
        p = 1.0 + (x / k) * p
    return jnp.where(x > -0.35, -(x * p), 1.0 - jnp.exp(x))


def _even_mix_kernel(z_ref, wpool_ref, pscale_ref, convw_ref, convb_ref, wa_ref, ba_ref,
                     wx_ref, bx_ref, lam_ref, o_ref, pbuf, cbuf, hstate, *, ts, d_pool, d_lru):
    sb = pl.program_id(1)
    group = d_pool // len(POOL_WINDOWS)
    blk = d_lru // LRU_BLOCKS

    @pl.when(sb == 0)
    def _():
        pbuf[0:HALO, :] = jnp.zeros((HALO, d_pool), F32)
        cbuf[0:HALO, :] = jnp.zeros((HALO, d_lru), F32)
        hstate[...] = jnp.zeros_like(hstate)

    up = z_ref[:, 0:d_pool]
    pbuf[HALO:HALO + ts, :] = up
    pos = sb * ts + lax.broadcasted_iota(I32, (ts, group), 0)
    ya = []
    for g, w in enumerate(POOL_WINDOWS):
        lo, hi = g * group, (g + 1) * group
        need = w - 1
        cur = pbuf[HALO - need:HALO + ts, lo:hi]
        span = 1
        while span < w:
            rows = cur.shape[0] - span
            cur = cur[span:span + rows, :] + cur[0:rows, :]
            span *= 2
        cnt = jnp.minimum(pos + 1, w).astype(F32)
        pooled = cur / cnt - up[:, lo:hi]
        y = jnp.dot(pooled.astype(BF16), wpool_ref[g], preferred_element_type=F32)
        ya.append(y * pscale_ref[0:1, lo:hi])
    o_ref[:, 0:d_pool] = jnp.concatenate(ya, axis=-1).astype(o_ref.dtype)
    pbuf[0:HALO, :] = pbuf[ts:ts + HALO, :]

    ul = z_ref[:, d_pool:d_pool + d_lru]
    ug = z_ref[:, d_pool + d_lru:d_pool + 2 * d_lru]
    cbuf[HALO:HALO + ts, :] = ul
    xc = jnp.broadcast_to(convb_ref[0:1, :], (ts, d_lru))
    for k in range(CONV_WIDTH):
        off = HALO - (CONV_WIDTH - 1) + k
        xc = xc + cbuf[off:off + ts, :] * convw_ref[k:k + 1, :]
    cbuf[0:HALO, :] = cbuf[ts:ts + HALO, :]

    xb = xc.astype(BF16)
    ra, ia = [], []
    for hb in range(LRU_BLOCKS):
        xs = xb[:, hb * blk:(hb + 1) * blk]
        ra.append(jnp.dot(xs, wa_ref[hb], preferred_element_type=F32))
        ia.append(jnp.dot(xs, wx_ref[hb], preferred_element_type=F32))
    r_gate = _sigmoid(jnp.concatenate(ra, axis=-1) + ba_ref[0:1, :])
    i_gate = _sigmoid(jnp.concatenate(ia, axis=-1) + bx_ref[0:1, :])
    log_a = LRU_C * r_gate * _log_sigmoid(lam_ref[0:1, :])
    a = jnp.exp(log_a)
    mult = jnp.sqrt(jnp.maximum(_neg_expm1(2.0 * log_a), 0.0))
    b = mult * i_gate * xc

    ridx = lax.broadcasted_iota(I32, (ts, d_lru), 0)
    d = 1
    while d < ts:
        keep = ridx >= d
        a_sh = jnp.where(keep, pltpu.roll(a, d, 0), 1.0)
        b_sh = jnp.where(keep, pltpu.roll(b, d, 0), 0.0)
        b = a * b_sh + b
        a = a * a_sh
        d *= 2
    h = b + a * hstate[0:1, :]
    hstate[...] = jnp.broadcast_to(h[ts - 1:ts, :], hstate.shape)
    o_ref[:, d_pool:d_pool + d_lru] = (h * _gelu(ug)).astype(o_ref.dtype)


def even_mix(z3d, w_pool, pool_scale, conv_w, conv_b, w_a, b_a, w_x, b_x, lam, *, ts=256):
    batch, seq, dz = z3d.shape
    d_pool = pool_scale.shape[-1]
    d_lru = lam.shape[-1]
    ts = min(ts, seq)
    group = d_pool // len(POOL_WINDOWS)
    blk = d_lru // LRU_BLOCKS
    kern = functools.partial(_even_mix_kernel, ts=ts, d_pool=d_pool, d_lru=d_lru)
    const2 = lambda b, s: (0, 0)
    const3 = lambda b, s: (0, 0, 0)
    return pl.pallas_call(
        kern,
        grid=(batch, seq // ts),
        in_specs=[
            pl.BlockSpec((None, ts, dz), lambda b, s: (b, s, 0)),
            _resident((len(POOL_WINDOWS), group, group), const3),
            _resident((1, d_pool), const2),
            _resident((CONV_WIDTH, d_lru), const2),
            _resident((1, d_lru), const2),
            _resident((LRU_BLOCKS, blk, blk), const3),
            _resident((1, d_lru), const2),
            _resident((LRU_BLOCKS, blk, blk), const3),
            _resident((1, d_lru), const2),
            _resident((1, d_lru), const2),
        ],
        out_specs=pl.BlockSpec((None, ts, d_pool + d_lru), lambda b, s: (b, s, 0)),
        out_shape=jax.ShapeDtypeStruct((batch, seq, d_pool + d_lru), BF16),
        scratch_shapes=[
            pltpu.VMEM((HALO + ts, d_pool), F32),
            pltpu.VMEM((HALO + ts, d_lru), F32),
            pltpu.VMEM((8, d_lru), F32),
        ],
        compiler_params=_params(("arbitrary", "arbitrary")),
        name="even_mix",
    )(z3d, w_pool, pool_scale, conv_w, conv_b, w_a, b_a, w_x, b_x, lam)


def _proj_ln_kernel(a_ref, w_ref, x_ref, mod_ref, g_ref, b_ref, o_ref, *, alpha, gate_row):
    y = jnp.dot(a_ref[...], w_ref[...], preferred_element_type=F32)
    gate = mod_ref[gate_row:gate_row + 1, :]
    r = alpha * x_ref[...] + (1.0 + gate) * y
    o_ref[...] = _layer_norm(r, g_ref[...], b_ref[...])


def proj_ln(a2d, w_bf16, x2d, mod_l, ln_g, ln_b, *, seq, alpha, gate_row, tm=512):
    n, d = x2d.shape
    k = a2d.shape[1]
    tm = min(tm, seq)
    blocks_per_batch = seq // tm
    kern = functools.partial(_proj_ln_kernel, alpha=alpha, gate_row=gate_row)
    return pl.pallas_call(
        kern,
        grid=(n // tm,),
        in_specs=[
            pl.BlockSpec((tm, k), lambda i: (i, 0)),
            _resident((k, d), lambda i: (0, 0)),
            pl.BlockSpec((tm, d), lambda i: (i, 0)),
            pl.BlockSpec((None, 6, d), lambda i: (i // blocks_per_batch, 0, 0)),
            _resident((1, d), lambda i: (0, 0)),
            _resident((1, d), lambda i: (0, 0)),
        ],
        out_specs=pl.BlockSpec((tm, d), lambda i: (i, 0)),
        out_shape=jax.ShapeDtypeStruct((n, d), F32),
        compiler_params=_params(("arbitrary",)),
        name="proj_ln",
    )(a2d, w_bf16, x2d, mod_l, ln_g, ln_b)


def _top16_rows(s, kidx, sentinel):
    t = s.shape[1]
    r16 = lax.broadcasted_iota(I32, (PEER_TOPK, t), 0)
    vals = jnp.zeros((PEER_TOPK, t), F32)
    idxs = jnp.zeros((PEER_TOPK, t), I32)
    for r in range(PEER_TOPK):
        m = jnp.max(s, axis=0, keepdims=True)
        am = jnp.min(jnp.where(s == m, kidx, sentinel), axis=0, keepdims=True)
        s = jnp.where(kidx == am, NEG_INF, s)
        vals = jnp.where(r16 == r, m, vals)
        idxs = jnp.where(r16 == r, am, idxs)
    return vals, idxs


def _take_rows16(table, sel):
    out = jnp.zeros(sel.shape, table.dtype)
    for s in range(PEER_TOPK):
        out = jnp.where(sel == s, table[s:s + 1, :], out)
    return out


def _peer_route_kernel(x_ref, mod_ref, wq_ref, keys_ref, h_ref, g_ref,
                       sc_ref, val_ref, idx_ref, e_ref, w_ref, et_ref, wt_ref, *, tr):
    nhp = 2 * PEER_HEADS
    sh = mod_ref[3:4, :]
    sc = mod_ref[4:5, :]
    h = (x_ref[...] * (1.0 + sc) + sh).astype(BF16)
    h_ref[...] = h
    q = jnp.dot(h, wq_ref[...], preferred_element_type=F32).astype(BF16)
    for hp in range(nhp):
        sc_ref[hp] = _dot_nt(keys_ref[hp], q[:, hp * N_KEYS:(hp + 1) * N_KEYS])

    kidx = lax.broadcasted_iota(I32, (N_KEYS, tr), 0)

    def first_topk(hp, carry):
        vals, idxs = _top16_rows(sc_ref[hp], kidx, N_KEYS)
        val_ref[hp] = vals
        idx_ref[hp] = idxs
        return carry

    lax.fori_loop(0, nhp, first_topk, 0)

    i16 = lax.broadcasted_iota(I32, (16, tr), 0)
    i8 = lax.broadcasted_iota(I32, (8, tr), 0)
    flat = jnp.concatenate([
        i16 * 16, i8 * 16 + 1, i8 * 16 + 2, i8 * 16 + 3,
        i16, 16 + i8, 32 + i8,
    ], axis=0)
    valid = jnp.concatenate([
        i16 < 16, i8 < 8, i8 < 5, i8 < 4,
        i16 >= 4, i8 >= 4, i8 == 4,
    ], axis=0)

    def second_topk(hd, carry):
        s1 = val_ref[2 * hd]
        s2 = val_ref[2 * hd + 1]
        cand = jnp.concatenate([
            s1 + s2[0:1, :], s1[0:8, :] + s2[1:2, :], s1[0:8, :] + s2[2:3, :], s1[0:8, :] + s2[3:4, :],
            s1[0:1, :] + s2, s1[1:2, :] + s2[0:8, :], s1[2:3, :] + s2[0:8, :],
        ], axis=0)
        cand = jnp.where(valid, cand, NEG_INF)
        top, fsel = _top16_rows(cand, flat, 256)
        e = jnp.exp(top - top[0:1, :])
        gate = e / jnp.sum(e, axis=0, keepdims=True)
        a = _take_rows16(idx_ref[2 * hd], fsel >> 4)
        b = _take_rows16(idx_ref[2 * hd + 1], fsel & 15)
        row0 = pl.multiple_of(hd * PEER_TOPK, PEER_TOPK)
        e_ref[pl.ds(row0, PEER_TOPK), :] = a * N_KEYS + b
        w_ref[pl.ds(row0, PEER_TOPK), :] = gate
        return carry

    lax.fori_loop(0, PEER_HEADS, second_topk, 0)

    n_act = PEER_HEADS * PEER_TOPK
    for c in range(tr // n_act):
        et_ref[c * n_act:(c + 1) * n_act, :] = e_ref[:, c * n_act:(c + 1) * n_act].T
        wt_ref[c * n_act:(c + 1) * n_act, :] = w_ref[:, c * n_act:(c + 1) * n_act].T

    iota_rows = lax.broadcasted_iota(I32, (N_KEYS, n_act), 0)

    def build(n, carry):
        erow = et_ref[pl.ds(n, 1), :]
        wrow = wt_ref[pl.ds(n, 1), :]
        pt = jnp.where(iota_rows == (erow >> 7), wrow, 0.0).astype(BF16)
        qt = jnp.where(iota_rows == (erow & (N_KEYS - 1)), 1.0, 0.0).astype(BF16)
        g_ref[n] = _dot_nt(pt, qt).astype(g_ref.dtype)
        return carry

    lax.fori_loop(0, tr, build, 0)


def peer_route(x2d, mod_l, wq_bf16, keys_bf16, *, seq, tr=256):
    n, d = x2d.shape
    tr = min(tr, seq)
    blocks_per_batch = seq // tr
    nhp = 2 * PEER_HEADS
    n_act = PEER_HEADS * PEER_TOPK
    kern = functools.partial(_peer_route_kernel, tr=tr)
    return pl.pallas_call(
        kern,
        grid=(n // tr,),
        in_specs=[
            pl.BlockSpec((tr, d), lambda i: (i, 0)),
            pl.BlockSpec((None, 6, d), lambda i: (i // blocks_per_batch, 0, 0)),
            _resident(wq_bf16.shape, lambda i: (0, 0)),
            _resident(keys_bf16.shape, lambda i: (0, 0, 0)),
        ],
        out_specs=[
            pl.BlockSpec((tr, d), lambda i: (i, 0)),
            pl.BlockSpec((tr, N_KEYS, N_KEYS), lambda i: (i, 0, 0)),
        ],
        out_shape=[
            jax.ShapeDtypeStruct((n, d), BF16),
            jax.ShapeDtypeStruct((n, N_KEYS, N_KEYS), BF16),
        ],
        scratch_shapes=[
            pltpu.VMEM((nhp, N_KEYS, tr), F32),
            pltpu.VMEM((nhp, PEER_TOPK, tr), F32),
            pltpu.VMEM((nhp, PEER_TOPK, tr), I32),
            pltpu.VMEM((n_act, tr), I32),
            pltpu.VMEM((n_act, tr), F32),
            pltpu.VMEM((tr, n_act), I32),
            pltpu.VMEM((tr, n_act), F32),
        ],
        compiler_params=_params(("arbitrary",)),
        name="peer_route",
    )(x2d, mod_l, wq_bf16, keys_bf16)


def _peer_dense_kernel(h_ref, g_ref, u_ref, v_ref, x_ref, mod_ref, lng_ref, lnb_ref, o_ref,
                       acc_ref, *, alpha):
    j = pl.program_id(1)

    @pl.when(j == 0)
    def _():
        acc_ref[...] = jnp.zeros_like(acc_ref)

    z = _dot_nt(h_ref[...], u_ref[...])
    act = (_gelu(z) * g_ref[...].astype(F32)).astype(BF16)
    acc_ref[...] += jnp.dot(act, v_ref[...], preferred_element_type=F32)

    @pl.when(j == pl.num_programs(1) - 1)
    def _():
        gate = mod_ref[5:6, :]
        r = alpha * x_ref[...] + (1.0 + gate) * acc_ref[...]
        o_ref[...] = _layer_norm(r, lng_ref[...], lnb_ref[...])


def peer_dense(h2d, g2d, u_bf16, v_bf16, x2d, mod_l, ln_g, ln_b, *, seq, alpha, tm=512, te=1024):
    n, d = x2d.shape
    n_exp = u_bf16.shape[0]
    tm = min(tm, seq)
    blocks_per_batch = seq // tm
    kern = functools.partial(_peer_dense_kernel, alpha=alpha)
    return pl.pallas_call(
        kern,
        grid=(n // tm, n_exp // te),
        in_specs=[
            pl.BlockSpec((tm, d), lambda i, j: (i, 0)),
            pl.BlockSpec((tm, te), lambda i, j: (i, j)),
            pl.BlockSpec((te, d), lambda i, j: (j, 0)),
            pl.BlockSpec((te, d), lambda i, j: (j, 0)),
            pl.BlockSpec((tm, d), lambda i, j: (i, 0)),
            pl.BlockSpec((None, 6, d), lambda i, j: (i // blocks_per_batch, 0, 0)),
            _resident((1, d), lambda i, j: (0, 0)),
            _resident((1, d), lambda i, j: (0, 0)),
        ],
        out_specs=pl.BlockSpec((tm, d), lambda i, j: (i, 0)),
        out_shape=jax.ShapeDtypeStruct((n, d), F32),
        scratch_shapes=[pltpu.VMEM((tm, d), F32)],
        compiler_params=_params(("arbitrary", "arbitrary")),
        name="peer_dense",
    )(h2d, g2d, u_bf16, v_bf16, x2d, mod_l, ln_g, ln_b)


def kernel(x, c, ada_w, ada_b, ln_g, ln_b, peer_wq, peer_keys, peer_u, peer_v, ev_w_in, ev_w_pool,
           ev_pool_scale, ev_conv_w, ev_conv_b, ev_w_a, ev_b_a, ev_w_x, ev_b_x, ev_lam, ev_w_out,
           od_w_in, od_b_f, od_w_out):
    batch, seq, d = x.shape
    depth = ada_w.shape[0]
    n = batch * seq
    alpha = (2.0 * depth) ** 0.25
    d_mix = od_w_out.shape[1]

    c_pad = jnp.zeros((8, d), F32).at[:batch].set(c)
    mod = ada_mod(c_pad, ada_w, ada_b)[:, :batch].reshape(depth, batch, 6, d)

    xf = x.reshape(n, d)
    for l in range(depth):
        mod_l = mod[l]
        g0 = ln_g[l, 0].reshape(1, d)
        b0 = ln_b[l, 0].reshape(1, d)
        g1 = ln_g[l, 1].reshape(1, d)
        b1 = ln_b[l, 1].reshape(1, d)
        if l % 2 == 0:
            e = l // 2
            z = mod_matmul(xf, mod_l, ev_w_in[e].astype(BF16), seq=seq, shift_row=0, scale_row=1,
                           out_dtype=F32)
            mixed = even_mix(
                z.reshape(batch, seq, -1), ev_w_pool[e].astype(BF16), ev_pool_scale[e].reshape(1, -1),
                ev_conv_w[e], ev_conv_b[e].reshape(1, -1), ev_w_a[e].astype(BF16),
                ev_b_a[e].reshape(1, -1), ev_w_x[e].astype(BF16), ev_b_x[e].reshape(1, -1),
                ev_lam[e].reshape(1, -1))
            xf = proj_ln(mixed.reshape(n, -1), ev_w_out[e].astype(BF16), xf, mod_l, g0, b0,
                         seq=seq, alpha=alpha, gate_row=2)
        else:
            o = l // 2
            w_in = od_w_in[o]
            w_f = jnp.zeros((d, 128), F32).at[:, :N_HEADS_ATTN].set(w_in[:, 3 * d_mix:])
            wf_hi = w_f.astype(BF16)
            wf_lo = (w_f - wf_hi.astype(F32)).astype(BF16)
            b_f = jnp.zeros((1, 128), F32).at[0, :N_HEADS_ATTN].set(od_b_f[o])
            qkv, f_cum = odd_in(xf, mod_l, w_in[:, :3 * d_mix].astype(BF16), wf_hi, wf_lo, b_f, seq=seq)
            f_keys = f_cum[:, :N_HEADS_ATTN].reshape(batch, seq, N_HEADS_ATTN)
            f_keys = jnp.transpose(f_keys, (0, 2, 1)).reshape(batch * N_HEADS_ATTN, 1, seq)
            attn = attention(qkv, f_keys, batch=batch, seq=seq)
            xf = proj_ln(attn, od_w_out[o].astype(BF16), xf, mod_l, g0, b0,
                         seq=seq, alpha=alpha, gate_row=2)

        keys = peer_keys[l].reshape(2 * PEER_HEADS, N_KEYS, -1).astype(BF16)
        h2, gmat = peer_route(xf, mod_l, peer_wq[l].astype(BF16), keys, seq=seq)
        xf = peer_dense(h2, gmat.reshape(n, N_KEYS * N_KEYS), peer_u[l].astype(BF16),
                        peer_v[l].astype(BF16), xf, mod_l, g1, b1, seq=seq, alpha=alpha)
    return xf.reshape(batch, seq, d)
```

```python
import functools
import math

import jax
import jax.numpy as jnp
from jax import lax
from jax.experimental import pallas as pl
from jax.experimental.pallas import tpu as pltpu

F32 = jnp.float32
BF16 = jnp.bfloat16
I32 = jnp.int32

LN_EPS = 1e-5
POOL_WINDOWS = (2, 4, 8, 16)
CONV_WIDTH = 4
LRU_BLOCKS = 8
LRU_C = 8.0
N_HEADS_ATTN = 16
PEER_HEADS = 8
PEER_TOPK = 16
N_KEYS = 128

V7X_VMEM_BYTES = 64 * 1024 * 1024
VMEM_LIMIT = 56 * 1024 * 1024
NEG_INF = float("-inf")


def _params(sem):
    return pltpu.CompilerParams(dimension_semantics=sem, vmem_limit_bytes=VMEM_LIMIT)


def _resident(shape, index_map):
    return pl.BlockSpec(shape, index_map, pipeline_mode=pl.Buffered(1))


def _gelu(x):
    c = math.sqrt(2.0 / math.pi)
    return 0.5 * x * (1.0 + jnp.tanh(c * (x + 0.044715 * (x * x * x))))


def _log_sigmoid(x):
    return jnp.minimum(x, 0.0) - jnp.log1p(jnp.exp(-jnp.abs(x)))


def _sigmoid(x):
    return 1.0 / (1.0 + jnp.exp(-x))


def _layer_norm(r, g, b):
    mu = jnp.mean(r, axis=-1, keepdims=True)
    d = r - mu
    var = jnp.mean(d * d, axis=-1, keepdims=True)
    return d * lax.rsqrt(var + LN_EPS) * g + b


def _dot_nt(a, b):
    return lax.dot_general(a, b, (((1,), (1,)), ((), ())), preferred_element_type=F32)


def _ada_kernel(c_ref, w_ref, b_ref, o_ref):
    c = c_ref[...]
    ca = c * _sigmoid(c)
    o_ref[...] = jnp.dot(ca, w_ref[...], preferred_element_type=F32) + b_ref[...]


def ada_mod(c_pad, ada_w, ada_b):
    depth, d, n6 = ada_w.shape
    rows = c_pad.shape[0]
    tn = 1024
    return pl.pallas_call(
        _ada_kernel,
        grid=(depth, n6 // tn),
        in_specs=[
            pl.BlockSpec((rows, d), lambda l, j: (0, 0)),
            pl.BlockSpec((None, d, tn), lambda l, j: (l, 0, j)),
            pl.BlockSpec((None, 1, tn), lambda l, j: (l, 0, j)),
        ],
        out_specs=pl.BlockSpec((None, rows, tn), lambda l, j: (l, 0, j)),
        out_shape=jax.ShapeDtypeStruct((depth, rows, n6), F32),
        compiler_params=_params(("arbitrary", "arbitrary")),
        name="ada_mod",
    )(c_pad, ada_w, ada_b.reshape(depth, 1, n6))


def _mod_matmul_kernel(x_ref, mod_ref, w_ref, o_ref, h_ref, *, shift_row, scale_row):
    @pl.when(pl.program_id(1) == 0)
    def _():
        sh = mod_ref[shift_row:shift_row + 1, :]
        sc = mod_ref[scale_row:scale_row + 1, :]
        h_ref[...] = (x_ref[...] * (1.0 + sc) + sh).astype(BF16)

    o_ref[...] = jnp.dot(h_ref[...], w_ref[...], preferred_element_type=F32).astype(o_ref.dtype)


def mod_matmul(x2d, mod_l, w_bf16, *, seq, shift_row, scale_row, out_dtype, tm=1024, tn=1024):
    n, d = x2d.shape
    nout = w_bf16.shape[1]
    tm = min(tm, seq)
    blocks_per_batch = seq // tm
    kern = functools.partial(_mod_matmul_kernel, shift_row=shift_row, scale_row=scale_row)
    return pl.pallas_call(
        kern,
        grid=(n // tm, nout // tn),
        in_specs=[
            pl.BlockSpec((tm, d), lambda i, j: (i, 0)),
            pl.BlockSpec((None, 6, d), lambda i, j: (i // blocks_per_batch, 0, 0)),
            pl.BlockSpec((d, tn), lambda i, j: (0, j)),
        ],
        out_specs=pl.BlockSpec((tm, tn), lambda i, j: (i, j)),
        out_shape=jax.ShapeDtypeStruct((n, nout), out_dtype),
        scratch_shapes=[pltpu.VMEM((tm, d), BF16)],
        compiler_params=_params(("arbitrary", "arbitrary")),
        name="mod_matmul",
    )(x2d, mod_l, w_bf16)


def _cumsum_rows(x):
    rows = x.shape[0]
    ridx = lax.broadcasted_iota(I32, x.shape, 0)
    d = 1
    while d < rows:
        x = x + jnp.where(ridx >= d, pltpu.roll(x, d, 0), 0.0)
        d *= 2
    return x


def _odd_in_kernel(x_ref, mod_ref, w_ref, wfh_ref, wfl_ref, bf_ref, o_ref, f_ref,
                   h_ref, carry_ref, *, blocks_per_batch, d_mix, q_scale, tn):
    i = pl.program_id(0)
    j = pl.program_id(1)

    @pl.when(j == 0)
    def _():
        sh = mod_ref[0:1, :]
        sc = mod_ref[1:2, :]
        h = x_ref[...] * (1.0 + sc) + sh
        h_hi = h.astype(BF16)
        h_ref[...] = h_hi
        h_lo = (h - h_hi.astype(F32)).astype(BF16)
        zf = (jnp.dot(h_hi, wfh_ref[...], preferred_element_type=F32)
              + jnp.dot(h_lo, wfh_ref[...], preferred_element_type=F32)
              + jnp.dot(h_hi, wfl_ref[...], preferred_element_type=F32))
        logf = _log_sigmoid(zf + bf_ref[...])

        @pl.when(i % blocks_per_batch == 0)
        def _():
            carry_ref[...] = jnp.zeros_like(carry_ref)

        cs = _cumsum_rows(logf) + carry_ref[0:1, :]
        f_ref[...] = cs
        carry_ref[...] = jnp.broadcast_to(cs[cs.shape[0] - 1:, :], carry_ref.shape)

    z = jnp.dot(h_ref[...], w_ref[...], preferred_element_type=F32)
    scale = jnp.where(j * tn < d_mix, q_scale, 1.0)
    o_ref[...] = (z * scale).astype(o_ref.dtype)


def odd_in(x2d, mod_l, w_qkv, wf_hi, wf_lo, b_f_pad, *, seq, tm=1024, tn=1024):
    n, d = x2d.shape
    nout = w_qkv.shape[1]
    d_mix = nout // 3
    tm = min(tm, seq)
    blocks_per_batch = seq // tm
    head_dim = d_mix // N_HEADS_ATTN
    kern = functools.partial(_odd_in_kernel, blocks_per_batch=blocks_per_batch, d_mix=d_mix,
                             q_scale=head_dim ** -0.5, tn=tn)
    return pl.pallas_call(
        kern,
        grid=(n // tm, nout // tn),
        in_specs=[
            pl.BlockSpec((tm, d), lambda i, j: (i, 0)),
            pl.BlockSpec((None, 6, d), lambda i, j: (i // blocks_per_batch, 0, 0)),
            pl.BlockSpec((d, tn), lambda i, j: (0, j)),
            _resident((d, 128), lambda i, j: (0, 0)),
            _resident((d, 128), lambda i, j: (0, 0)),
            _resident((1, 128), lambda i, j: (0, 0)),
        ],
        out_specs=[
            pl.BlockSpec((tm, tn), lambda i, j: (i, j)),
            pl.BlockSpec((tm, 128), lambda i, j: (i, 0)),
        ],
        out_shape=[
            jax.ShapeDtypeStruct((n, nout), BF16),
            jax.ShapeDtypeStruct((n, 128), F32),
        ],
        scratch_shapes=[pltpu.VMEM((tm, d), BF16), pltpu.VMEM((8, 128), F32)],
        compiler_params=_params(("arbitrary", "arbitrary")),
        name="odd_in",
    )(x2d, mod_l, w_qkv, wf_hi, wf_lo, b_f_pad)


def _attn_kernel(q_ref, k_ref, v_ref, fk_ref, o_ref, sa_ref, sb_ref, *, tq):
    qi = pl.program_id(2)
    f_ref0 = fk_ref[0:1, pl.ds(pl.multiple_of(qi * tq, tq), tq)][:, 0:1]
    dh = q_ref.shape[1]

    def scores(c, s_ref):
        start = pl.multiple_of(c * tq, tq)
        s = _dot_nt(q_ref[...], k_ref[pl.ds(start, tq), :])
        s_ref[...] = s + (f_ref0 - fk_ref[0:1, pl.ds(start, tq)])

    def update(c, s_ref, carry, diagonal):
        m, l, acc = carry
        start = pl.multiple_of(c * tq, tq)
        if diagonal:
            row = lax.broadcasted_iota(I32, (tq, tq), 0)
            col = lax.broadcasted_iota(I32, (tq, tq), 1)
            s_ref[...] = jnp.where(col <= row, s_ref[...], NEG_INF)
        m_new = jnp.maximum(m, jnp.max(s_ref[...], axis=-1, keepdims=True))
        alpha = jnp.exp(m - m_new)
        p = jnp.exp(s_ref[...] - m_new)
        l = alpha * l + jnp.sum(p, axis=-1, keepdims=True)
        v = v_ref[pl.ds(start, tq), :]
        acc = alpha * acc + jnp.dot(p.astype(BF16), v, preferred_element_type=F32)
        return m_new, l, acc

    scores(0, sa_ref)

    def pair(p, carry):
        scores(2 * p + 1, sb_ref)
        carry = update(2 * p, sa_ref, carry, False)
        scores(2 * p + 2, sa_ref)
        return update(2 * p + 1, sb_ref, carry, False)

    def odd_tail(carry):
        scores(qi, sb_ref)
        carry = update(qi - 1, sa_ref, carry, False)
        return update(qi, sb_ref, carry, True)

    def even_tail(carry):
        return update(qi, sa_ref, carry, True)

    init = (jnp.full((tq, 1), NEG_INF, F32), jnp.zeros((tq, 1), F32), jnp.zeros((tq, dh), F32))
    carry = lax.fori_loop(0, qi // 2, pair, init)
    m, l, acc = lax.cond(qi % 2 == 1, odd_tail, even_tail, carry)
    o_ref[...] = (acc / l).astype(o_ref.dtype)


def attention(qkv, f_keys, *, batch, seq, tq=512):
    n = qkv.shape[0]
    d_mix = qkv.shape[1] // 3
    heads = N_HEADS_ATTN
    dh = d_mix // heads
    tq = min(tq, seq)
    nq = seq // tq
    kern = functools.partial(_attn_kernel, tq=tq)
    return pl.pallas_call(
        kern,
        grid=(batch, heads, nq),
        in_specs=[
            pl.BlockSpec((tq, dh), lambda b, h, i: (b * nq + i, h)),
            pl.BlockSpec((seq, dh), lambda b, h, i: (b, heads + h)),
            pl.BlockSpec((seq, dh), lambda b, h, i: (b, 2 * heads + h)),
            pl.BlockSpec((None, 1, seq), lambda b, h, i: (b * heads + h, 0, 0)),
        ],
        out_specs=pl.BlockSpec((tq, dh), lambda b, h, i: (b * nq + i, h)),
        out_shape=jax.ShapeDtypeStruct((n, d_mix), BF16),
        scratch_shapes=[pltpu.VMEM((tq, tq), F32), pltpu.VMEM((tq, tq), F32)],
        compiler_params=_params(("arbitrary", "arbitrary", "arbitrary")),
        name="attention",
    )(qkv, qkv, qkv, f_keys)


HALO = 16


def _neg_expm1(x):
    p = 1.0 + x / 10.0
    for k in range(9, 1, -1):
        p = 1.0 + (x / k) * p
    return jnp.where(x > -0.35, -(x * p), 1.0 - jnp.exp(x))


def _even_mix_kernel(z_ref, wpool_ref, pscale_ref, convw_ref, convb_ref, wa_ref, ba_ref,
                     wx_ref, bx_ref, lam_ref, o_ref, pbuf, cbuf, hstate, *, ts, d_pool, d_lru):
    sb = pl.program_id(1)
    group = d_pool // len(POOL_WINDOWS)
    blk = d_lru // LRU_BLOCKS

    @pl.when(sb == 0)
    def _():
        pbuf[0:HALO, :] = jnp.zeros((HALO, d_pool), F32)
        cbuf[0:HALO, :] = jnp.zeros((HALO, d_lru), F32)
        hstate[...] = jnp.zeros_like(hstate)

    up = z_ref[:, 0:d_pool]
    pbuf[HALO:HALO + ts, :] = up
    pos = sb * ts + lax.broadcasted_iota(I32, (ts, group), 0)
    ya = []
    for g, w in enumerate(POOL_WINDOWS):
        lo, hi = g * group, (g + 1) * group
        need = w - 1
        cur = pbuf[HALO - need:HALO + ts, lo:hi]
        span = 1
        while span < w:
            rows = cur.shape[0] - span
            cur = cur[span:span + rows, :] + cur[0:rows, :]
            span *= 2
        cnt = jnp.minimum(pos + 1, w).astype(F32)
        pooled = cur / cnt - up[:, lo:hi]
        y = jnp.dot(pooled.astype(BF16), wpool_ref[g], preferred_element_type=F32)
        ya.append(y * pscale_ref[0:1, lo:hi])
    o_ref[:, 0:d_pool] = jnp.concatenate(ya, axis=-1).astype(o_ref.dtype)
    pbuf[0:HALO, :] = pbuf[ts:ts + HALO, :]

    ul = z_ref[:, d_pool:d_pool + d_lru]
    ug = z_ref[:, d_pool + d_lru:d_pool + 2 * d_lru]
    cbuf[HALO:HALO + ts, :] = ul
    xc = jnp.broadcast_to(convb_ref[0:1, :], (ts, d_lru))
    for k in range(CONV_WIDTH):
        off = HALO - (CONV_WIDTH - 1) + k
        xc = xc + cbuf[off:off + ts, :] * convw_ref[k:k + 1, :]
    cbuf[0:HALO, :] = cbuf[ts:ts + HALO, :]

    xb = xc.astype(BF16)
    ra, ia = [], []
    for hb in range(LRU_BLOCKS):
        xs = xb[:, hb * blk:(hb + 1) * blk]
        ra.append(jnp.dot(xs, wa_ref[hb], preferred_element_type=F32))
        ia.append(jnp.dot(xs, wx_ref[hb], preferred_element_type=F32))
    r_gate = _sigmoid(jnp.concatenate(ra, axis=-1) + ba_ref[0:1, :])
    i_gate = _sigmoid(jnp.concatenate(ia, axis=-1) + bx_ref[0:1, :])
    log_a = LRU_C * r_gate * _log_sigmoid(lam_ref[0:1, :])
    a = jnp.exp(log_a)
    mult = jnp.sqrt(jnp.maximum(_neg_expm1(2.0 * log_a), 0.0))
    b = mult * i_gate * xc

    ridx = lax.broadcasted_iota(I32, (ts, d_lru), 0)
    d = 1
    while d < ts:
        keep = ridx >= d
        a_sh = jnp.where(keep, pltpu.roll(a, d, 0), 1.0)
        b_sh = jnp.where(keep, pltpu.roll(b, d, 0), 0.0)
        b = a * b_sh + b
        a = a * a_sh
        d *= 2
    h = b + a * hstate[0:1, :]
    hstate[...] = jnp.broadcast_to(h[ts - 1:ts, :], hstate.shape)
    o_ref[:, d_pool:d_pool + d_lru] = (h * _gelu(ug)).astype(o_ref.dtype)


def even_mix(z3d, w_pool, pool_scale, conv_w, conv_b, w_a, b_a, w_x, b_x, lam, *, ts=256):
    batch, seq, dz = z3d.shape
    d_pool = pool_scale.shape[-1]
    d_lru = lam.shape[-1]
    ts = min(ts, seq)
    group = d_pool // len(POOL_WINDOWS)
    blk = d_lru // LRU_BLOCKS
    kern = functools.partial(_even_mix_kernel, ts=ts, d_pool=d_pool, d_lru=d_lru)
    const2 = lambda b, s: (0, 0)
    const3 = lambda b, s: (0, 0, 0)
    return pl.pallas_call(
        kern,
        grid=(batch, seq // ts),
        in_specs=[
            pl.BlockSpec((None, ts, dz), lambda b, s: (b, s, 0)),
            _resident((len(POOL_WINDOWS), group, group), const3),
            _resident((1, d_pool), const2),
            _resident((CONV_WIDTH, d_lru), const2),
            _resident((1, d_lru), const2),
            _resident((LRU_BLOCKS, blk, blk), const3),
            _resident((1, d_lru), const2),
            _resident((LRU_BLOCKS, blk, blk), const3),
            _resident((1, d_lru), const2),
            _resident((1, d_lru), const2),
        ],
        out_specs=pl.BlockSpec((None, ts, d_pool + d_lru), lambda b, s: (b, s, 0)),
        out_shape=jax.ShapeDtypeStruct((batch, seq, d_pool + d_lru), BF16),
        scratch_shapes=[
            pltpu.VMEM((HALO + ts, d_pool), F32),
            pltpu.VMEM((HALO + ts, d_lru), F32),
            pltpu.VMEM((8, d_lru), F32),
        ],
        compiler_params=_params(("arbitrary", "arbitrary")),
        name="even_mix",
    )(z3d, w_pool, pool_scale, conv_w, conv_b, w_a, b_a, w_x, b_x, lam)


def _proj_ln_kernel(a_ref, w_ref, x_ref, mod_ref, g_ref, b_ref, o_ref, *, alpha, gate_row):
    y = jnp.dot(a_ref[...], w_ref[...], preferred_element_type=F32)
    gate = mod_ref[gate_row:gate_row + 1, :]
    r = alpha * x_ref[...] + (1.0 + gate) * y
    o_ref[...] = _layer_norm(r, g_ref[...], b_ref[...])


def proj_ln(a2d, w_bf16, x2d, mod_l, ln_g, ln_b, *, seq, alpha, gate_row, tm=512):
    n, d = x2d.shape
    k = a2d.shape[1]
    tm = min(tm, seq)
    blocks_per_batch = seq // tm
    kern = functools.partial(_proj_ln_kernel, alpha=alpha, gate_row=gate_row)
    return pl.pallas_call(
        kern,
        grid=(n // tm,),
        in_specs=[
            pl.BlockSpec((tm, k), lambda i: (i, 0)),
            _resident((k, d), lambda i: (0, 0)),
            pl.BlockSpec((tm, d), lambda i: (i, 0)),
            pl.BlockSpec((None, 6, d), lambda i: (i // blocks_per_batch, 0, 0)),
            _resident((1, d), lambda i: (0, 0)),
            _resident((1, d), lambda i: (0, 0)),
        ],
        out_specs=pl.BlockSpec((tm, d), lambda i: (i, 0)),
        out_shape=jax.ShapeDtypeStruct((n, d), F32),
        compiler_params=_params(("arbitrary",)),
        name="proj_ln",
    )(a2d, w_bf16, x2d, mod_l, ln_g, ln_b)


def _top16_rows(s, kidx, sentinel):
    t = s.shape[1]
    r16 = lax.broadcasted_iota(I32, (PEER_TOPK, t), 0)
    vals = jnp.zeros((PEER_TOPK, t), F32)
    idxs = jnp.zeros((PEER_TOPK, t), I32)
    for r in range(PEER_TOPK):
        m = jnp.max(s, axis=0, keepdims=True)
        am = jnp.min(jnp.where(s == m, kidx, sentinel), axis=0, keepdims=True)
        s = jnp.where(kidx == am, NEG_INF, s)
        vals = jnp.where(r16 == r, m, vals)
        idxs = jnp.where(r16 == r, am, idxs)
    return vals, idxs


def _take_rows16(table, sel):
    out = jnp.zeros(sel.shape, table.dtype)
    for s in range(PEER_TOPK):
        out = jnp.where(sel == s, table[s:s + 1, :], out)
    return out


BUILD_UNROLL = 8


def _peer_route_kernel(q_ref, keys_ref, g_ref,
                       sc_ref, val_ref, idx_ref, e_ref, w_ref, et_ref, wt_ref, *, tr):
    nhp = 2 * PEER_HEADS
    for hp in range(nhp):
        sc_ref[hp] = _dot_nt(keys_ref[hp], q_ref[:, hp * N_KEYS:(hp + 1) * N_KEYS])

    kidx = lax.broadcasted_iota(I32, (N_KEYS, tr), 0)

    def first_topk(hp, carry):
        vals, idxs = _top16_rows(sc_ref[hp], kidx, N_KEYS)
        val_ref[hp] = vals
        idx_ref[hp] = idxs
        return carry

    lax.fori_loop(0, nhp, first_topk, 0)

    i16 = lax.broadcasted_iota(I32, (16, tr), 0)
    i8 = lax.broadcasted_iota(I32, (8, tr), 0)
    flat = jnp.concatenate([
        i16 * 16, i8 * 16 + 1, i8 * 16 + 2, i8 * 16 + 3,
        i16, 16 + i8, 32 + i8,
    ], axis=0)
    valid = jnp.concatenate([
        i16 < 16, i8 < 8, i8 < 5, i8 < 4,
        i16 >= 4, i8 >= 4, i8 == 4,
    ], axis=0)

    def second_topk(hd, carry):
        s1 = val_ref[2 * hd]
        s2 = val_ref[2 * hd + 1]
        cand = jnp.concatenate([
            s1 + s2[0:1, :], s1[0:8, :] + s2[1:2, :], s1[0:8, :] + s2[2:3, :], s1[0:8, :] + s2[3:4, :],
            s1[0:1, :] + s2, s1[1:2, :] + s2[0:8, :], s1[2:3, :] + s2[0:8, :],
        ], axis=0)
        cand = jnp.where(valid, cand, NEG_INF)
        top, fsel = _top16_rows(cand, flat, 256)
        e = jnp.exp(top - top[0:1, :])
        gate = e / jnp.sum(e, axis=0, keepdims=True)
        a = _take_rows16(idx_ref[2 * hd], fsel >> 4)
        b = _take_rows16(idx_ref[2 * hd + 1], fsel & 15)
        row0 = pl.multiple_of(hd * PEER_TOPK, PEER_TOPK)
        e_ref[pl.ds(row0, PEER_TOPK), :] = a * N_KEYS + b
        w_ref[pl.ds(row0, PEER_TOPK), :] = gate
        return carry

    lax.fori_loop(0, PEER_HEADS, second_topk, 0)

    n_act = PEER_HEADS * PEER_TOPK
    for c in range(tr // n_act):
        et_ref[c * n_act:(c + 1) * n_act, :] = e_ref[:, c * n_act:(c + 1) * n_act].T
        wt_ref[c * n_act:(c + 1) * n_act, :] = w_ref[:, c * n_act:(c + 1) * n_act].T

    iota_rows = lax.broadcasted_iota(I32, (N_KEYS, n_act), 0)

    def build(step, carry):
        base = pl.multiple_of(step * BUILD_UNROLL, BUILD_UNROLL)
        erows = et_ref[pl.ds(base, BUILD_UNROLL), :]
        wrows = wt_ref[pl.ds(base, BUILD_UNROLL), :]
        for u in range(BUILD_UNROLL):
            erow = erows[u:u + 1, :]
            wrow = wrows[u:u + 1, :]
            pt = jnp.where(iota_rows == (erow >> 7), wrow, 0.0).astype(BF16)
            qt = jnp.where(iota_rows == (erow & (N_KEYS - 1)), 1.0, 0.0).astype(BF16)
            g_ref[base + u] = _dot_nt(pt, qt)
        return carry

    lax.fori_loop(0, tr // BUILD_UNROLL, build, 0)


def peer_route(q2d, keys_bf16, *, seq, tr=256):
    n, d = q2d.shape
    tr = min(tr, seq)
    nhp = 2 * PEER_HEADS
    n_act = PEER_HEADS * PEER_TOPK
    kern = functools.partial(_peer_route_kernel, tr=tr)
    return pl.pallas_call(
        kern,
        grid=(n // tr,),
        in_specs=[
            pl.BlockSpec((tr, d), lambda i: (i, 0)),
            _resident(keys_bf16.shape, lambda i: (0, 0, 0)),
        ],
        out_specs=pl.BlockSpec((tr, N_KEYS, N_KEYS), lambda i: (i, 0, 0)),
        out_shape=jax.ShapeDtypeStruct((n, N_KEYS, N_KEYS), F32),
        scratch_shapes=[
            pltpu.VMEM((nhp, N_KEYS, tr), F32),
            pltpu.VMEM((nhp, PEER_TOPK, tr), F32),
            pltpu.VMEM((nhp, PEER_TOPK, tr), I32),
            pltpu.VMEM((n_act, tr), I32),
            pltpu.VMEM((n_act, tr), F32),
            pltpu.VMEM((tr, n_act), I32),
            pltpu.VMEM((tr, n_act), F32),
        ],
        compiler_params=_params(("arbitrary",)),
        name="peer_route",
    )(q2d, keys_bf16)


def _peer_dense_kernel(x_ref, g_ref, u_ref, v_ref, mod_ref, lng_ref, lnb_ref, o_ref,
                       h_ref, act_ref, acc_ref, *, alpha):
    j = pl.program_id(1)
    tm, rows, _ = g_ref.shape

    @pl.when(j == 0)
    def _():
        acc_ref[...] = jnp.zeros_like(acc_ref)
        sh = mod_ref[3:4, :]
        sc = mod_ref[4:5, :]
        h_ref[...] = (x_ref[...] * (1.0 + sc) + sh).astype(BF16)

    z = _dot_nt(h_ref[...], u_ref[...])
    for r in range(rows):
        cols = slice(r * N_KEYS, (r + 1) * N_KEYS)
        act_ref[:, cols] = (_gelu(z[:, cols]) * g_ref[:, r, :]).astype(BF16)
    acc_ref[...] += jnp.dot(act_ref[...], v_ref[...], preferred_element_type=F32)

    @pl.when(j == pl.num_programs(1) - 1)
    def _():
        gate = mod_ref[5:6, :]
        r = alpha * x_ref[...] + (1.0 + gate) * acc_ref[...]
        o_ref[...] = _layer_norm(r, lng_ref[...], lnb_ref[...])


def peer_dense(x2d, g3d, u_bf16, v_bf16, mod_l, ln_g, ln_b, *, seq, alpha, tm=512, te=1024):
    n, d = x2d.shape
    n_exp = u_bf16.shape[0]
    tm = min(tm, seq)
    blocks_per_batch = seq // tm
    kern = functools.partial(_peer_dense_kernel, alpha=alpha)
    return pl.pallas_call(
        kern,
        grid=(n // tm, n_exp // te),
        in_specs=[
            pl.BlockSpec((tm, d), lambda i, j: (i, 0)),
            pl.BlockSpec((tm, te // N_KEYS, N_KEYS), lambda i, j: (i, j, 0)),
            pl.BlockSpec((te, d), lambda i, j: (j, 0)),
            pl.BlockSpec((te, d), lambda i, j: (j, 0)),
            pl.BlockSpec((None, 6, d), lambda i, j: (i // blocks_per_batch, 0, 0)),
            _resident((1, d), lambda i, j: (0, 0)),
            _resident((1, d), lambda i, j: (0, 0)),
        ],
        out_specs=pl.BlockSpec((tm, d), lambda i, j: (i, 0)),
        out_shape=jax.ShapeDtypeStruct((n, d), F32),
        scratch_shapes=[
            pltpu.VMEM((tm, d), BF16),
            pltpu.VMEM((tm, te), BF16),
            pltpu.VMEM((tm, d), F32),
        ],
        compiler_params=_params(("arbitrary", "arbitrary")),
        name="peer_dense",
    )(x2d, g3d, u_bf16, v_bf16, mod_l, ln_g, ln_b)


def kernel(x, c, ada_w, ada_b, ln_g, ln_b, peer_wq, peer_keys, peer_u, peer_v, ev_w_in, ev_w_pool,
           ev_pool_scale, ev_conv_w, ev_conv_b, ev_w_a, ev_b_a, ev_w_x, ev_b_x, ev_lam, ev_w_out,
           od_w_in, od_b_f, od_w_out):
    batch, seq, d = x.shape
    depth = ada_w.shape[0]
    n = batch * seq
    alpha = (2.0 * depth) ** 0.25
    d_mix = od_w_out.shape[1]

    c_pad = jnp.zeros((8, d), F32).at[:batch].set(c)
    mod = ada_mod(c_pad, ada_w, ada_b)[:, :batch].reshape(depth, batch, 6, d)

    xf = x.reshape(n, d)
    for l in range(depth):
        mod_l = mod[l]
        g0 = ln_g[l, 0].reshape(1, d)
        b0 = ln_b[l, 0].reshape(1, d)
        g1 = ln_g[l, 1].reshape(1, d)
        b1 = ln_b[l, 1].reshape(1, d)
        if l % 2 == 0:
            e = l // 2
            z = mod_matmul(xf, mod_l, ev_w_in[e].astype(BF16), seq=seq, shift_row=0, scale_row=1,
                           out_dtype=F32)
            mixed = even_mix(
                z.reshape(batch, seq, -1), ev_w_pool[e].astype(BF16), ev_pool_scale[e].reshape(1, -1),
                ev_conv_w[e], ev_conv_b[e].reshape(1, -1), ev_w_a[e].astype(BF16),
                ev_b_a[e].reshape(1, -1), ev_w_x[e].astype(BF16), ev_b_x[e].reshape(1, -1),
                ev_lam[e].reshape(1, -1))
            xf = proj_ln(mixed.reshape(n, -1), ev_w_out[e].astype(BF16), xf, mod_l, g0, b0,
                         seq=seq, alpha=alpha, gate_row=2)
        else:
            o = l // 2
            w_in = od_w_in[o]
            w_f = jnp.zeros((d, 128), F32).at[:, :N_HEADS_ATTN].set(w_in[:, 3 * d_mix:])
            wf_hi = w_f.astype(BF16)
            wf_lo = (w_f - wf_hi.astype(F32)).astype(BF16)
            b_f = jnp.zeros((1, 128), F32).at[0, :N_HEADS_ATTN].set(od_b_f[o])
            qkv, f_cum = odd_in(xf, mod_l, w_in[:, :3 * d_mix].astype(BF16), wf_hi, wf_lo, b_f, seq=seq)
            f_keys = f_cum[:, :N_HEADS_ATTN].reshape(batch, seq, N_HEADS_ATTN)
            f_keys = jnp.transpose(f_keys, (0, 2, 1)).reshape(batch * N_HEADS_ATTN, 1, seq)
            attn = attention(qkv, f_keys, batch=batch, seq=seq)
            xf = proj_ln(attn, od_w_out[o].astype(BF16), xf, mod_l, g0, b0,
                         seq=seq, alpha=alpha, gate_row=2)

        keys = peer_keys[l].reshape(2 * PEER_HEADS, N_KEYS, -1).astype(BF16)
        q = mod_matmul(xf, mod_l, peer_wq[l].astype(BF16), seq=seq, shift_row=3, scale_row=4,
                       out_dtype=BF16)
        gmat = peer_route(q, keys, seq=seq)
        xf = peer_dense(xf, gmat, peer_u[l].astype(BF16), peer_v[l].astype(BF16), mod_l, g1, b1,
                        seq=seq, alpha=alpha)
    return xf.reshape(batch, seq, d)
```

```python
import functools
import math

import jax
import jax.numpy as jnp
from jax import lax
from jax.experimental import pallas as pl
from jax.experimental.pallas import tpu as pltpu

F32 = jnp.float32
BF16 = jnp.bfloat16
I32 = jnp.int32

LN_EPS = 1e-5
POOL_WINDOWS = (2, 4, 8, 16)
CONV_WIDTH = 4
LRU_BLOCKS = 8
LRU_C = 8.0
N_HEADS_ATTN = 16
PEER_HEADS = 8
PEER_TOPK = 16
N_KEYS = 128

V7X_VMEM_BYTES = 64 * 1024 * 1024
VMEM_LIMIT = 56 * 1024 * 1024
NEG_INF = float("-inf")
LOG2E = math.log2(math.e)


def _params(sem):
    return pltpu.CompilerParams(dimension_semantics=sem, vmem_limit_bytes=VMEM_LIMIT)


def _resident(shape, index_map):
    return pl.BlockSpec(shape, index_map, pipeline_mode=pl.Buffered(1))


def _gelu(x):
    c = math.sqrt(2.0 / math.pi)
    return 0.5 * x * (1.0 + jnp.tanh(c * (x + 0.044715 * (x * x * x))))


def _log_sigmoid(x):
    return jnp.minimum(x, 0.0) - jnp.log1p(jnp.exp(-jnp.abs(x)))


def _sigmoid(x):
    return 1.0 / (1.0 + jnp.exp(-x))


def _layer_norm(r, g, b):
    mu = jnp.mean(r, axis=-1, keepdims=True)
    d = r - mu
    var = jnp.mean(d * d, axis=-1, keepdims=True)
    return d * lax.rsqrt(var + LN_EPS) * g + b


def _dot_nt(a, b):
    return lax.dot_general(a, b, (((1,), (1,)), ((), ())), preferred_element_type=F32)


def _ada_kernel(c_ref, w_ref, b_ref, o_ref):
    c = c_ref[...]
    ca = c * _sigmoid(c)
    o_ref[...] = jnp.dot(ca, w_ref[...], preferred_element_type=F32) + b_ref[...]


def ada_mod(c_pad, ada_w, ada_b):
    depth, d, n6 = ada_w.shape
    rows = c_pad.shape[0]
    tn = 1024
    return pl.pallas_call(
        _ada_kernel,
        grid=(depth, n6 // tn),
        in_specs=[
            pl.BlockSpec((rows, d), lambda l, j: (0, 0)),
            pl.BlockSpec((None, d, tn), lambda l, j: (l, 0, j)),
            pl.BlockSpec((None, 1, tn), lambda l, j: (l, 0, j)),
        ],
        out_specs=pl.BlockSpec((None, rows, tn), lambda l, j: (l, 0, j)),
        out_shape=jax.ShapeDtypeStruct((depth, rows, n6), F32),
        compiler_params=_params(("arbitrary", "arbitrary")),
        name="ada_mod",
    )(c_pad, ada_w, ada_b.reshape(depth, 1, n6))


def _mod_matmul_kernel(x_ref, mod_ref, w_ref, o_ref, h_ref, *, shift_row, scale_row):
    @pl.when(pl.program_id(1) == 0)
    def _():
        sh = mod_ref[shift_row:shift_row + 1, :]
        sc = mod_ref[scale_row:scale_row + 1, :]
        h_ref[...] = (x_ref[...] * (1.0 + sc) + sh).astype(BF16)

    o_ref[...] = jnp.dot(h_ref[...], w_ref[...], preferred_element_type=F32).astype(o_ref.dtype)


def mod_matmul(x2d, mod_l, w_bf16, *, seq, shift_row, scale_row, out_dtype, emit_h=False,
               tm=1024, tn=1024):
    n, d = x2d.shape
    nout = w_bf16.shape[1]
    tm = min(tm, seq)
    blocks_per_batch = seq // tm
    kern = functools.partial(_mod_matmul_kernel, shift_row=shift_row, scale_row=scale_row)
    out_specs = [pl.BlockSpec((tm, tn), lambda i, j: (i, j))]
    out_shape = [jax.ShapeDtypeStruct((n, nout), out_dtype)]
    scratch = []
    if emit_h:
        out_specs.append(pl.BlockSpec((tm, d), lambda i, j: (i, 0)))
        out_shape.append(jax.ShapeDtypeStruct((n, d), BF16))
    else:
        scratch.append(pltpu.VMEM((tm, d), BF16))
    res = pl.pallas_call(
        kern,
        grid=(n // tm, nout // tn),
        in_specs=[
            pl.BlockSpec((tm, d), lambda i, j: (i, 0)),
            pl.BlockSpec((None, 6, d), lambda i, j: (i // blocks_per_batch, 0, 0)),
            pl.BlockSpec((d, tn), lambda i, j: (0, j)),
        ],
        out_specs=out_specs,
        out_shape=out_shape,
        scratch_shapes=scratch,
        compiler_params=_params(("arbitrary", "arbitrary")),
        name="mod_matmul",
    )(x2d, mod_l, w_bf16)
    return res if emit_h else res[0]


def _cumsum_rows(x):
    rows = x.shape[0]
    ridx = lax.broadcasted_iota(I32, x.shape, 0)
    d = 1
    while d < rows:
        x = x + jnp.where(ridx >= d, pltpu.roll(x, d, 0), 0.0)
        d *= 2
    return x


def _odd_in_kernel(x_ref, mod_ref, w_ref, wfh_ref, wfl_ref, bf_ref, o_ref, f_ref,
                   h_ref, carry_ref, *, blocks_per_batch, d_mix, q_scale, tn):
    i = pl.program_id(0)
    j = pl.program_id(1)

    @pl.when(j == 0)
    def _():
        sh = mod_ref[0:1, :]
        sc = mod_ref[1:2, :]
        h = x_ref[...] * (1.0 + sc) + sh
        h_hi = h.astype(BF16)
        h_ref[...] = h_hi
        h_lo = (h - h_hi.astype(F32)).astype(BF16)
        zf = (jnp.dot(h_hi, wfh_ref[...], preferred_element_type=F32)
              + jnp.dot(h_lo, wfh_ref[...], preferred_element_type=F32)
              + jnp.dot(h_hi, wfl_ref[...], preferred_element_type=F32))
        logf = _log_sigmoid(zf + bf_ref[...])

        @pl.when(i % blocks_per_batch == 0)
        def _():
            carry_ref[...] = jnp.zeros_like(carry_ref)

        cs = _cumsum_rows(logf) + carry_ref[0:1, :]
        f_ref[...] = cs
        carry_ref[...] = jnp.broadcast_to(cs[cs.shape[0] - 1:, :], carry_ref.shape)

    z = jnp.dot(h_ref[...], w_ref[...], preferred_element_type=F32)
    scale = jnp.where(j * tn < d_mix, q_scale, 1.0)
    o_ref[...] = (z * scale).astype(o_ref.dtype)


def odd_in(x2d, mod_l, w_qkv, wf_hi, wf_lo, b_f_pad, *, seq, tm=1024, tn=1024):
    n, d = x2d.shape
    nout = w_qkv.shape[1]
    d_mix = nout // 3
    tm = min(tm, seq)
    blocks_per_batch = seq // tm
    head_dim = d_mix // N_HEADS_ATTN
    kern = functools.partial(_odd_in_kernel, blocks_per_batch=blocks_per_batch, d_mix=d_mix,
                             q_scale=head_dim ** -0.5 * LOG2E, tn=tn)
    return pl.pallas_call(
        kern,
        grid=(n // tm, nout // tn),
        in_specs=[
            pl.BlockSpec((tm, d), lambda i, j: (i, 0)),
            pl.BlockSpec((None, 6, d), lambda i, j: (i // blocks_per_batch, 0, 0)),
            pl.BlockSpec((d, tn), lambda i, j: (0, j)),
            _resident((d, 128), lambda i, j: (0, 0)),
            _resident((d, 128), lambda i, j: (0, 0)),
            _resident((1, 128), lambda i, j: (0, 0)),
        ],
        out_specs=[
            pl.BlockSpec((tm, tn), lambda i, j: (i, j)),
            pl.BlockSpec((tm, 128), lambda i, j: (i, 0)),
        ],
        out_shape=[
            jax.ShapeDtypeStruct((n, nout), BF16),
            jax.ShapeDtypeStruct((n, 128), F32),
        ],
        scratch_shapes=[pltpu.VMEM((tm, d), BF16), pltpu.VMEM((8, 128), F32)],
        compiler_params=_params(("arbitrary", "arbitrary")),
        name="odd_in",
    )(x2d, mod_l, w_qkv, wf_hi, wf_lo, b_f_pad)


def _attn_kernel(q_ref, kt_ref, v_ref, fk_ref, o_ref, sa_ref, sb_ref, *, tq):
    qi = pl.program_id(2)
    f_ref0 = fk_ref[0:1, pl.ds(pl.multiple_of(qi * tq, tq), tq)][:, 0:1]
    dh = q_ref.shape[1]

    def scores(c, s_ref):
        start = pl.multiple_of(c * tq, tq)
        s = jnp.dot(q_ref[...], kt_ref[:, pl.ds(start, tq)], preferred_element_type=F32)
        s_ref[...] = s + (f_ref0 - fk_ref[0:1, pl.ds(start, tq)]) * LOG2E

    def update(c, s_ref, carry, diagonal):
        m, l, acc = carry
        start = pl.multiple_of(c * tq, tq)
        if diagonal:
            row = lax.broadcasted_iota(I32, (tq, tq), 0)
            col = lax.broadcasted_iota(I32, (tq, tq), 1)
            s_ref[...] = jnp.where(col <= row, s_ref[...], NEG_INF)
        m_new = jnp.maximum(m, jnp.max(s_ref[...], axis=-1, keepdims=True))
        alpha = jnp.exp2(m - m_new)
        p = jnp.exp2(s_ref[...] - m_new)
        l = alpha * l + jnp.sum(p, axis=-1, keepdims=True)
        v = v_ref[pl.ds(start, tq), :]
        acc = alpha * acc + jnp.dot(p.astype(BF16), v, preferred_element_type=F32)
        return m_new, l, acc

    scores(0, sa_ref)

    def pair(p, carry):
        scores(2 * p + 1, sb_ref)
        carry = update(2 * p, sa_ref, carry, False)
        scores(2 * p + 2, sa_ref)
        return update(2 * p + 1, sb_ref, carry, False)

    def odd_tail(carry):
        scores(qi, sb_ref)
        carry = update(qi - 1, sa_ref, carry, False)
        return update(qi, sb_ref, carry, True)

    def even_tail(carry):
        return update(qi, sa_ref, carry, True)

    init = (jnp.full((tq, 1), NEG_INF, F32), jnp.zeros((tq, 1), F32), jnp.zeros((tq, dh), F32))
    carry = lax.fori_loop(0, qi // 2, pair, init)
    m, l, acc = lax.cond(qi % 2 == 1, odd_tail, even_tail, carry)
    o_ref[...] = (acc / l).astype(o_ref.dtype)


def attention(qkv, k_t, f_keys, *, batch, seq, tq=512):
    n = qkv.shape[0]
    d_mix = qkv.shape[1] // 3
    heads = N_HEADS_ATTN
    dh = d_mix // heads
    tq = min(tq, seq)
    nq = seq // tq
    kern = functools.partial(_attn_kernel, tq=tq)
    return pl.pallas_call(
        kern,
        grid=(batch, heads, nq),
        in_specs=[
            pl.BlockSpec((tq, dh), lambda b, h, i: (b * nq + i, h)),
            pl.BlockSpec((None, None, dh, seq), lambda b, h, i: (b, h, 0, 0)),
            pl.BlockSpec((seq, dh), lambda b, h, i: (b, 2 * heads + h)),
            pl.BlockSpec((None, 1, seq), lambda b, h, i: (b * heads + h, 0, 0)),
        ],
        out_specs=pl.BlockSpec((tq, dh), lambda b, h, i: (b * nq + i, h)),
        out_shape=jax.ShapeDtypeStruct((n, d_mix), BF16),
        scratch_shapes=[pltpu.VMEM((tq, tq), F32), pltpu.VMEM((tq, tq), F32)],
        compiler_params=_params(("arbitrary", "arbitrary", "arbitrary")),
        name="attention",
    )(qkv, k_t, qkv, f_keys)


HALO = 16


def _neg_expm1(x):
    p = 1.0 + x / 10.0
    for k in range(9, 1, -1):
        p = 1.0 + (x / k) * p
    return jnp.where(x > -0.35, -(x * p), 1.0 - jnp.exp(x))


def _even_mix_kernel(z_ref, wpool_ref, pscale_ref, convw_ref, convb_ref, wa_ref, ba_ref,
                     wx_ref, bx_ref, lam_ref, o_ref, pbuf, cbuf, hstate, *, ts, d_pool, d_lru):
    sb = pl.program_id(1)
    group = d_pool // len(POOL_WINDOWS)
    blk = d_lru // LRU_BLOCKS

    @pl.when(sb == 0)
    def _():
        pbuf[0:HALO, :] = jnp.zeros((HALO, d_pool), F32)
        cbuf[0:HALO, :] = jnp.zeros((HALO, d_lru), F32)
        hstate[...] = jnp.zeros_like(hstate)

    up = z_ref[:, 0:d_pool]
    pbuf[HALO:HALO + ts, :] = up
    pos = sb * ts + lax.broadcasted_iota(I32, (ts, group), 0)
    ya = []
    for g, w in enumerate(POOL_WINDOWS):
        lo, hi = g * group, (g + 1) * group
        need = w - 1
        cur = pbuf[HALO - need:HALO + ts, lo:hi]
        span = 1
        while span < w:
            rows = cur.shape[0] - span
            cur = cur[span:span + rows, :] + cur[0:rows, :]
            span *= 2
        cnt = jnp.minimum(pos + 1, w).astype(F32)
        pooled = cur / cnt - up[:, lo:hi]
        y = jnp.dot(pooled.astype(BF16), wpool_ref[g], preferred_element_type=F32)
        ya.append(y * pscale_ref[0:1, lo:hi])
    o_ref[:, 0:d_pool] = jnp.concatenate(ya, axis=-1).astype(o_ref.dtype)
    pbuf[0:HALO, :] = pbuf[ts:ts + HALO, :]

    ul = z_ref[:, d_pool:d_pool + d_lru]
    ug = z_ref[:, d_pool + d_lru:d_pool + 2 * d_lru]
    cbuf[HALO:HALO + ts, :] = ul
    xc = jnp.broadcast_to(convb_ref[0:1, :], (ts, d_lru))
    for k in range(CONV_WIDTH):
        off = HALO - (CONV_WIDTH - 1) + k
        xc = xc + cbuf[off:off + ts, :] * convw_ref[k:k + 1, :]
    cbuf[0:HALO, :] = cbuf[ts:ts + HALO, :]

    xb = xc.astype(BF16)
    ra, ia = [], []
    for hb in range(LRU_BLOCKS):
        xs = xb[:, hb * blk:(hb + 1) * blk]
        ra.append(jnp.dot(xs, wa_ref[hb], preferred_element_type=F32))
        ia.append(jnp.dot(xs, wx_ref[hb], preferred_element_type=F32))
    r_gate = _sigmoid(jnp.concatenate(ra, axis=-1) + ba_ref[0:1, :])
    i_gate = _sigmoid(jnp.concatenate(ia, axis=-1) + bx_ref[0:1, :])
    log_a = LRU_C * r_gate * _log_sigmoid(lam_ref[0:1, :])
    a = jnp.exp(log_a)
    mult = jnp.sqrt(jnp.maximum(_neg_expm1(2.0 * log_a), 0.0))
    b = mult * i_gate * xc

    ridx = lax.broadcasted_iota(I32, (ts, d_lru), 0)
    d = 1
    while d < ts:
        keep = ridx >= d
        a_sh = jnp.where(keep, pltpu.roll(a, d, 0), 1.0)
        b_sh = jnp.where(keep, pltpu.roll(b, d, 0), 0.0)
        b = a * b_sh + b
        a = a * a_sh
        d *= 2
    h = b + a * hstate[0:1, :]
    hstate[...] = jnp.broadcast_to(h[ts - 1:ts, :], hstate.shape)
    o_ref[:, d_pool:d_pool + d_lru] = (h * _gelu(ug)).astype(o_ref.dtype)


def even_mix(z3d, w_pool, pool_scale, conv_w, conv_b, w_a, b_a, w_x, b_x, lam, *, ts=256):
    batch, seq, dz = z3d.shape
    d_pool = pool_scale.shape[-1]
    d_lru = lam.shape[-1]
    ts = min(ts, seq)
    group = d_pool // len(POOL_WINDOWS)
    blk = d_lru // LRU_BLOCKS
    kern = functools.partial(_even_mix_kernel, ts=ts, d_pool=d_pool, d_lru=d_lru)
    const2 = lambda b, s: (0, 0)
    const3 = lambda b, s: (0, 0, 0)
    return pl.pallas_call(
        kern,
        grid=(batch, seq // ts),
        in_specs=[
            pl.BlockSpec((None, ts, dz), lambda b, s: (b, s, 0)),
            _resident((len(POOL_WINDOWS), group, group), const3),
            _resident((1, d_pool), const2),
            _resident((CONV_WIDTH, d_lru), const2),
            _resident((1, d_lru), const2),
            _resident((LRU_BLOCKS, blk, blk), const3),
            _resident((1, d_lru), const2),
            _resident((LRU_BLOCKS, blk, blk), const3),
            _resident((1, d_lru), const2),
            _resident((1, d_lru), const2),
        ],
        out_specs=pl.BlockSpec((None, ts, d_pool + d_lru), lambda b, s: (b, s, 0)),
        out_shape=jax.ShapeDtypeStruct((batch, seq, d_pool + d_lru), BF16),
        scratch_shapes=[
            pltpu.VMEM((HALO + ts, d_pool), F32),
            pltpu.VMEM((HALO + ts, d_lru), F32),
            pltpu.VMEM((8, d_lru), F32),
        ],
        compiler_params=_params(("arbitrary", "arbitrary")),
        name="even_mix",
    )(z3d, w_pool, pool_scale, conv_w, conv_b, w_a, b_a, w_x, b_x, lam)


def _proj_ln_kernel(a_ref, w_ref, x_ref, mod_ref, g_ref, b_ref, o_ref, *, alpha, gate_row):
    y = jnp.dot(a_ref[...], w_ref[...], preferred_element_type=F32)
    gate = mod_ref[gate_row:gate_row + 1, :]
    r = alpha * x_ref[...] + (1.0 + gate) * y
    o_ref[...] = _layer_norm(r, g_ref[...], b_ref[...])


def proj_ln(a2d, w_bf16, x2d, mod_l, ln_g, ln_b, *, seq, alpha, gate_row, tm=512):
    n, d = x2d.shape
    k = a2d.shape[1]
    tm = min(tm, seq)
    blocks_per_batch = seq // tm
    kern = functools.partial(_proj_ln_kernel, alpha=alpha, gate_row=gate_row)
    return pl.pallas_call(
        kern,
        grid=(n // tm,),
        in_specs=[
            pl.BlockSpec((tm, k), lambda i: (i, 0)),
            _resident((k, d), lambda i: (0, 0)),
            pl.BlockSpec((tm, d), lambda i: (i, 0)),
            pl.BlockSpec((None, 6, d), lambda i: (i // blocks_per_batch, 0, 0)),
            _resident((1, d), lambda i: (0, 0)),
            _resident((1, d), lambda i: (0, 0)),
        ],
        out_specs=pl.BlockSpec((tm, d), lambda i: (i, 0)),
        out_shape=jax.ShapeDtypeStruct((n, d), F32),
        compiler_params=_params(("arbitrary",)),
        name="proj_ln",
    )(a2d, w_bf16, x2d, mod_l, ln_g, ln_b)


def _top16_rows(s, kidx, sentinel):
    t = s.shape[1]
    r16 = lax.broadcasted_iota(I32, (PEER_TOPK, t), 0)
    vals = jnp.zeros((PEER_TOPK, t), F32)
    idxs = jnp.zeros((PEER_TOPK, t), I32)
    for r in range(PEER_TOPK):
        m = jnp.max(s, axis=0, keepdims=True)
        am = jnp.min(jnp.where(s == m, kidx, sentinel), axis=0, keepdims=True)
        s = jnp.where(kidx == am, NEG_INF, s)
        vals = jnp.where(r16 == r, m, vals)
        idxs = jnp.where(r16 == r, am, idxs)
    return vals, idxs


def _take_rows16(table, sel):
    out = jnp.zeros(sel.shape, table.dtype)
    for s in range(PEER_TOPK):
        out = jnp.where(sel == s, table[s:s + 1, :], out)
    return out


BUILD_UNROLL = 16


def _peer_route_kernel(q_ref, keys_ref, g_ref,
                       sc_ref, val_ref, idx_ref, e_ref, w_ref, et_ref, wt_ref, *, tr):
    nhp = 2 * PEER_HEADS
    for hp in range(nhp):
        sc_ref[hp] = _dot_nt(keys_ref[hp], q_ref[:, hp * N_KEYS:(hp + 1) * N_KEYS])

    kidx = lax.broadcasted_iota(I32, (N_KEYS, tr), 0)

    def first_topk(hp, carry):
        vals, idxs = _top16_rows(sc_ref[hp], kidx, N_KEYS)
        val_ref[hp] = vals
        idx_ref[hp] = idxs
        return carry

    lax.fori_loop(0, nhp, first_topk, 0)

    i16 = lax.broadcasted_iota(I32, (16, tr), 0)
    i8 = lax.broadcasted_iota(I32, (8, tr), 0)
    flat = jnp.concatenate([
        i16 * 16, i8 * 16 + 1, i8 * 16 + 2, i8 * 16 + 3,
        i16, 16 + i8, 32 + i8,
    ], axis=0)
    valid = jnp.concatenate([
        i16 < 16, i8 < 8, i8 < 5, i8 < 4,
        i16 >= 4, i8 >= 4, i8 == 4,
    ], axis=0)

    def second_topk(hd, carry):
        s1 = val_ref[2 * hd]
        s2 = val_ref[2 * hd + 1]
        cand = jnp.concatenate([
            s1 + s2[0:1, :], s1[0:8, :] + s2[1:2, :], s1[0:8, :] + s2[2:3, :], s1[0:8, :] + s2[3:4, :],
            s1[0:1, :] + s2, s1[1:2, :] + s2[0:8, :], s1[2:3, :] + s2[0:8, :],
        ], axis=0)
        cand = jnp.where(valid, cand, NEG_INF)
        top, fsel = _top16_rows(cand, flat, 256)
        e = jnp.exp(top - top[0:1, :])
        gate = e / jnp.sum(e, axis=0, keepdims=True)
        a = _take_rows16(idx_ref[2 * hd], fsel >> 4)
        b = _take_rows16(idx_ref[2 * hd + 1], fsel & 15)
        row0 = pl.multiple_of(hd * PEER_TOPK, PEER_TOPK)
        e_ref[pl.ds(row0, PEER_TOPK), :] = a * N_KEYS + b
        w_ref[pl.ds(row0, PEER_TOPK), :] = gate
        return carry

    lax.fori_loop(0, PEER_HEADS, second_topk, 0)

    n_act = PEER_HEADS * PEER_TOPK
    for c in range(tr // n_act):
        et_ref[c * n_act:(c + 1) * n_act, :] = e_ref[:, c * n_act:(c + 1) * n_act].T
        wt_ref[c * n_act:(c + 1) * n_act, :] = w_ref[:, c * n_act:(c + 1) * n_act].T

    iota_rows = lax.broadcasted_iota(I32, (N_KEYS, n_act), 0)

    def build(step, carry):
        base = pl.multiple_of(step * BUILD_UNROLL, BUILD_UNROLL)
        erows = et_ref[pl.ds(base, BUILD_UNROLL), :]
        wrows = wt_ref[pl.ds(base, BUILD_UNROLL), :]
        for u in range(BUILD_UNROLL):
            erow = erows[u:u + 1, :]
            wrow = wrows[u:u + 1, :]
            pt = jnp.where(iota_rows == (erow >> 7), wrow, 0.0).astype(BF16)
            qt = jnp.where(iota_rows == (erow & (N_KEYS - 1)), 1.0, 0.0).astype(BF16)
            g_ref[base + u] = _dot_nt(pt, qt)
        return carry

    lax.fori_loop(0, tr // BUILD_UNROLL, build, 0)


def peer_route(q2d, keys_bf16, *, seq, tr=256):
    n, d = q2d.shape
    tr = min(tr, seq)
    nhp = 2 * PEER_HEADS
    n_act = PEER_HEADS * PEER_TOPK
    kern = functools.partial(_peer_route_kernel, tr=tr)
    return pl.pallas_call(
        kern,
        grid=(n // tr,),
        in_specs=[
            pl.BlockSpec((tr, d), lambda i: (i, 0)),
            _resident(keys_bf16.shape, lambda i: (0, 0, 0)),
        ],
        out_specs=pl.BlockSpec((tr, N_KEYS, N_KEYS), lambda i: (i, 0, 0)),
        out_shape=jax.ShapeDtypeStruct((n, N_KEYS, N_KEYS), F32),
        scratch_shapes=[
            pltpu.VMEM((nhp, N_KEYS, tr), F32),
            pltpu.VMEM((nhp, PEER_TOPK, tr), F32),
            pltpu.VMEM((nhp, PEER_TOPK, tr), I32),
            pltpu.VMEM((n_act, tr), I32),
            pltpu.VMEM((n_act, tr), F32),
            pltpu.VMEM((tr, n_act), I32),
            pltpu.VMEM((tr, n_act), F32),
        ],
        compiler_params=_params(("arbitrary",)),
        name="peer_route",
    )(q2d, keys_bf16)


def _peer_dense_kernel(h_ref, g_ref, ut_ref, v_ref, o_ref, act_ref):
    j = pl.program_id(1)
    rows = g_ref.shape[1]

    @pl.when(j == 0)
    def _():
        o_ref[...] = jnp.zeros_like(o_ref)

    z = jnp.dot(h_ref[...], ut_ref[...], preferred_element_type=F32)
    for r in range(rows):
        cols = slice(r * N_KEYS, (r + 1) * N_KEYS)
        act_ref[:, cols] = (_gelu(z[:, cols]) * g_ref[:, r, :]).astype(BF16)
    o_ref[...] += jnp.dot(act_ref[...], v_ref[...], preferred_element_type=F32)


def peer_dense(h2d, g3d, ut_bf16, v_bf16, *, seq, tm=1024, te=1024):
    n, d = h2d.shape
    n_exp = v_bf16.shape[0]
    tm = min(tm, seq)
    return pl.pallas_call(
        _peer_dense_kernel,
        grid=(n // tm, n_exp // te),
        in_specs=[
            pl.BlockSpec((tm, d), lambda i, j: (i, 0), pipeline_mode=pl.Buffered(1)),
            pl.BlockSpec((tm, te // N_KEYS, N_KEYS), lambda i, j: (i, j, 0)),
            pl.BlockSpec((d, te), lambda i, j: (0, j)),
            pl.BlockSpec((te, d), lambda i, j: (j, 0)),
        ],
        out_specs=pl.BlockSpec((tm, d), lambda i, j: (i, 0)),
        out_shape=jax.ShapeDtypeStruct((n, d), F32),
        scratch_shapes=[pltpu.VMEM((tm, te), BF16)],
        compiler_params=_params(("arbitrary", "arbitrary")),
        name="peer_dense",
    )(h2d, g3d, ut_bf16, v_bf16)


def _residual_ln_kernel(x_ref, y_ref, mod_ref, g_ref, b_ref, o_ref, *, alpha, gate_row):
    gate = mod_ref[gate_row:gate_row + 1, :]
    r = alpha * x_ref[...] + (1.0 + gate) * y_ref[...]
    o_ref[...] = _layer_norm(r, g_ref[...], b_ref[...])


def residual_ln(x2d, y2d, mod_l, ln_g, ln_b, *, seq, alpha, gate_row, tm=512):
    n, d = x2d.shape
    tm = min(tm, seq)
    blocks_per_batch = seq // tm
    kern = functools.partial(_residual_ln_kernel, alpha=alpha, gate_row=gate_row)
    return pl.pallas_call(
        kern,
        grid=(n // tm,),
        in_specs=[
            pl.BlockSpec((tm, d), lambda i: (i, 0)),
            pl.BlockSpec((tm, d), lambda i: (i, 0)),
            pl.BlockSpec((None, 6, d), lambda i: (i // blocks_per_batch, 0, 0)),
            _resident((1, d), lambda i: (0, 0)),
            _resident((1, d), lambda i: (0, 0)),
        ],
        out_specs=pl.BlockSpec((tm, d), lambda i: (i, 0)),
        out_shape=jax.ShapeDtypeStruct((n, d), F32),
        compiler_params=_params(("arbitrary",)),
        name="residual_ln",
    )(x2d, y2d, mod_l, ln_g, ln_b)


def kernel(x, c, ada_w, ada_b, ln_g, ln_b, peer_wq, peer_keys, peer_u, peer_v, ev_w_in, ev_w_pool,
           ev_pool_scale, ev_conv_w, ev_conv_b, ev_w_a, ev_b_a, ev_w_x, ev_b_x, ev_lam, ev_w_out,
           od_w_in, od_b_f, od_w_out):
    batch, seq, d = x.shape
    depth = ada_w.shape[0]
    n = batch * seq
    alpha = (2.0 * depth) ** 0.25
    d_mix = od_w_out.shape[1]

    c_pad = jnp.zeros((8, d), F32).at[:batch].set(c)
    mod = ada_mod(c_pad, ada_w, ada_b)[:, :batch].reshape(depth, batch, 6, d)

    xf = x.reshape(n, d)
    for l in range(depth):
        mod_l = mod[l]
        g0 = ln_g[l, 0].reshape(1, d)
        b0 = ln_b[l, 0].reshape(1, d)
        g1 = ln_g[l, 1].reshape(1, d)
        b1 = ln_b[l, 1].reshape(1, d)
        if l % 2 == 0:
            e = l // 2
            z = mod_matmul(xf, mod_l, ev_w_in[e].astype(BF16), seq=seq, shift_row=0, scale_row=1,
                           out_dtype=F32)
            mixed = even_mix(
                z.reshape(batch, seq, -1), ev_w_pool[e].astype(BF16), ev_pool_scale[e].reshape(1, -1),
                ev_conv_w[e], ev_conv_b[e].reshape(1, -1), ev_w_a[e].astype(BF16),
                ev_b_a[e].reshape(1, -1), ev_w_x[e].astype(BF16), ev_b_x[e].reshape(1, -1),
                ev_lam[e].reshape(1, -1))
            xf = proj_ln(mixed.reshape(n, -1), ev_w_out[e].astype(BF16), xf, mod_l, g0, b0,
                         seq=seq, alpha=alpha, gate_row=2)
        else:
            o = l // 2
            w_in = od_w_in[o]
            w_f = jnp.zeros((d, 128), F32).at[:, :N_HEADS_ATTN].set(w_in[:, 3 * d_mix:])
            wf_hi = w_f.astype(BF16)
            wf_lo = (w_f - wf_hi.astype(F32)).astype(BF16)
            b_f = jnp.zeros((1, 128), F32).at[0, :N_HEADS_ATTN].set(od_b_f[o])
            qkv, f_cum = odd_in(xf, mod_l, w_in[:, :3 * d_mix].astype(BF16), wf_hi, wf_lo, b_f, seq=seq)
            f_keys = f_cum[:, :N_HEADS_ATTN].reshape(batch, seq, N_HEADS_ATTN)
            f_keys = jnp.transpose(f_keys, (0, 2, 1)).reshape(batch * N_HEADS_ATTN, 1, seq)
            k_t = qkv[:, d_mix:2 * d_mix].reshape(batch, seq, N_HEADS_ATTN, d_mix // N_HEADS_ATTN)
            k_t = jnp.transpose(k_t, (0, 2, 3, 1))
            attn = attention(qkv, k_t, f_keys, batch=batch, seq=seq)
            xf = proj_ln(attn, od_w_out[o].astype(BF16), xf, mod_l, g0, b0,
                         seq=seq, alpha=alpha, gate_row=2)

        keys = peer_keys[l].reshape(2 * PEER_HEADS, N_KEYS, -1).astype(BF16)
        q, h2 = mod_matmul(xf, mod_l, peer_wq[l].astype(BF16), seq=seq, shift_row=3, scale_row=4,
                           out_dtype=BF16, emit_h=True)
        gmat = peer_route(q, keys, seq=seq)
        y = peer_dense(h2, gmat, peer_u[l].T.astype(BF16), peer_v[l].astype(BF16), seq=seq)
        xf = residual_ln(xf, y, mod_l, g1, b1, seq=seq, alpha=alpha, gate_row=5)
    return xf.reshape(batch, seq, d)
```

```python
import functools
import math

import jax
import jax.numpy as jnp
from jax import lax
from jax.experimental import pallas as pl
from jax.experimental.pallas import tpu as pltpu

F32 = jnp.float32
BF16 = jnp.bfloat16
I32 = jnp.int32

LN_EPS = 1e-5
POOL_WINDOWS = (2, 4, 8, 16)
CONV_WIDTH = 4
LRU_BLOCKS = 8
LRU_C = 8.0
N_HEADS_ATTN = 16
PEER_HEADS = 8
PEER_TOPK = 16
N_KEYS = 128

V7X_VMEM_BYTES = 64 * 1024 * 1024
VMEM_LIMIT = 56 * 1024 * 1024
NEG_INF = float("-inf")
LOG2E = math.log2(math.e)


def _params(sem):
    return pltpu.CompilerParams(dimension_semantics=sem, vmem_limit_bytes=VMEM_LIMIT)


def _resident(shape, index_map):
    return pl.BlockSpec(shape, index_map, pipeline_mode=pl.Buffered(1))


def _gelu(x):
    c = math.sqrt(2.0 / math.pi)
    return 0.5 * x * (1.0 + jnp.tanh(c * (x + 0.044715 * (x * x * x))))


def _log_sigmoid(x):
    return jnp.minimum(x, 0.0) - jnp.log1p(jnp.exp(-jnp.abs(x)))


def _sigmoid(x):
    return 1.0 / (1.0 + jnp.exp(-x))


def _layer_norm(r, g, b):
    mu = jnp.mean(r, axis=-1, keepdims=True)
    d = r - mu
    var = jnp.mean(d * d, axis=-1, keepdims=True)
    return d * lax.rsqrt(var + LN_EPS) * g + b


def _dot_nt(a, b):
    return lax.dot_general(a, b, (((1,), (1,)), ((), ())), preferred_element_type=F32)


def _ada_kernel(c_ref, w_ref, b_ref, o_ref):
    c = c_ref[...]
    ca = c * _sigmoid(c)
    o_ref[...] = jnp.dot(ca, w_ref[...], preferred_element_type=F32) + b_ref[...]


def ada_mod(c_pad, ada_w, ada_b):
    depth, d, n6 = ada_w.shape
    rows = c_pad.shape[0]
    tn = 1024
    return pl.pallas_call(
        _ada_kernel,
        grid=(depth, n6 // tn),
        in_specs=[
            pl.BlockSpec((rows, d), lambda l, j: (0, 0)),
            pl.BlockSpec((None, d, tn), lambda l, j: (l, 0, j)),
            pl.BlockSpec((None, 1, tn), lambda l, j: (l, 0, j)),
        ],
        out_specs=pl.BlockSpec((None, rows, tn), lambda l, j: (l, 0, j)),
        out_shape=jax.ShapeDtypeStruct((depth, rows, n6), F32),
        compiler_params=_params(("arbitrary", "arbitrary")),
        name="ada_mod",
    )(c_pad, ada_w, ada_b.reshape(depth, 1, n6))


def _mod_matmul_kernel(x_ref, mod_ref, w_ref, o_ref, h_ref, *, shift_row, scale_row):
    @pl.when(pl.program_id(1) == 0)
    def _():
        sh = mod_ref[shift_row:shift_row + 1, :]
        sc = mod_ref[scale_row:scale_row + 1, :]
        h_ref[...] = (x_ref[...] * (1.0 + sc) + sh).astype(BF16)

    o_ref[...] = jnp.dot(h_ref[...], w_ref[...], preferred_element_type=F32).astype(o_ref.dtype)


def mod_matmul(x2d, mod_l, w_bf16, *, seq, shift_row, scale_row, out_dtype, emit_h=False,
               tm=1024, tn=1024):
    n, d = x2d.shape
    nout = w_bf16.shape[1]
    tm = min(tm, seq)
    blocks_per_batch = seq // tm
    kern = functools.partial(_mod_matmul_kernel, shift_row=shift_row, scale_row=scale_row)
    out_specs = [pl.BlockSpec((tm, tn), lambda i, j: (i, j))]
    out_shape = [jax.ShapeDtypeStruct((n, nout), out_dtype)]
    scratch = []
    if emit_h:
        out_specs.append(pl.BlockSpec((tm, d), lambda i, j: (i, 0)))
        out_shape.append(jax.ShapeDtypeStruct((n, d), BF16))
    else:
        scratch.append(pltpu.VMEM((tm, d), BF16))
    res = pl.pallas_call(
        kern,
        grid=(n // tm, nout // tn),
        in_specs=[
            pl.BlockSpec((tm, d), lambda i, j: (i, 0)),
            pl.BlockSpec((None, 6, d), lambda i, j: (i // blocks_per_batch, 0, 0)),
            pl.BlockSpec((d, tn), lambda i, j: (0, j)),
        ],
        out_specs=out_specs,
        out_shape=out_shape,
        scratch_shapes=scratch,
        compiler_params=_params(("arbitrary", "arbitrary")),
        name="mod_matmul",
    )(x2d, mod_l, w_bf16)
    return res if emit_h else res[0]


def _cumsum_rows(x):
    rows = x.shape[0]
    ridx = lax.broadcasted_iota(I32, x.shape, 0)
    d = 1
    while d < rows:
        x = x + jnp.where(ridx >= d, pltpu.roll(x, d, 0), 0.0)
        d *= 2
    return x


def _odd_in_kernel(x_ref, mod_ref, w_ref, wfh_ref, wfl_ref, bf_ref, o_ref, f_ref,
                   h_ref, carry_ref, *, blocks_per_batch, d_mix, q_scale, tn):
    i = pl.program_id(0)
    j = pl.program_id(1)

    @pl.when(j == 0)
    def _():
        sh = mod_ref[0:1, :]
        sc = mod_ref[1:2, :]
        h = x_ref[...] * (1.0 + sc) + sh
        h_hi = h.astype(BF16)
        h_ref[...] = h_hi
        h_lo = (h - h_hi.astype(F32)).astype(BF16)
        zf = (jnp.dot(h_hi, wfh_ref[...], preferred_element_type=F32)
              + jnp.dot(h_lo, wfh_ref[...], preferred_element_type=F32)
              + jnp.dot(h_hi, wfl_ref[...], preferred_element_type=F32))
        logf = _log_sigmoid(zf + bf_ref[...])

        @pl.when(i % blocks_per_batch == 0)
        def _():
            carry_ref[...] = jnp.zeros_like(carry_ref)

        cs = _cumsum_rows(logf) + carry_ref[0:1, :]
        f_ref[...] = cs
        carry_ref[...] = jnp.broadcast_to(cs[cs.shape[0] - 1:, :], carry_ref.shape)

    z = jnp.dot(h_ref[...], w_ref[...], preferred_element_type=F32)
    scale = jnp.where(j * tn < d_mix, q_scale, 1.0)
    o_ref[...] = (z * scale).astype(o_ref.dtype)


def odd_in(x2d, mod_l, w_qkv, wf_hi, wf_lo, b_f_pad, *, seq, tm=1024, tn=1024):
    n, d = x2d.shape
    nout = w_qkv.shape[1]
    d_mix = nout // 3
    tm = min(tm, seq)
    blocks_per_batch = seq // tm
    head_dim = d_mix // N_HEADS_ATTN
    kern = functools.partial(_odd_in_kernel, blocks_per_batch=blocks_per_batch, d_mix=d_mix,
                             q_scale=head_dim ** -0.5 * LOG2E, tn=tn)
    return pl.pallas_call(
        kern,
        grid=(n // tm, nout // tn),
        in_specs=[
            pl.BlockSpec((tm, d), lambda i, j: (i, 0)),
            pl.BlockSpec((None, 6, d), lambda i, j: (i // blocks_per_batch, 0, 0)),
            pl.BlockSpec((d, tn), lambda i, j: (0, j)),
            _resident((d, 128), lambda i, j: (0, 0)),
            _resident((d, 128), lambda i, j: (0, 0)),
            _resident((1, 128), lambda i, j: (0, 0)),
        ],
        out_specs=[
            pl.BlockSpec((tm, tn), lambda i, j: (i, j)),
            pl.BlockSpec((tm, 128), lambda i, j: (i, 0)),
        ],
        out_shape=[
            jax.ShapeDtypeStruct((n, nout), BF16),
            jax.ShapeDtypeStruct((n, 128), F32),
        ],
        scratch_shapes=[pltpu.VMEM((tm, d), BF16), pltpu.VMEM((8, 128), F32)],
        compiler_params=_params(("arbitrary", "arbitrary")),
        name="odd_in",
    )(x2d, mod_l, w_qkv, wf_hi, wf_lo, b_f_pad)


def _attn_kernel(q_ref, kt_ref, v_ref, fk_ref, o_ref, sa_ref, sb_ref, *, tq):
    qi = pl.program_id(2)
    f_ref0 = fk_ref[0:1, pl.ds(pl.multiple_of(qi * tq, tq), tq)][:, 0:1]
    dh = q_ref.shape[1]

    def scores(c, s_ref):
        start = pl.multiple_of(c * tq, tq)
        s = jnp.dot(q_ref[...], kt_ref[:, pl.ds(start, tq)], preferred_element_type=F32)
        s_ref[...] = s + (f_ref0 - fk_ref[0:1, pl.ds(start, tq)]) * LOG2E

    def update(c, s_ref, carry, diagonal):
        m, l, acc = carry
        start = pl.multiple_of(c * tq, tq)
        if diagonal:
            row = lax.broadcasted_iota(I32, (tq, tq), 0)
            col = lax.broadcasted_iota(I32, (tq, tq), 1)
            s_ref[...] = jnp.where(col <= row, s_ref[...], NEG_INF)
        m_new = jnp.maximum(m, jnp.max(s_ref[...], axis=-1, keepdims=True))
        alpha = jnp.exp2(m - m_new)
        p = jnp.exp2(s_ref[...] - m_new)
        l = alpha * l + jnp.sum(p, axis=-1, keepdims=True)
        v = v_ref[pl.ds(start, tq), :]
        acc = alpha * acc + jnp.dot(p.astype(BF16), v, preferred_element_type=F32)
        return m_new, l, acc

    scores(0, sa_ref)

    def pair(p, carry):
        scores(2 * p + 1, sb_ref)
        carry = update(2 * p, sa_ref, carry, False)
        scores(2 * p + 2, sa_ref)
        return update(2 * p + 1, sb_ref, carry, False)

    def odd_tail(carry):
        scores(qi, sb_ref)
        carry = update(qi - 1, sa_ref, carry, False)
        return update(qi, sb_ref, carry, True)

    def even_tail(carry):
        return update(qi, sa_ref, carry, True)

    init = (jnp.full((tq, 1), NEG_INF, F32), jnp.zeros((tq, 1), F32), jnp.zeros((tq, dh), F32))
    carry = lax.fori_loop(0, qi // 2, pair, init)
    m, l, acc = lax.cond(qi % 2 == 1, odd_tail, even_tail, carry)
    o_ref[...] = (acc / l).astype(o_ref.dtype)


def attention(qkv, k_t, f_keys, *, batch, seq, tq=512):
    n = qkv.shape[0]
    d_mix = qkv.shape[1] // 3
    heads = N_HEADS_ATTN
    dh = d_mix // heads
    tq = min(tq, seq)
    nq = seq // tq
    kern = functools.partial(_attn_kernel, tq=tq)
    return pl.pallas_call(
        kern,
        grid=(batch, heads, nq),
        in_specs=[
            pl.BlockSpec((tq, dh), lambda b, h, i: (b * nq + i, h)),
            pl.BlockSpec((None, None, dh, seq), lambda b, h, i: (b, h, 0, 0)),
            pl.BlockSpec((seq, dh), lambda b, h, i: (b, 2 * heads + h)),
            pl.BlockSpec((None, 1, seq), lambda b, h, i: (b * heads + h, 0, 0)),
        ],
        out_specs=pl.BlockSpec((tq, dh), lambda b, h, i: (b * nq + i, h)),
        out_shape=jax.ShapeDtypeStruct((n, d_mix), BF16),
        scratch_shapes=[pltpu.VMEM((tq, tq), F32), pltpu.VMEM((tq, tq), F32)],
        compiler_params=_params(("arbitrary", "arbitrary", "arbitrary")),
        name="attention",
    )(qkv, k_t, qkv, f_keys)


HALO = 16


def _neg_expm1(x):
    p = 1.0 + x / 10.0
    for k in range(9, 1, -1):
        p = 1.0 + (x / k) * p
    return jnp.where(x > -0.35, -(x * p), 1.0 - jnp.exp(x))


def _even_mix_kernel(z_ref, wpool_ref, pscale_ref, convw_ref, convb_ref, wa_ref, ba_ref,
                     wx_ref, bx_ref, lam_ref, o_ref, pbuf, cbuf, hstate, *, ts, d_pool, d_lru):
    sb = pl.program_id(1)
    group = d_pool // len(POOL_WINDOWS)
    blk = d_lru // LRU_BLOCKS

    @pl.when(sb == 0)
    def _():
        pbuf[0:HALO, :] = jnp.zeros((HALO, d_pool), F32)
        cbuf[0:HALO, :] = jnp.zeros((HALO, d_lru), F32)
        hstate[...] = jnp.zeros_like(hstate)

    up = z_ref[:, 0:d_pool]
    pbuf[HALO:HALO + ts, :] = up
    pos = sb * ts + lax.broadcasted_iota(I32, (ts, group), 0)
    ya = []
    for g, w in enumerate(POOL_WINDOWS):
        lo, hi = g * group, (g + 1) * group
        need = w - 1
        cur = pbuf[HALO - need:HALO + ts, lo:hi]
        span = 1
        while span < w:
            rows = cur.shape[0] - span
            cur = cur[span:span + rows, :] + cur[0:rows, :]
            span *= 2
        cnt = jnp.minimum(pos + 1, w).astype(F32)
        pooled = cur / cnt - up[:, lo:hi]
        y = jnp.dot(pooled.astype(BF16), wpool_ref[g], preferred_element_type=F32)
        ya.append(y * pscale_ref[0:1, lo:hi])
    o_ref[:, 0:d_pool] = jnp.concatenate(ya, axis=-1).astype(o_ref.dtype)
    pbuf[0:HALO, :] = pbuf[ts:ts + HALO, :]

    ul = z_ref[:, d_pool:d_pool + d_lru]
    ug = z_ref[:, d_pool + d_lru:d_pool + 2 * d_lru]
    cbuf[HALO:HALO + ts, :] = ul
    xc = jnp.broadcast_to(convb_ref[0:1, :], (ts, d_lru))
    for k in range(CONV_WIDTH):
        off = HALO - (CONV_WIDTH - 1) + k
        xc = xc + cbuf[off:off + ts, :] * convw_ref[k:k + 1, :]
    cbuf[0:HALO, :] = cbuf[ts:ts + HALO, :]

    xb = xc.astype(BF16)
    ra, ia = [], []
    for hb in range(LRU_BLOCKS):
        xs = xb[:, hb * blk:(hb + 1) * blk]
        ra.append(jnp.dot(xs, wa_ref[hb], preferred_element_type=F32))
        ia.append(jnp.dot(xs, wx_ref[hb], preferred_element_type=F32))
    r_gate = _sigmoid(jnp.concatenate(ra, axis=-1) + ba_ref[0:1, :])
    i_gate = _sigmoid(jnp.concatenate(ia, axis=-1) + bx_ref[0:1, :])
    log_a = LRU_C * r_gate * _log_sigmoid(lam_ref[0:1, :])
    a = jnp.exp(log_a)
    mult = jnp.sqrt(jnp.maximum(_neg_expm1(2.0 * log_a), 0.0))
    b = mult * i_gate * xc

    ridx = lax.broadcasted_iota(I32, (ts, d_lru), 0)
    d = 1
    while d < ts:
        keep = ridx >= d
        a_sh = jnp.where(keep, pltpu.roll(a, d, 0), 1.0)
        b_sh = jnp.where(keep, pltpu.roll(b, d, 0), 0.0)
        b = a * b_sh + b
        a = a * a_sh
        d *= 2
    h = b + a * hstate[0:1, :]
    hstate[...] = jnp.broadcast_to(h[ts - 1:ts, :], hstate.shape)
    o_ref[:, d_pool:d_pool + d_lru] = (h * _gelu(ug)).astype(o_ref.dtype)


def even_mix(z3d, w_pool, pool_scale, conv_w, conv_b, w_a, b_a, w_x, b_x, lam, *, ts=256):
    batch, seq, dz = z3d.shape
    d_pool = pool_scale.shape[-1]
    d_lru = lam.shape[-1]
    ts = min(ts, seq)
    group = d_pool // len(POOL_WINDOWS)
    blk = d_lru // LRU_BLOCKS
    kern = functools.partial(_even_mix_kernel, ts=ts, d_pool=d_pool, d_lru=d_lru)
    const2 = lambda b, s: (0, 0)
    const3 = lambda b, s: (0, 0, 0)
    return pl.pallas_call(
        kern,
        grid=(batch, seq // ts),
        in_specs=[
            pl.BlockSpec((None, ts, dz), lambda b, s: (b, s, 0)),
            _resident((len(POOL_WINDOWS), group, group), const3),
            _resident((1, d_pool), const2),
            _resident((CONV_WIDTH, d_lru), const2),
            _resident((1, d_lru), const2),
            _resident((LRU_BLOCKS, blk, blk), const3),
            _resident((1, d_lru), const2),
            _resident((LRU_BLOCKS, blk, blk), const3),
            _resident((1, d_lru), const2),
            _resident((1, d_lru), const2),
        ],
        out_specs=pl.BlockSpec((None, ts, d_pool + d_lru), lambda b, s: (b, s, 0)),
        out_shape=jax.ShapeDtypeStruct((batch, seq, d_pool + d_lru), BF16),
        scratch_shapes=[
            pltpu.VMEM((HALO + ts, d_pool), F32),
            pltpu.VMEM((HALO + ts, d_lru), F32),
            pltpu.VMEM((8, d_lru), F32),
        ],
        compiler_params=_params(("arbitrary", "arbitrary")),
        name="even_mix",
    )(z3d, w_pool, pool_scale, conv_w, conv_b, w_a, b_a, w_x, b_x, lam)


def _proj_ln_kernel(a_ref, w_ref, x_ref, mod_ref, g_ref, b_ref, o_ref, *, alpha, gate_row):
    y = jnp.dot(a_ref[...], w_ref[...], preferred_element_type=F32)
    gate = mod_ref[gate_row:gate_row + 1, :]
    r = alpha * x_ref[...] + (1.0 + gate) * y
    o_ref[...] = _layer_norm(r, g_ref[...], b_ref[...])


def proj_ln(a2d, w_bf16, x2d, mod_l, ln_g, ln_b, *, seq, alpha, gate_row, tm=512):
    n, d = x2d.shape
    k = a2d.shape[1]
    tm = min(tm, seq)
    blocks_per_batch = seq // tm
    kern = functools.partial(_proj_ln_kernel, alpha=alpha, gate_row=gate_row)
    return pl.pallas_call(
        kern,
        grid=(n // tm,),
        in_specs=[
            pl.BlockSpec((tm, k), lambda i: (i, 0)),
            _resident((k, d), lambda i: (0, 0)),
            pl.BlockSpec((tm, d), lambda i: (i, 0)),
            pl.BlockSpec((None, 6, d), lambda i: (i // blocks_per_batch, 0, 0)),
            _resident((1, d), lambda i: (0, 0)),
            _resident((1, d), lambda i: (0, 0)),
        ],
        out_specs=pl.BlockSpec((tm, d), lambda i: (i, 0)),
        out_shape=jax.ShapeDtypeStruct((n, d), F32),
        compiler_params=_params(("arbitrary",)),
        name="proj_ln",
    )(a2d, w_bf16, x2d, mod_l, ln_g, ln_b)


def _top16_rows(s, kidx, sentinel):
    t = s.shape[1]
    r16 = lax.broadcasted_iota(I32, (PEER_TOPK, t), 0)
    vals = jnp.zeros((PEER_TOPK, t), F32)
    idxs = jnp.zeros((PEER_TOPK, t), I32)
    for r in range(PEER_TOPK):
        m = jnp.max(s, axis=0, keepdims=True)
        am = jnp.min(jnp.where(s == m, kidx, sentinel), axis=0, keepdims=True)
        s = jnp.where(kidx == am, NEG_INF, s)
        vals = jnp.where(r16 == r, m, vals)
        idxs = jnp.where(r16 == r, am, idxs)
    return vals, idxs


def _take_rows16(table, sel):
    out = jnp.zeros(sel.shape, table.dtype)
    for s in range(PEER_TOPK):
        out = jnp.where(sel == s, table[s:s + 1, :], out)
    return out


SUBLANES = 8
BUILD_UNROLL = 16


def _peer_route_kernel(q_ref, keys_ref, g_ref,
                       gs_ref, sc_ref, val_ref, idx_ref, e_ref, w_ref, et_ref, wt_ref, *, tr):
    nhp = 2 * PEER_HEADS
    for hp in range(nhp):
        sc_ref[hp] = _dot_nt(keys_ref[hp], q_ref[:, hp * N_KEYS:(hp + 1) * N_KEYS])

    kidx = lax.broadcasted_iota(I32, (N_KEYS, tr), 0)

    def first_topk(hp, carry):
        vals, idxs = _top16_rows(sc_ref[hp], kidx, N_KEYS)
        val_ref[hp] = vals
        idx_ref[hp] = idxs
        return carry

    lax.fori_loop(0, nhp, first_topk, 0)

    i16 = lax.broadcasted_iota(I32, (16, tr), 0)
    i8 = lax.broadcasted_iota(I32, (8, tr), 0)
    flat = jnp.concatenate([
        i16 * 16, i8 * 16 + 1, i8 * 16 + 2, i8 * 16 + 3,
        i16, 16 + i8, 32 + i8,
    ], axis=0)
    valid = jnp.concatenate([
        i16 < 16, i8 < 8, i8 < 5, i8 < 4,
        i16 >= 4, i8 >= 4, i8 == 4,
    ], axis=0)

    def second_topk(hd, carry):
        s1 = val_ref[2 * hd]
        s2 = val_ref[2 * hd + 1]
        cand = jnp.concatenate([
            s1 + s2[0:1, :], s1[0:8, :] + s2[1:2, :], s1[0:8, :] + s2[2:3, :], s1[0:8, :] + s2[3:4, :],
            s1[0:1, :] + s2, s1[1:2, :] + s2[0:8, :], s1[2:3, :] + s2[0:8, :],
        ], axis=0)
        cand = jnp.where(valid, cand, NEG_INF)
        top, fsel = _top16_rows(cand, flat, 256)
        e = jnp.exp(top - top[0:1, :])
        gate = e / jnp.sum(e, axis=0, keepdims=True)
        a = _take_rows16(idx_ref[2 * hd], fsel >> 4)
        b = _take_rows16(idx_ref[2 * hd + 1], fsel & 15)
        row0 = pl.multiple_of(hd * PEER_TOPK, PEER_TOPK)
        e_ref[pl.ds(row0, PEER_TOPK), :] = a * N_KEYS + b
        w_ref[pl.ds(row0, PEER_TOPK), :] = gate
        return carry

    lax.fori_loop(0, PEER_HEADS, second_topk, 0)

    n_act = PEER_HEADS * PEER_TOPK
    for c in range(tr // n_act):
        et_ref[c * n_act:(c + 1) * n_act, :] = e_ref[:, c * n_act:(c + 1) * n_act].T
        wt_ref[c * n_act:(c + 1) * n_act, :] = w_ref[:, c * n_act:(c + 1) * n_act].T

    iota_rows = lax.broadcasted_iota(I32, (N_KEYS, n_act), 0)
    sub_iota = lax.broadcasted_iota(I32, (SUBLANES, N_KEYS), 0)

    def build(step, carry):
        base = pl.multiple_of(step * BUILD_UNROLL, BUILD_UNROLL)
        erows = et_ref[pl.ds(base, BUILD_UNROLL), :]
        wrows = wt_ref[pl.ds(base, BUILD_UNROLL), :]
        for u in range(BUILD_UNROLL):
            erow = erows[u:u + 1, :]
            wrow = wrows[u:u + 1, :]
            pt = jnp.where(iota_rows == (erow >> 7), wrow, 0.0).astype(BF16)
            qt = jnp.where(iota_rows == (erow & (N_KEYS - 1)), 1.0, 0.0).astype(BF16)
            gs_ref[u * N_KEYS:(u + 1) * N_KEYS, :] = _dot_nt(pt, qt)
        for gi in range(BUILD_UNROLL // SUBLANES):
            grp = step * (BUILD_UNROLL // SUBLANES) + gi
            for k in range(N_KEYS // SUBLANES):
                tiles = []
                for t in range(SUBLANES):
                    row0 = (gi * SUBLANES + t) * N_KEYS + k * SUBLANES
                    tiles.append(gs_ref[row0:row0 + SUBLANES, :])
                for dist in (4, 2, 1):
                    take_hi = (sub_iota & dist) != 0
                    for t in range(SUBLANES):
                        if t & dist:
                            continue
                        lo, hi = tiles[t], tiles[t + dist]
                        tiles[t] = jnp.where(take_hi, pltpu.roll(hi, dist, 0), lo)
                        tiles[t + dist] = jnp.where(take_hi, hi, pltpu.roll(lo, SUBLANES - dist, 0))
                for s in range(SUBLANES):
                    g_ref[grp, k * SUBLANES + s] = tiles[s]
        return carry

    lax.fori_loop(0, tr // BUILD_UNROLL, build, 0)


def peer_route(q2d, keys_bf16, *, seq, tr=256):
    n, d = q2d.shape
    tr = min(tr, seq)
    nhp = 2 * PEER_HEADS
    n_act = PEER_HEADS * PEER_TOPK
    kern = functools.partial(_peer_route_kernel, tr=tr)
    return pl.pallas_call(
        kern,
        grid=(n // tr,),
        in_specs=[
            pl.BlockSpec((tr, d), lambda i: (i, 0)),
            _resident(keys_bf16.shape, lambda i: (0, 0, 0)),
        ],
        out_specs=pl.BlockSpec((tr // SUBLANES, N_KEYS, SUBLANES, N_KEYS), lambda i: (i, 0, 0, 0)),
        out_shape=jax.ShapeDtypeStruct((n // SUBLANES, N_KEYS, SUBLANES, N_KEYS), F32),
        scratch_shapes=[
            pltpu.VMEM((BUILD_UNROLL * N_KEYS, N_KEYS), F32),
            pltpu.VMEM((nhp, N_KEYS, tr), F32),
            pltpu.VMEM((nhp, PEER_TOPK, tr), F32),
            pltpu.VMEM((nhp, PEER_TOPK, tr), I32),
            pltpu.VMEM((n_act, tr), I32),
            pltpu.VMEM((n_act, tr), F32),
            pltpu.VMEM((tr, n_act), I32),
            pltpu.VMEM((tr, n_act), F32),
        ],
        compiler_params=_params(("arbitrary",)),
        name="peer_route",
    )(q2d, keys_bf16)


DENSE_CHUNK = 256


def _peer_dense_kernel(h_ref, g_ref, ut_ref, v_ref, o_ref, act_ref):
    j = pl.program_id(1)
    rows = g_ref.shape[1]
    tm = h_ref.shape[0]

    @pl.when(j == 0)
    def _():
        o_ref[...] = jnp.zeros_like(o_ref)

    per = DENSE_CHUNK // N_KEYS
    for c in range(rows // per):
        ccols = slice(c * DENSE_CHUNK, (c + 1) * DENSE_CHUNK)
        z = jnp.dot(h_ref[...], ut_ref[:, ccols], preferred_element_type=F32)
        for r in range(per):
            row = c * per + r
            gate = g_ref[:, row, :, :].reshape(tm, N_KEYS)
            act_ref[:, row * N_KEYS:(row + 1) * N_KEYS] = (
                _gelu(z[:, r * N_KEYS:(r + 1) * N_KEYS]) * gate).astype(BF16)
    half = act_ref.shape[1] // 2
    o_ref[...] += jnp.dot(act_ref[:, :half], v_ref[:half, :], preferred_element_type=F32)
    o_ref[...] += jnp.dot(act_ref[:, half:], v_ref[half:, :], preferred_element_type=F32)


def peer_dense(h2d, g3d, ut_bf16, v_bf16, *, seq, tm=1024, te=1024):
    n, d = h2d.shape
    n_exp = v_bf16.shape[0]
    tm = min(tm, seq)
    return pl.pallas_call(
        _peer_dense_kernel,
        grid=(n // tm, n_exp // te),
        in_specs=[
            pl.BlockSpec((tm, d), lambda i, j: (i, 0), pipeline_mode=pl.Buffered(1)),
            pl.BlockSpec((tm // SUBLANES, te // N_KEYS, SUBLANES, N_KEYS), lambda i, j: (i, j, 0, 0)),
            pl.BlockSpec((d, te), lambda i, j: (0, j)),
            pl.BlockSpec((te, d), lambda i, j: (j, 0)),
        ],
        out_specs=pl.BlockSpec((tm, d), lambda i, j: (i, 0)),
        out_shape=jax.ShapeDtypeStruct((n, d), F32),
        scratch_shapes=[pltpu.VMEM((tm, te), BF16)],
        compiler_params=_params(("arbitrary", "arbitrary")),
        name="peer_dense",
    )(h2d, g3d, ut_bf16, v_bf16)


def _residual_ln_kernel(x_ref, y_ref, mod_ref, g_ref, b_ref, o_ref, *, alpha, gate_row):
    gate = mod_ref[gate_row:gate_row + 1, :]
    r = alpha * x_ref[...] + (1.0 + gate) * y_ref[...]
    o_ref[...] = _layer_norm(r, g_ref[...], b_ref[...])


def residual_ln(x2d, y2d, mod_l, ln_g, ln_b, *, seq, alpha, gate_row, tm=512):
    n, d = x2d.shape
    tm = min(tm, seq)
    blocks_per_batch = seq // tm
    kern = functools.partial(_residual_ln_kernel, alpha=alpha, gate_row=gate_row)
    return pl.pallas_call(
        kern,
        grid=(n // tm,),
        in_specs=[
            pl.BlockSpec((tm, d), lambda i: (i, 0)),
            pl.BlockSpec((tm, d), lambda i: (i, 0)),
            pl.BlockSpec((None, 6, d), lambda i: (i // blocks_per_batch, 0, 0)),
            _resident((1, d), lambda i: (0, 0)),
            _resident((1, d), lambda i: (0, 0)),
        ],
        out_specs=pl.BlockSpec((tm, d), lambda i: (i, 0)),
        out_shape=jax.ShapeDtypeStruct((n, d), F32),
        compiler_params=_params(("arbitrary",)),
        name="residual_ln",
    )(x2d, y2d, mod_l, ln_g, ln_b)


def kernel(x, c, ada_w, ada_b, ln_g, ln_b, peer_wq, peer_keys, peer_u, peer_v, ev_w_in, ev_w_pool,
           ev_pool_scale, ev_conv_w, ev_conv_b, ev_w_a, ev_b_a, ev_w_x, ev_b_x, ev_lam, ev_w_out,
           od_w_in, od_b_f, od_w_out):
    batch, seq, d = x.shape
    depth = ada_w.shape[0]
    n = batch * seq
    alpha = (2.0 * depth) ** 0.25
    d_mix = od_w_out.shape[1]

    c_pad = jnp.zeros((8, d), F32).at[:batch].set(c)
    mod = ada_mod(c_pad, ada_w, ada_b)[:, :batch].reshape(depth, batch, 6, d)

    xf = x.reshape(n, d)
    for l in range(depth):
        mod_l = mod[l]
        g0 = ln_g[l, 0].reshape(1, d)
        b0 = ln_b[l, 0].reshape(1, d)
        g1 = ln_g[l, 1].reshape(1, d)
        b1 = ln_b[l, 1].reshape(1, d)
        if l % 2 == 0:
            e = l // 2
            z = mod_matmul(xf, mod_l, ev_w_in[e].astype(BF16), seq=seq, shift_row=0, scale_row=1,
                           out_dtype=F32)
            mixed = even_mix(
                z.reshape(batch, seq, -1), ev_w_pool[e].astype(BF16), ev_pool_scale[e].reshape(1, -1),
                ev_conv_w[e], ev_conv_b[e].reshape(1, -1), ev_w_a[e].astype(BF16),
                ev_b_a[e].reshape(1, -1), ev_w_x[e].astype(BF16), ev_b_x[e].reshape(1, -1),
                ev_lam[e].reshape(1, -1))
            xf = proj_ln(mixed.reshape(n, -1), ev_w_out[e].astype(BF16), xf, mod_l, g0, b0,
                         seq=seq, alpha=alpha, gate_row=2)
        else:
            o = l // 2
            w_in = od_w_in[o]
            w_f = jnp.zeros((d, 128), F32).at[:, :N_HEADS_ATTN].set(w_in[:, 3 * d_mix:])
            wf_hi = w_f.astype(BF16)
            wf_lo = (w_f - wf_hi.astype(F32)).astype(BF16)
            b_f = jnp.zeros((1, 128), F32).at[0, :N_HEADS_ATTN].set(od_b_f[o])
            qkv, f_cum = odd_in(xf, mod_l, w_in[:, :3 * d_mix].astype(BF16), wf_hi, wf_lo, b_f, seq=seq)
            f_keys = f_cum[:, :N_HEADS_ATTN].reshape(batch, seq, N_HEADS_ATTN)
            f_keys = jnp.transpose(f_keys, (0, 2, 1)).reshape(batch * N_HEADS_ATTN, 1, seq)
            k_t = qkv[:, d_mix:2 * d_mix].reshape(batch, seq, N_HEADS_ATTN, d_mix // N_HEADS_ATTN)
            k_t = jnp.transpose(k_t, (0, 2, 3, 1))
            attn = attention(qkv, k_t, f_keys, batch=batch, seq=seq)
            xf = proj_ln(attn, od_w_out[o].astype(BF16), xf, mod_l, g0, b0,
                         seq=seq, alpha=alpha, gate_row=2)

        keys = peer_keys[l].reshape(2 * PEER_HEADS, N_KEYS, -1).astype(BF16)
        q, h2 = mod_matmul(xf, mod_l, peer_wq[l].astype(BF16), seq=seq, shift_row=3, scale_row=4,
                           out_dtype=BF16, emit_h=True)
        gmat = peer_route(q, keys, seq=seq)
        y = peer_dense(h2, gmat, peer_u[l].T.astype(BF16), peer_v[l].astype(BF16), seq=seq)
        xf = residual_ln(xf, y, mod_l, g1, b1, seq=seq, alpha=alpha, gate_row=5)
    return xf.reshape(batch, seq, d)
```

```python
import functools
import math

import jax
import jax.numpy as jnp
from jax import lax
from jax.experimental import pallas as pl
from jax.experimental.pallas import tpu as pltpu

F32 = jnp.float32
BF16 = jnp.bfloat16
I32 = jnp.int32

LN_EPS = 1e-5
POOL_WINDOWS = (2, 4, 8, 16)
CONV_WIDTH = 4
LRU_BLOCKS = 8
LRU_C = 8.0
N_HEADS_ATTN = 16
PEER_HEADS = 8
PEER_TOPK = 16
N_KEYS = 128

V7X_VMEM_BYTES = 64 * 1024 * 1024
VMEM_LIMIT = 56 * 1024 * 1024
NEG_INF = float("-inf")
LOG2E = math.log2(math.e)


def _params(sem):
    return pltpu.CompilerParams(dimension_semantics=sem, vmem_limit_bytes=VMEM_LIMIT)


def _resident(shape, index_map):
    return pl.BlockSpec(shape, index_map, pipeline_mode=pl.Buffered(1))


def _gelu(x):
    c = math.sqrt(2.0 / math.pi)
    return 0.5 * x * (1.0 + jnp.tanh(c * (x + 0.044715 * (x * x * x))))


def _log_sigmoid(x):
    return jnp.minimum(x, 0.0) - jnp.log1p(jnp.exp(-jnp.abs(x)))


def _sigmoid(x):
    return 1.0 / (1.0 + jnp.exp(-x))


def _layer_norm(r, g, b):
    mu = jnp.mean(r, axis=-1, keepdims=True)
    d = r - mu
    var = jnp.mean(d * d, axis=-1, keepdims=True)
    return d * lax.rsqrt(var + LN_EPS) * g + b


def _dot_nt(a, b):
    return lax.dot_general(a, b, (((1,), (1,)), ((), ())), preferred_element_type=F32)


def _ada_kernel(c_ref, w_ref, b_ref, o_ref):
    c = c_ref[...]
    ca = c * _sigmoid(c)
    o_ref[...] = jnp.dot(ca, w_ref[...], preferred_element_type=F32) + b_ref[...]


def ada_mod(c_pad, ada_w, ada_b):
    depth, d, n6 = ada_w.shape
    rows = c_pad.shape[0]
    tn = 1024
    return pl.pallas_call(
        _ada_kernel,
        grid=(depth, n6 // tn),
        in_specs=[
            pl.BlockSpec((rows, d), lambda l, j: (0, 0)),
            pl.BlockSpec((None, d, tn), lambda l, j: (l, 0, j)),
            pl.BlockSpec((None, 1, tn), lambda l, j: (l, 0, j)),
        ],
        out_specs=pl.BlockSpec((None, rows, tn), lambda l, j: (l, 0, j)),
        out_shape=jax.ShapeDtypeStruct((depth, rows, n6), F32),
        compiler_params=_params(("arbitrary", "arbitrary")),
        name="ada_mod",
    )(c_pad, ada_w, ada_b.reshape(depth, 1, n6))


def _mod_matmul_kernel(x_ref, mod_ref, w_ref, o_ref, h_ref, *, shift_row, scale_row):
    @pl.when(pl.program_id(1) == 0)
    def _():
        sh = mod_ref[shift_row:shift_row + 1, :]
        sc = mod_ref[scale_row:scale_row + 1, :]
        h_ref[...] = (x_ref[...] * (1.0 + sc) + sh).astype(BF16)

    o_ref[...] = jnp.dot(h_ref[...], w_ref[...], preferred_element_type=F32).astype(o_ref.dtype)


def mod_matmul(x2d, mod_l, w_bf16, *, seq, shift_row, scale_row, out_dtype, emit_h=False,
               tm=1024, tn=1024):
    n, d = x2d.shape
    nout = w_bf16.shape[1]
    tm = min(tm, seq)
    blocks_per_batch = seq // tm
    kern = functools.partial(_mod_matmul_kernel, shift_row=shift_row, scale_row=scale_row)
    out_specs = [pl.BlockSpec((tm, tn), lambda i, j: (i, j))]
    out_shape = [jax.ShapeDtypeStruct((n, nout), out_dtype)]
    scratch = []
    if emit_h:
        out_specs.append(pl.BlockSpec((tm, d), lambda i, j: (i, 0)))
        out_shape.append(jax.ShapeDtypeStruct((n, d), BF16))
    else:
        scratch.append(pltpu.VMEM((tm, d), BF16))
    res = pl.pallas_call(
        kern,
        grid=(n // tm, nout // tn),
        in_specs=[
            pl.BlockSpec((tm, d), lambda i, j: (i, 0)),
            pl.BlockSpec((None, 6, d), lambda i, j: (i // blocks_per_batch, 0, 0)),
            pl.BlockSpec((d, tn), lambda i, j: (0, j)),
        ],
        out_specs=out_specs,
        out_shape=out_shape,
        scratch_shapes=scratch,
        compiler_params=_params(("arbitrary", "arbitrary")),
        name="mod_matmul",
    )(x2d, mod_l, w_bf16)
    return res if emit_h else res[0]


def _cumsum_rows(x):
    rows = x.shape[0]
    ridx = lax.broadcasted_iota(I32, x.shape, 0)
    d = 1
    while d < rows:
        x = x + jnp.where(ridx >= d, pltpu.roll(x, d, 0), 0.0)
        d *= 2
    return x


def _odd_in_kernel(x_ref, mod_ref, w_ref, wkt_ref, wfh_ref, wfl_ref, bf_ref, o_ref, kt_ref, f_ref,
                   h_ref, carry_ref, *, blocks_per_batch, nb, q_scale):
    i = pl.program_id(0)
    j = pl.program_id(1)

    @pl.when(j == 0)
    def _():
        sh = mod_ref[0:1, :]
        sc = mod_ref[1:2, :]
        h = x_ref[...] * (1.0 + sc) + sh
        h_hi = h.astype(BF16)
        h_ref[...] = h_hi
        h_lo = (h - h_hi.astype(F32)).astype(BF16)
        zf = (jnp.dot(h_hi, wfh_ref[...], preferred_element_type=F32)
              + jnp.dot(h_lo, wfh_ref[...], preferred_element_type=F32)
              + jnp.dot(h_hi, wfl_ref[...], preferred_element_type=F32))
        logf = _log_sigmoid(zf + bf_ref[...])

        @pl.when(i % blocks_per_batch == 0)
        def _():
            carry_ref[...] = jnp.zeros_like(carry_ref)

        cs = _cumsum_rows(logf) + carry_ref[0:1, :]
        f_ref[...] = cs
        carry_ref[...] = jnp.broadcast_to(cs[cs.shape[0] - 1:, :], carry_ref.shape)

    is_key = jnp.logical_and(j >= nb, j < 2 * nb)

    @pl.when(jnp.logical_not(is_key))
    def _():
        z = jnp.dot(h_ref[...], w_ref[...], preferred_element_type=F32)
        scale = jnp.where(j < nb, q_scale, 1.0)
        o_ref[...] = (z * scale).astype(o_ref.dtype)

    @pl.when(is_key)
    def _():
        kt_ref[...] = _dot_nt(wkt_ref[...], h_ref[...]).astype(kt_ref.dtype)


def odd_in(x2d, mod_l, w_qv, w_kt, wf_hi, wf_lo, b_f_pad, *, seq, tm=1024, tn=1024):
    n, d = x2d.shape
    d_mix = w_kt.shape[0]
    tm = min(tm, seq)
    blocks_per_batch = seq // tm
    head_dim = d_mix // N_HEADS_ATTN
    nb = d_mix // tn
    kern = functools.partial(_odd_in_kernel, blocks_per_batch=blocks_per_batch, nb=nb,
                             q_scale=head_dim ** -0.5 * LOG2E)

    def qv_block(j):
        return jnp.where(j < nb, j, jnp.where(j < 2 * nb, nb - 1, j - nb))

    def k_block(j):
        return jnp.clip(j - nb, 0, nb - 1)

    return pl.pallas_call(
        kern,
        grid=(n // tm, 3 * nb),
        in_specs=[
            pl.BlockSpec((tm, d), lambda i, j: (i, 0)),
            pl.BlockSpec((None, 6, d), lambda i, j: (i // blocks_per_batch, 0, 0)),
            pl.BlockSpec((d, tn), lambda i, j: (0, qv_block(j))),
            pl.BlockSpec((tn, d), lambda i, j: (k_block(j), 0)),
            _resident((d, 128), lambda i, j: (0, 0)),
            _resident((d, 128), lambda i, j: (0, 0)),
            _resident((1, 128), lambda i, j: (0, 0)),
        ],
        out_specs=[
            pl.BlockSpec((tm, tn), lambda i, j: (i, qv_block(j))),
            pl.BlockSpec((tn, tm), lambda i, j: (k_block(j), i)),
            pl.BlockSpec((tm, 128), lambda i, j: (i, 0)),
        ],
        out_shape=[
            jax.ShapeDtypeStruct((n, 2 * d_mix), BF16),
            jax.ShapeDtypeStruct((d_mix, n), BF16),
            jax.ShapeDtypeStruct((n, 128), F32),
        ],
        scratch_shapes=[pltpu.VMEM((tm, d), BF16), pltpu.VMEM((8, 128), F32)],
        compiler_params=_params(("arbitrary", "arbitrary")),
        name="odd_in",
    )(x2d, mod_l, w_qv, w_kt, wf_hi, wf_lo, b_f_pad)


def _attn_kernel(q_ref, kt_ref, v_ref, fk_ref, o_ref, sa_ref, sb_ref, vaug_ref, *, tq, tk):
    qi = pl.program_id(2)
    f_ref0 = fk_ref[0:1, pl.ds(pl.multiple_of(qi * tq, tq), tk)][:, 0:1]
    dh = q_ref.shape[1]

    @pl.when(qi == 0)
    def _():
        vaug_ref[:, 0:dh] = v_ref[...]
        vaug_ref[:, dh:2 * dh] = jnp.ones((v_ref.shape[0], dh), BF16)
    per_q = tq // tk
    n_full = qi * per_q

    def scores(c, s_ref):
        start = pl.multiple_of(c * tk, tk)
        s = jnp.dot(q_ref[...], kt_ref[:, pl.ds(start, tk)], preferred_element_type=F32)
        s_ref[...] = s + (f_ref0 - fk_ref[0:1, pl.ds(start, tk)]) * LOG2E

    def update(c, s_ref, carry, diag_offset=None):
        m, l, acc = carry
        start = pl.multiple_of(c * tk, tk)
        if diag_offset is not None:
            row = lax.broadcasted_iota(I32, (tq, tk), 0)
            col = lax.broadcasted_iota(I32, (tq, tk), 1) + diag_offset
            s_ref[...] = jnp.where(col <= row, s_ref[...], NEG_INF)
        m_new = jnp.maximum(m, jnp.max(s_ref[...], axis=-1, keepdims=True))
        alpha = jnp.exp2(m - m_new)
        p = jnp.exp2(s_ref[...] - m_new)
        v = vaug_ref[pl.ds(start, tk), :]
        acc = alpha * acc + jnp.dot(p.astype(BF16), v, preferred_element_type=F32)
        return m_new, l, acc

    scores(0, sa_ref)

    def pair(p, carry):
        scores(2 * p + 1, sb_ref)
        carry = update(2 * p, sa_ref, carry)
        scores(2 * p + 2, sa_ref)
        return update(2 * p + 1, sb_ref, carry)

    init = (jnp.full((tq, 1), NEG_INF, F32), jnp.zeros((tq, 1), F32), jnp.zeros((tq, 2 * dh), F32))
    carry = lax.fori_loop(0, n_full // 2, pair, init)
    def diagonal(carry, bufs):
        for dchunk in range(per_q):
            if dchunk + 1 < per_q:
                scores(n_full + dchunk + 1, bufs[(dchunk + 1) % 2])
            carry = update(n_full + dchunk, bufs[dchunk % 2], carry, diag_offset=dchunk * tk)
        return carry

    def even_tail(carry):
        return diagonal(carry, (sa_ref, sb_ref))

    def odd_tail(carry):
        scores(n_full, sb_ref)
        carry = update(n_full - 1, sa_ref, carry)
        return diagonal(carry, (sb_ref, sa_ref))

    if per_q % 2 == 0:
        m, l, acc = even_tail(carry)
    else:
        m, l, acc = lax.cond(n_full % 2 == 1, odd_tail, even_tail, carry)
    o_ref[...] = (acc[:, 0:dh] / acc[:, dh:dh + 1]).astype(o_ref.dtype)


def attention(qv, k_t, f_keys, *, batch, seq, tq=512, tk=512):
    n = qv.shape[0]
    d_mix = qv.shape[1] // 2
    heads = N_HEADS_ATTN
    dh = d_mix // heads
    tq = min(tq, seq)
    tk = min(tk, tq)
    nq = seq // tq
    kern = functools.partial(_attn_kernel, tq=tq, tk=tk)
    return pl.pallas_call(
        kern,
        grid=(batch, heads, nq),
        in_specs=[
            pl.BlockSpec((tq, dh), lambda b, h, i: (b * nq + i, h)),
            pl.BlockSpec((dh, seq), lambda b, h, i: (h, b)),
            pl.BlockSpec((seq, dh), lambda b, h, i: (b, heads + h)),
            pl.BlockSpec((None, 1, seq), lambda b, h, i: (b * heads + h, 0, 0)),
        ],
        out_specs=pl.BlockSpec((tq, dh), lambda b, h, i: (b * nq + i, h)),
        out_shape=jax.ShapeDtypeStruct((n, d_mix), BF16),
        scratch_shapes=[pltpu.VMEM((tq, tk), F32), pltpu.VMEM((tq, tk), F32),
                        pltpu.VMEM((seq, 2 * dh), BF16)],
        compiler_params=_params(("arbitrary", "arbitrary", "arbitrary")),
        name="attention",
    )(qv, k_t, qv, f_keys)


HALO = 16


def _neg_expm1(x):
    p = 1.0 + x / 10.0
    for k in range(9, 1, -1):
        p = 1.0 + (x / k) * p
    return jnp.where(x > -0.35, -(x * p), 1.0 - jnp.exp(x))


def _even_mix_kernel(z_ref, wpool_ref, pscale_ref, convw_ref, convb_ref, wa_ref, ba_ref,
                     wx_ref, bx_ref, lam_ref, o_ref, pbuf, cbuf, hstate, *, ts, d_pool, d_lru):
    sb = pl.program_id(1)
    group = d_pool // len(POOL_WINDOWS)
    blk = d_lru // LRU_BLOCKS

    @pl.when(sb == 0)
    def _():
        pbuf[0:HALO, :] = jnp.zeros((HALO, d_pool), F32)
        cbuf[0:HALO, :] = jnp.zeros((HALO, d_lru), F32)
        hstate[...] = jnp.zeros_like(hstate)

    up = z_ref[:, 0:d_pool]
    pbuf[HALO:HALO + ts, :] = up
    pos = sb * ts + lax.broadcasted_iota(I32, (ts, group), 0)
    ya = []
    for g, w in enumerate(POOL_WINDOWS):
        lo, hi = g * group, (g + 1) * group
        need = w - 1
        cur = pbuf[HALO - need:HALO + ts, lo:hi]
        span = 1
        while span < w:
            rows = cur.shape[0] - span
            cur = cur[span:span + rows, :] + cur[0:rows, :]
            span *= 2
        cnt = jnp.minimum(pos + 1, w).astype(F32)
        pooled = cur / cnt - up[:, lo:hi]
        y = jnp.dot(pooled.astype(BF16), wpool_ref[g], preferred_element_type=F32)
        ya.append(y * pscale_ref[0:1, lo:hi])
    o_ref[:, 0:d_pool] = jnp.concatenate(ya, axis=-1).astype(o_ref.dtype)
    pbuf[0:HALO, :] = pbuf[ts:ts + HALO, :]

    ul = z_ref[:, d_pool:d_pool + d_lru]
    ug = z_ref[:, d_pool + d_lru:d_pool + 2 * d_lru]
    cbuf[HALO:HALO + ts, :] = ul
    xc = jnp.broadcast_to(convb_ref[0:1, :], (ts, d_lru))
    for k in range(CONV_WIDTH):
        off = HALO - (CONV_WIDTH - 1) + k
        xc = xc + cbuf[off:off + ts, :] * convw_ref[k:k + 1, :]
    cbuf[0:HALO, :] = cbuf[ts:ts + HALO, :]

    xb = xc.astype(BF16)
    ra, ia = [], []
    for hb in range(LRU_BLOCKS):
        xs = xb[:, hb * blk:(hb + 1) * blk]
        ra.append(jnp.dot(xs, wa_ref[hb], preferred_element_type=F32))
        ia.append(jnp.dot(xs, wx_ref[hb], preferred_element_type=F32))
    r_gate = _sigmoid(jnp.concatenate(ra, axis=-1) + ba_ref[0:1, :])
    i_gate = _sigmoid(jnp.concatenate(ia, axis=-1) + bx_ref[0:1, :])
    log_a = LRU_C * r_gate * _log_sigmoid(lam_ref[0:1, :])
    a = jnp.exp(log_a)
    mult = jnp.sqrt(jnp.maximum(_neg_expm1(2.0 * log_a), 0.0))
    b = mult * i_gate * xc

    ridx = lax.broadcasted_iota(I32, (ts, d_lru), 0)
    d = 1
    while d < ts:
        keep = ridx >= d
        a_sh = jnp.where(keep, pltpu.roll(a, d, 0), 1.0)
        b_sh = jnp.where(keep, pltpu.roll(b, d, 0), 0.0)
        b = a * b_sh + b
        a = a * a_sh
        d *= 2
    h = b + a * hstate[0:1, :]
    hstate[...] = jnp.broadcast_to(h[ts - 1:ts, :], hstate.shape)
    o_ref[:, d_pool:d_pool + d_lru] = (h * _gelu(ug)).astype(o_ref.dtype)


def even_mix(z3d, w_pool, pool_scale, conv_w, conv_b, w_a, b_a, w_x, b_x, lam, *, ts=256):
    batch, seq, dz = z3d.shape
    d_pool = pool_scale.shape[-1]
    d_lru = lam.shape[-1]
    ts = min(ts, seq)
    group = d_pool // len(POOL_WINDOWS)
    blk = d_lru // LRU_BLOCKS
    kern = functools.partial(_even_mix_kernel, ts=ts, d_pool=d_pool, d_lru=d_lru)
    const2 = lambda b, s: (0, 0)
    const3 = lambda b, s: (0, 0, 0)
    return pl.pallas_call(
        kern,
        grid=(batch, seq // ts),
        in_specs=[
            pl.BlockSpec((None, ts, dz), lambda b, s: (b, s, 0)),
            _resident((len(POOL_WINDOWS), group, group), const3),
            _resident((1, d_pool), const2),
            _resident((CONV_WIDTH, d_lru), const2),
            _resident((1, d_lru), const2),
            _resident((LRU_BLOCKS, blk, blk), const3),
            _resident((1, d_lru), const2),
            _resident((LRU_BLOCKS, blk, blk), const3),
            _resident((1, d_lru), const2),
            _resident((1, d_lru), const2),
        ],
        out_specs=pl.BlockSpec((None, ts, d_pool + d_lru), lambda b, s: (b, s, 0)),
        out_shape=jax.ShapeDtypeStruct((batch, seq, d_pool + d_lru), BF16),
        scratch_shapes=[
            pltpu.VMEM((HALO + ts, d_pool), F32),
            pltpu.VMEM((HALO + ts, d_lru), F32),
            pltpu.VMEM((8, d_lru), F32),
        ],
        compiler_params=_params(("arbitrary", "arbitrary")),
        name="even_mix",
    )(z3d, w_pool, pool_scale, conv_w, conv_b, w_a, b_a, w_x, b_x, lam)


def _proj_ln_kernel(a_ref, w_ref, x_ref, mod_ref, g_ref, b_ref, o_ref, *, alpha, gate_row):
    y = jnp.dot(a_ref[...], w_ref[...], preferred_element_type=F32)
    gate = mod_ref[gate_row:gate_row + 1, :]
    r = alpha * x_ref[...] + (1.0 + gate) * y
    o_ref[...] = _layer_norm(r, g_ref[...], b_ref[...])


def proj_ln(a2d, w_bf16, x2d, mod_l, ln_g, ln_b, *, seq, alpha, gate_row, tm=512):
    n, d = x2d.shape
    k = a2d.shape[1]
    tm = min(tm, seq)
    blocks_per_batch = seq // tm
    kern = functools.partial(_proj_ln_kernel, alpha=alpha, gate_row=gate_row)
    return pl.pallas_call(
        kern,
        grid=(n // tm,),
        in_specs=[
            pl.BlockSpec((tm, k), lambda i: (i, 0)),
            _resident((k, d), lambda i: (0, 0)),
            pl.BlockSpec((tm, d), lambda i: (i, 0)),
            pl.BlockSpec((None, 6, d), lambda i: (i // blocks_per_batch, 0, 0)),
            _resident((1, d), lambda i: (0, 0)),
            _resident((1, d), lambda i: (0, 0)),
        ],
        out_specs=pl.BlockSpec((tm, d), lambda i: (i, 0)),
        out_shape=jax.ShapeDtypeStruct((n, d), F32),
        compiler_params=_params(("arbitrary",)),
        name="proj_ln",
    )(a2d, w_bf16, x2d, mod_l, ln_g, ln_b)


def _top16_rows(s, kidx, sentinel):
    t = s.shape[1]
    r16 = lax.broadcasted_iota(I32, (PEER_TOPK, t), 0)
    vals = jnp.zeros((PEER_TOPK, t), F32)
    idxs = jnp.zeros((PEER_TOPK, t), I32)
    for r in range(PEER_TOPK):
        m = jnp.max(s, axis=0, keepdims=True)
        am = jnp.min(jnp.where(s == m, kidx, sentinel), axis=0, keepdims=True)
        s = jnp.where(kidx == am, NEG_INF, s)
        vals = jnp.where(r16 == r, m, vals)
        idxs = jnp.where(r16 == r, am, idxs)
    return vals, idxs


def _take_rows16(table, sel):
    out = jnp.zeros(sel.shape, table.dtype)
    for s in range(PEER_TOPK):
        out = jnp.where(sel == s, table[s:s + 1, :], out)
    return out


SUBLANES = 8
BUILD_UNROLL = 16


def _peer_route_kernel(q_ref, keys_ref, g_ref,
                       gs_ref, sc_ref, val_ref, idx_ref, e_ref, w_ref, et_ref, wt_ref, *, tr):
    nhp = 2 * PEER_HEADS
    for hp in range(nhp):
        sc_ref[hp] = _dot_nt(keys_ref[hp], q_ref[:, hp * N_KEYS:(hp + 1) * N_KEYS])

    kidx = lax.broadcasted_iota(I32, (N_KEYS, tr), 0)

    def first_topk(hp, carry):
        vals, idxs = _top16_rows(sc_ref[hp], kidx, N_KEYS)
        val_ref[hp] = vals
        idx_ref[hp] = idxs
        return carry

    lax.fori_loop(0, nhp, first_topk, 0)

    i16 = lax.broadcasted_iota(I32, (16, tr), 0)
    i8 = lax.broadcasted_iota(I32, (8, tr), 0)
    flat = jnp.concatenate([
        i16 * 16, i8 * 16 + 1, i8 * 16 + 2, i8 * 16 + 3,
        i16, 16 + i8, 32 + i8,
    ], axis=0)
    valid = jnp.concatenate([
        i16 < 16, i8 < 8, i8 < 5, i8 < 4,
        i16 >= 4, i8 >= 4, i8 == 4,
    ], axis=0)

    def second_topk(hd, carry):
        s1 = val_ref[2 * hd]
        s2 = val_ref[2 * hd + 1]
        cand = jnp.concatenate([
            s1 + s2[0:1, :], s1[0:8, :] + s2[1:2, :], s1[0:8, :] + s2[2:3, :], s1[0:8, :] + s2[3:4, :],
            s1[0:1, :] + s2, s1[1:2, :] + s2[0:8, :], s1[2:3, :] + s2[0:8, :],
        ], axis=0)
        cand = jnp.where(valid, cand, NEG_INF)
        top, fsel = _top16_rows(cand, flat, 256)
        e = jnp.exp(top - top[0:1, :])
        gate = e / jnp.sum(e, axis=0, keepdims=True)
        a = _take_rows16(idx_ref[2 * hd], fsel >> 4)
        b = _take_rows16(idx_ref[2 * hd + 1], fsel & 15)
        row0 = pl.multiple_of(hd * PEER_TOPK, PEER_TOPK)
        e_ref[pl.ds(row0, PEER_TOPK), :] = a * N_KEYS + b
        w_ref[pl.ds(row0, PEER_TOPK), :] = gate
        return carry

    lax.fori_loop(0, PEER_HEADS, second_topk, 0)

    n_act = PEER_HEADS * PEER_TOPK
    for c in range(tr // n_act):
        et_ref[c * n_act:(c + 1) * n_act, :] = e_ref[:, c * n_act:(c + 1) * n_act].T
        wt_ref[c * n_act:(c + 1) * n_act, :] = w_ref[:, c * n_act:(c + 1) * n_act].T

    iota_rows = lax.broadcasted_iota(I32, (N_KEYS, n_act), 0)
    sub_iota = lax.broadcasted_iota(I32, (SUBLANES, N_KEYS), 0)

    def build(step, carry):
        base = pl.multiple_of(step * BUILD_UNROLL, BUILD_UNROLL)
        erows = et_ref[pl.ds(base, BUILD_UNROLL), :]
        wrows = wt_ref[pl.ds(base, BUILD_UNROLL), :]
        for u in range(BUILD_UNROLL):
            erow = erows[u:u + 1, :]
            wrow = wrows[u:u + 1, :]
            pt = jnp.where(iota_rows == (erow >> 7), wrow, 0.0).astype(BF16)
            qt = jnp.where(iota_rows == (erow & (N_KEYS - 1)), 1.0, 0.0).astype(BF16)
            gs_ref[u * N_KEYS:(u + 1) * N_KEYS, :] = _dot_nt(pt, qt)
        for gi in range(BUILD_UNROLL // SUBLANES):
            grp = step * (BUILD_UNROLL // SUBLANES) + gi
            for k in range(N_KEYS // SUBLANES):
                tiles = []
                for t in range(SUBLANES):
                    row0 = (gi * SUBLANES + t) * N_KEYS + k * SUBLANES
                    tiles.append(gs_ref[row0:row0 + SUBLANES, :])
                for dist in (4, 2, 1):
                    take_hi = (sub_iota & dist) != 0
                    for t in range(SUBLANES):
                        if t & dist:
                            continue
                        lo, hi = tiles[t], tiles[t + dist]
                        tiles[t] = jnp.where(take_hi, pltpu.roll(hi, dist, 0), lo)
                        tiles[t + dist] = jnp.where(take_hi, hi, pltpu.roll(lo, SUBLANES - dist, 0))
                for s in range(SUBLANES):
                    g_ref[grp, k * SUBLANES + s] = tiles[s]
        return carry

    lax.fori_loop(0, tr // BUILD_UNROLL, build, 0)


def peer_route(q2d, keys_bf16, *, seq, tr=256):
    n, d = q2d.shape
    tr = min(tr, seq)
    nhp = 2 * PEER_HEADS
    n_act = PEER_HEADS * PEER_TOPK
    kern = functools.partial(_peer_route_kernel, tr=tr)
    return pl.pallas_call(
        kern,
        grid=(n // tr,),
        in_specs=[
            pl.BlockSpec((tr, d), lambda i: (i, 0)),
            _resident(keys_bf16.shape, lambda i: (0, 0, 0)),
        ],
        out_specs=pl.BlockSpec((tr // SUBLANES, N_KEYS, SUBLANES, N_KEYS), lambda i: (i, 0, 0, 0)),
        out_shape=jax.ShapeDtypeStruct((n // SUBLANES, N_KEYS, SUBLANES, N_KEYS), F32),
        scratch_shapes=[
            pltpu.VMEM((BUILD_UNROLL * N_KEYS, N_KEYS), F32),
            pltpu.VMEM((nhp, N_KEYS, tr), F32),
            pltpu.VMEM((nhp, PEER_TOPK, tr), F32),
            pltpu.VMEM((nhp, PEER_TOPK, tr), I32),
            pltpu.VMEM((n_act, tr), I32),
            pltpu.VMEM((n_act, tr), F32),
            pltpu.VMEM((tr, n_act), I32),
            pltpu.VMEM((tr, n_act), F32),
        ],
        compiler_params=_params(("arbitrary",)),
        name="peer_route",
    )(q2d, keys_bf16)


DENSE_CHUNK = 256


def _peer_dense_kernel(h_ref, g_ref, ut_ref, v_ref, o_ref, act_ref):
    j = pl.program_id(1)
    rows = g_ref.shape[1]
    tm = h_ref.shape[0]

    @pl.when(j == 0)
    def _():
        o_ref[...] = jnp.zeros_like(o_ref)

    per = DENSE_CHUNK // N_KEYS
    for c in range(rows // per):
        ccols = slice(c * DENSE_CHUNK, (c + 1) * DENSE_CHUNK)
        z = jnp.dot(h_ref[...], ut_ref[:, ccols], preferred_element_type=F32)
        for r in range(per):
            row = c * per + r
            gate = g_ref[:, row, :, :].reshape(tm, N_KEYS)
            act_ref[:, row * N_KEYS:(row + 1) * N_KEYS] = (
                _gelu(z[:, r * N_KEYS:(r + 1) * N_KEYS]) * gate).astype(BF16)
    half = act_ref.shape[1] // 2
    o_ref[...] += jnp.dot(act_ref[:, :half], v_ref[:half, :], preferred_element_type=F32)
    o_ref[...] += jnp.dot(act_ref[:, half:], v_ref[half:, :], preferred_element_type=F32)


def peer_dense(h2d, g3d, ut_bf16, v_bf16, *, seq, tm=1024, te=1024):
    n, d = h2d.shape
    n_exp = v_bf16.shape[0]
    tm = min(tm, seq)
    return pl.pallas_call(
        _peer_dense_kernel,
        grid=(n // tm, n_exp // te),
        in_specs=[
            pl.BlockSpec((tm, d), lambda i, j: (i, 0), pipeline_mode=pl.Buffered(1)),
            pl.BlockSpec((tm // SUBLANES, te // N_KEYS, SUBLANES, N_KEYS), lambda i, j: (i, j, 0, 0)),
            pl.BlockSpec((d, te), lambda i, j: (0, j)),
            pl.BlockSpec((te, d), lambda i, j: (j, 0)),
        ],
        out_specs=pl.BlockSpec((tm, d), lambda i, j: (i, 0)),
        out_shape=jax.ShapeDtypeStruct((n, d), F32),
        scratch_shapes=[pltpu.VMEM((tm, te), BF16)],
        compiler_params=_params(("arbitrary", "arbitrary")),
        name="peer_dense",
    )(h2d, g3d, ut_bf16, v_bf16)


def _residual_ln_kernel(x_ref, y_ref, mod_ref, g_ref, b_ref, o_ref, *, alpha, gate_row):
    gate = mod_ref[gate_row:gate_row + 1, :]
    r = alpha * x_ref[...] + (1.0 + gate) * y_ref[...]
    o_ref[...] = _layer_norm(r, g_ref[...], b_ref[...])


def residual_ln(x2d, y2d, mod_l, ln_g, ln_b, *, seq, alpha, gate_row, tm=512):
    n, d = x2d.shape
    tm = min(tm, seq)
    blocks_per_batch = seq // tm
    kern = functools.partial(_residual_ln_kernel, alpha=alpha, gate_row=gate_row)
    return pl.pallas_call(
        kern,
        grid=(n // tm,),
        in_specs=[
            pl.BlockSpec((tm, d), lambda i: (i, 0)),
            pl.BlockSpec((tm, d), lambda i: (i, 0)),
            pl.BlockSpec((None, 6, d), lambda i: (i // blocks_per_batch, 0, 0)),
            _resident((1, d), lambda i: (0, 0)),
            _resident((1, d), lambda i: (0, 0)),
        ],
        out_specs=pl.BlockSpec((tm, d), lambda i: (i, 0)),
        out_shape=jax.ShapeDtypeStruct((n, d), F32),
        compiler_params=_params(("arbitrary",)),
        name="residual_ln",
    )(x2d, y2d, mod_l, ln_g, ln_b)


def kernel(x, c, ada_w, ada_b, ln_g, ln_b, peer_wq, peer_keys, peer_u, peer_v, ev_w_in, ev_w_pool,
           ev_pool_scale, ev_conv_w, ev_conv_b, ev_w_a, ev_b_a, ev_w_x, ev_b_x, ev_lam, ev_w_out,
           od_w_in, od_b_f, od_w_out):
    batch, seq, d = x.shape
    depth = ada_w.shape[0]
    n = batch * seq
    alpha = (2.0 * depth) ** 0.25
    d_mix = od_w_out.shape[1]

    c_pad = jnp.zeros((8, d), F32).at[:batch].set(c)
    mod = ada_mod(c_pad, ada_w, ada_b)[:, :batch].reshape(depth, batch, 6, d)

    xf = x.reshape(n, d)
    for l in range(depth):
        mod_l = mod[l]
        g0 = ln_g[l, 0].reshape(1, d)
        b0 = ln_b[l, 0].reshape(1, d)
        g1 = ln_g[l, 1].reshape(1, d)
        b1 = ln_b[l, 1].reshape(1, d)
        if l % 2 == 0:
            e = l // 2
            z = mod_matmul(xf, mod_l, ev_w_in[e].astype(BF16), seq=seq, shift_row=0, scale_row=1,
                           out_dtype=F32)
            mixed = even_mix(
                z.reshape(batch, seq, -1), ev_w_pool[e].astype(BF16), ev_pool_scale[e].reshape(1, -1),
                ev_conv_w[e], ev_conv_b[e].reshape(1, -1), ev_w_a[e].astype(BF16),
                ev_b_a[e].reshape(1, -1), ev_w_x[e].astype(BF16), ev_b_x[e].reshape(1, -1),
                ev_lam[e].reshape(1, -1))
            xf = proj_ln(mixed.reshape(n, -1), ev_w_out[e].astype(BF16), xf, mod_l, g0, b0,
                         seq=seq, alpha=alpha, gate_row=2)
        else:
            o = l // 2
            w_in = od_w_in[o]
            w_f = jnp.zeros((d, 128), F32).at[:, :N_HEADS_ATTN].set(w_in[:, 3 * d_mix:])
            wf_hi = w_f.astype(BF16)
            wf_lo = (w_f - wf_hi.astype(F32)).astype(BF16)
            b_f = jnp.zeros((1, 128), F32).at[0, :N_HEADS_ATTN].set(od_b_f[o])
            w_qv = jnp.concatenate([w_in[:, :d_mix], w_in[:, 2 * d_mix:3 * d_mix]], axis=1).astype(BF16)
            w_kt = w_in[:, d_mix:2 * d_mix].T.astype(BF16)
            qv, k_t, f_cum = odd_in(xf, mod_l, w_qv, w_kt, wf_hi, wf_lo, b_f, seq=seq)
            f_keys = f_cum[:, :N_HEADS_ATTN].reshape(batch, seq, N_HEADS_ATTN)
            f_keys = jnp.transpose(f_keys, (0, 2, 1)).reshape(batch * N_HEADS_ATTN, 1, seq)
            attn = attention(qv, k_t, f_keys, batch=batch, seq=seq)
            xf = proj_ln(attn, od_w_out[o].astype(BF16), xf, mod_l, g0, b0,
                         seq=seq, alpha=alpha, gate_row=2)

        keys = peer_keys[l].reshape(2 * PEER_HEADS, N_KEYS, -1).astype(BF16)
        q, h2 = mod_matmul(xf, mod_l, peer_wq[l].astype(BF16), seq=seq, shift_row=3, scale_row=4,
                           out_dtype=BF16, emit_h=True)
        gmat = peer_route(q, keys, seq=seq)
        y = peer_dense(h2, gmat, peer_u[l].T.astype(BF16), peer_v[l].astype(BF16), seq=seq)
        xf = residual_ln(xf, y, mod_l, g1, b1, seq=seq, alpha=alpha, gate_row=5)
    return xf.reshape(batch, seq, d)
```

```python
import functools
import math

import jax
import jax.numpy as jnp
from jax import lax
from jax.experimental import pallas as pl
from jax.experimental.pallas import tpu as pltpu

F32 = jnp.float32
BF16 = jnp.bfloat16
I32 = jnp.int32

LN_EPS = 1e-5
POOL_WINDOWS = (2, 4, 8, 16)
CONV_WIDTH = 4
LRU_BLOCKS = 8
LRU_C = 8.0
N_HEADS_ATTN = 16
PEER_HEADS = 8
PEER_TOPK = 16
N_KEYS = 128

V7X_VMEM_BYTES = 64 * 1024 * 1024
VMEM_LIMIT = 56 * 1024 * 1024
NEG_INF = float("-inf")
LOG2E = math.log2(math.e)


def _params(sem):
    return pltpu.CompilerParams(dimension_semantics=sem, vmem_limit_bytes=VMEM_LIMIT)


def _resident(shape, index_map):
    return pl.BlockSpec(shape, index_map, pipeline_mode=pl.Buffered(1))


def _gelu(x):
    c = math.sqrt(2.0 / math.pi)
    return 0.5 * x * (1.0 + jnp.tanh(c * (x + 0.044715 * (x * x * x))))


def _log_sigmoid(x):
    return jnp.minimum(x, 0.0) - jnp.log1p(jnp.exp(-jnp.abs(x)))


def _sigmoid(x):
    return 1.0 / (1.0 + jnp.exp(-x))


def _layer_norm(r, g, b):
    mu = jnp.mean(r, axis=-1, keepdims=True)
    d = r - mu
    var = jnp.mean(d * d, axis=-1, keepdims=True)
    return d * lax.rsqrt(var + LN_EPS) * g + b


def _dot_nt(a, b):
    return lax.dot_general(a, b, (((1,), (1,)), ((), ())), preferred_element_type=F32)


def _ada_kernel(c_ref, w_ref, b_ref, o_ref):
    c = c_ref[...]
    ca = c * _sigmoid(c)
    o_ref[...] = jnp.dot(ca, w_ref[...], preferred_element_type=F32) + b_ref[...]


def ada_mod(c_pad, ada_w, ada_b):
    depth, d, n6 = ada_w.shape
    rows = c_pad.shape[0]
    tn = 1024
    return pl.pallas_call(
        _ada_kernel,
        grid=(depth, n6 // tn),
        in_specs=[
            pl.BlockSpec((rows, d), lambda l, j: (0, 0)),
            pl.BlockSpec((None, d, tn), lambda l, j: (l, 0, j)),
            pl.BlockSpec((None, 1, tn), lambda l, j: (l, 0, j)),
        ],
        out_specs=pl.BlockSpec((None, rows, tn), lambda l, j: (l, 0, j)),
        out_shape=jax.ShapeDtypeStruct((depth, rows, n6), F32),
        compiler_params=_params(("arbitrary", "arbitrary")),
        name="ada_mod",
    )(c_pad, ada_w, ada_b.reshape(depth, 1, n6))


def _mod_matmul_kernel(x_ref, mod_ref, w_ref, o_ref, h_ref, *, shift_row, scale_row):
    @pl.when(pl.program_id(1) == 0)
    def _():
        sh = mod_ref[shift_row:shift_row + 1, :]
        sc = mod_ref[scale_row:scale_row + 1, :]
        h_ref[...] = (x_ref[...] * (1.0 + sc) + sh).astype(BF16)

    o_ref[...] = jnp.dot(h_ref[...], w_ref[...], preferred_element_type=F32).astype(o_ref.dtype)


def mod_matmul(x2d, mod_l, w_bf16, *, seq, shift_row, scale_row, out_dtype, emit_h=False,
               tm=1024, tn=1024):
    n, d = x2d.shape
    nout = w_bf16.shape[1]
    tm = min(tm, seq)
    blocks_per_batch = seq // tm
    kern = functools.partial(_mod_matmul_kernel, shift_row=shift_row, scale_row=scale_row)
    out_specs = [pl.BlockSpec((tm, tn), lambda i, j: (i, j))]
    out_shape = [jax.ShapeDtypeStruct((n, nout), out_dtype)]
    scratch = []
    if emit_h:
        out_specs.append(pl.BlockSpec((tm, d), lambda i, j: (i, 0)))
        out_shape.append(jax.ShapeDtypeStruct((n, d), BF16))
    else:
        scratch.append(pltpu.VMEM((tm, d), BF16))
    res = pl.pallas_call(
        kern,
        grid=(n // tm, nout // tn),
        in_specs=[
            pl.BlockSpec((tm, d), lambda i, j: (i, 0)),
            pl.BlockSpec((None, 6, d), lambda i, j: (i // blocks_per_batch, 0, 0)),
            pl.BlockSpec((d, tn), lambda i, j: (0, j)),
        ],
        out_specs=out_specs,
        out_shape=out_shape,
        scratch_shapes=scratch,
        compiler_params=_params(("arbitrary", "arbitrary")),
        name="mod_matmul",
    )(x2d, mod_l, w_bf16)
    return res if emit_h else res[0]


def _cumsum_rows(x):
    rows = x.shape[0]
    ridx = lax.broadcasted_iota(I32, x.shape, 0)
    d = 1
    while d < rows:
        x = x + jnp.where(ridx >= d, pltpu.roll(x, d, 0), 0.0)
        d *= 2
    return x


def _odd_in_kernel(x_ref, mod_ref, w_ref, wkt_ref, wfh_ref, wfl_ref, bf_ref, o_ref, kt_ref, f_ref,
                   h_ref, carry_ref, *, blocks_per_batch, nb, q_scale):
    i = pl.program_id(0)
    j = pl.program_id(1)

    @pl.when(j == 0)
    def _():
        sh = mod_ref[0:1, :]
        sc = mod_ref[1:2, :]
        h = x_ref[...] * (1.0 + sc) + sh
        h_hi = h.astype(BF16)
        h_ref[...] = h_hi
        h_lo = (h - h_hi.astype(F32)).astype(BF16)
        zf = (jnp.dot(h_hi, wfh_ref[...], preferred_element_type=F32)
              + jnp.dot(h_lo, wfh_ref[...], preferred_element_type=F32)
              + jnp.dot(h_hi, wfl_ref[...], preferred_element_type=F32))
        logf = _log_sigmoid(zf + bf_ref[...])

        @pl.when(i % blocks_per_batch == 0)
        def _():
            carry_ref[...] = jnp.zeros_like(carry_ref)

        cs = _cumsum_rows(logf) + carry_ref[0:1, :]
        f_ref[...] = cs
        carry_ref[...] = jnp.broadcast_to(cs[cs.shape[0] - 1:, :], carry_ref.shape)

    is_key = jnp.logical_and(j >= nb, j < 2 * nb)

    @pl.when(jnp.logical_not(is_key))
    def _():
        z = jnp.dot(h_ref[...], w_ref[...], preferred_element_type=F32)
        scale = jnp.where(j < nb, q_scale, 1.0)
        o_ref[...] = (z * scale).astype(o_ref.dtype)

    @pl.when(is_key)
    def _():
        kt_ref[...] = _dot_nt(wkt_ref[...], h_ref[...]).astype(kt_ref.dtype)


def odd_in(x2d, mod_l, w_qv, w_kt, wf_hi, wf_lo, b_f_pad, *, seq, tm=1024, tn=1024):
    n, d = x2d.shape
    d_mix = w_kt.shape[0]
    tm = min(tm, seq)
    blocks_per_batch = seq // tm
    head_dim = d_mix // N_HEADS_ATTN
    nb = d_mix // tn
    kern = functools.partial(_odd_in_kernel, blocks_per_batch=blocks_per_batch, nb=nb,
                             q_scale=head_dim ** -0.5 * LOG2E)

    def qv_block(j):
        return jnp.where(j < nb, j, jnp.where(j < 2 * nb, nb - 1, j - nb))

    def k_block(j):
        return jnp.clip(j - nb, 0, nb - 1)

    return pl.pallas_call(
        kern,
        grid=(n // tm, 3 * nb),
        in_specs=[
            pl.BlockSpec((tm, d), lambda i, j: (i, 0)),
            pl.BlockSpec((None, 6, d), lambda i, j: (i // blocks_per_batch, 0, 0)),
            pl.BlockSpec((d, tn), lambda i, j: (0, qv_block(j))),
            pl.BlockSpec((tn, d), lambda i, j: (k_block(j), 0)),
            _resident((d, 128), lambda i, j: (0, 0)),
            _resident((d, 128), lambda i, j: (0, 0)),
            _resident((1, 128), lambda i, j: (0, 0)),
        ],
        out_specs=[
            pl.BlockSpec((tm, tn), lambda i, j: (i, qv_block(j))),
            pl.BlockSpec((tn, tm), lambda i, j: (k_block(j), i)),
            pl.BlockSpec((tm, 128), lambda i, j: (i, 0)),
        ],
        out_shape=[
            jax.ShapeDtypeStruct((n, 2 * d_mix), BF16),
            jax.ShapeDtypeStruct((d_mix, n), BF16),
            jax.ShapeDtypeStruct((n, 128), F32),
        ],
        scratch_shapes=[pltpu.VMEM((tm, d), BF16), pltpu.VMEM((8, 128), F32)],
        compiler_params=_params(("arbitrary", "arbitrary")),
        name="odd_in",
    )(x2d, mod_l, w_qv, w_kt, wf_hi, wf_lo, b_f_pad)


def _attn_kernel(q_ref, kt_ref, v_ref, fk_ref, o_ref, sa_ref, sb_ref, vaug_ref, *, tq, tk):
    qi = pl.program_id(2)
    f_ref0 = fk_ref[0:1, pl.ds(pl.multiple_of(qi * tq, tq), tk)][:, 0:1]
    dh = q_ref.shape[1]

    @pl.when(qi == 0)
    def _():
        vaug_ref[:, 0:dh] = v_ref[...]
        vaug_ref[:, dh:2 * dh] = jnp.ones((v_ref.shape[0], dh), BF16)
    per_q = tq // tk
    n_full = qi * per_q

    def scores(c, s_ref):
        start = pl.multiple_of(c * tk, tk)
        s = jnp.dot(q_ref[...], kt_ref[:, pl.ds(start, tk)], preferred_element_type=F32)
        s_ref[...] = s + (f_ref0 - fk_ref[0:1, pl.ds(start, tk)]) * LOG2E

    def update(c, s_ref, carry, diag_offset=None):
        m, l, acc = carry
        start = pl.multiple_of(c * tk, tk)
        if diag_offset is not None:
            row = lax.broadcasted_iota(I32, (tq, tk), 0)
            col = lax.broadcasted_iota(I32, (tq, tk), 1) + diag_offset
            s_ref[...] = jnp.where(col <= row, s_ref[...], NEG_INF)
        m_new = jnp.maximum(m, jnp.max(s_ref[...], axis=-1, keepdims=True))
        alpha = jnp.exp2(m - m_new)
        p = jnp.exp2(s_ref[...] - m_new)
        v = vaug_ref[pl.ds(start, tk), :]
        acc = alpha * acc + jnp.dot(p.astype(BF16), v, preferred_element_type=F32)
        return m_new, l, acc

    scores(0, sa_ref)

    def pair(p, carry):
        scores(2 * p + 1, sb_ref)
        carry = update(2 * p, sa_ref, carry)
        scores(2 * p + 2, sa_ref)
        return update(2 * p + 1, sb_ref, carry)

    init = (jnp.full((tq, 1), NEG_INF, F32), jnp.zeros((tq, 1), F32), jnp.zeros((tq, 2 * dh), F32))
    carry = lax.fori_loop(0, n_full // 2, pair, init)
    def diagonal(carry, bufs):
        for dchunk in range(per_q):
            if dchunk + 1 < per_q:
                scores(n_full + dchunk + 1, bufs[(dchunk + 1) % 2])
            carry = update(n_full + dchunk, bufs[dchunk % 2], carry, diag_offset=dchunk * tk)
        return carry

    def even_tail(carry):
        return diagonal(carry, (sa_ref, sb_ref))

    def odd_tail(carry):
        scores(n_full, sb_ref)
        carry = update(n_full - 1, sa_ref, carry)
        return diagonal(carry, (sb_ref, sa_ref))

    if per_q % 2 == 0:
        m, l, acc = even_tail(carry)
    else:
        m, l, acc = lax.cond(n_full % 2 == 1, odd_tail, even_tail, carry)
    o_ref[...] = (acc[:, 0:dh] / acc[:, dh:dh + 1]).astype(o_ref.dtype)


def attention(qv, k_t, f_keys, *, batch, seq, tq=512, tk=512):
    n = qv.shape[0]
    d_mix = qv.shape[1] // 2
    heads = N_HEADS_ATTN
    dh = d_mix // heads
    tq = min(tq, seq)
    tk = min(tk, tq)
    nq = seq // tq
    kern = functools.partial(_attn_kernel, tq=tq, tk=tk)
    return pl.pallas_call(
        kern,
        grid=(batch, heads, nq),
        in_specs=[
            pl.BlockSpec((tq, dh), lambda b, h, i: (b * nq + i, h)),
            pl.BlockSpec((dh, seq), lambda b, h, i: (h, b)),
            pl.BlockSpec((seq, dh), lambda b, h, i: (b, heads + h)),
            pl.BlockSpec((None, 1, seq), lambda b, h, i: (b * heads + h, 0, 0)),
        ],
        out_specs=pl.BlockSpec((tq, dh), lambda b, h, i: (b * nq + i, h)),
        out_shape=jax.ShapeDtypeStruct((n, d_mix), BF16),
        scratch_shapes=[pltpu.VMEM((tq, tk), F32), pltpu.VMEM((tq, tk), F32),
                        pltpu.VMEM((seq, 2 * dh), BF16)],
        compiler_params=_params(("arbitrary", "arbitrary", "arbitrary")),
        name="attention",
    )(qv, k_t, qv, f_keys)


HALO = 16


def _neg_expm1(x):
    p = 1.0 + x / 10.0
    for k in range(9, 1, -1):
        p = 1.0 + (x / k) * p
    return jnp.where(x > -0.35, -(x * p), 1.0 - jnp.exp(x))


def _even_mix_kernel(z_ref, wpool_ref, pscale_ref, convw_ref, convb_ref, wa_ref, ba_ref,
                     wx_ref, bx_ref, lam_ref, o_ref, pbuf, cbuf, hstate, *, ts, d_pool, d_lru):
    sb = pl.program_id(1)
    group = d_pool // len(POOL_WINDOWS)
    blk = d_lru // LRU_BLOCKS

    @pl.when(sb == 0)
    def _():
        pbuf[0:HALO, :] = jnp.zeros((HALO, d_pool), F32)
        cbuf[0:HALO, :] = jnp.zeros((HALO, d_lru), F32)
        hstate[...] = jnp.zeros_like(hstate)

    up = z_ref[:, 0:d_pool]
    pbuf[HALO:HALO + ts, :] = up
    pos = sb * ts + lax.broadcasted_iota(I32, (ts, group), 0)
    ya = []
    for g, w in enumerate(POOL_WINDOWS):
        lo, hi = g * group, (g + 1) * group
        need = w - 1
        cur = pbuf[HALO - need:HALO + ts, lo:hi]
        span = 1
        while span < w:
            rows = cur.shape[0] - span
            cur = cur[span:span + rows, :] + cur[0:rows, :]
            span *= 2
        cnt = jnp.minimum(pos + 1, w).astype(F32)
        pooled = cur / cnt - up[:, lo:hi]
        y = jnp.dot(pooled.astype(BF16), wpool_ref[g], preferred_element_type=F32)
        ya.append(y * pscale_ref[0:1, lo:hi])
    o_ref[:, 0:d_pool] = jnp.concatenate(ya, axis=-1).astype(o_ref.dtype)
    pbuf[0:HALO, :] = pbuf[ts:ts + HALO, :]

    ul = z_ref[:, d_pool:d_pool + d_lru]
    ug = z_ref[:, d_pool + d_lru:d_pool + 2 * d_lru]
    cbuf[HALO:HALO + ts, :] = ul
    xc = jnp.broadcast_to(convb_ref[0:1, :], (ts, d_lru))
    for k in range(CONV_WIDTH):
        off = HALO - (CONV_WIDTH - 1) + k
        xc = xc + cbuf[off:off + ts, :] * convw_ref[k:k + 1, :]
    cbuf[0:HALO, :] = cbuf[ts:ts + HALO, :]

    xb = xc.astype(BF16)
    ra, ia = [], []
    for hb in range(LRU_BLOCKS):
        xs = xb[:, hb * blk:(hb + 1) * blk]
        ra.append(jnp.dot(xs, wa_ref[hb], preferred_element_type=F32))
        ia.append(jnp.dot(xs, wx_ref[hb], preferred_element_type=F32))
    r_gate = _sigmoid(jnp.concatenate(ra, axis=-1) + ba_ref[0:1, :])
    i_gate = _sigmoid(jnp.concatenate(ia, axis=-1) + bx_ref[0:1, :])
    log_a = LRU_C * r_gate * _log_sigmoid(lam_ref[0:1, :])
    a = jnp.exp(log_a)
    mult = jnp.sqrt(jnp.maximum(_neg_expm1(2.0 * log_a), 0.0))
    b = mult * i_gate * xc

    ridx = lax.broadcasted_iota(I32, (ts, d_lru), 0)
    d = 1
    while d < ts:
        keep = ridx >= d
        a_sh = jnp.where(keep, pltpu.roll(a, d, 0), 1.0)
        b_sh = jnp.where(keep, pltpu.roll(b, d, 0), 0.0)
        b = a * b_sh + b
        a = a * a_sh
        d *= 2
    h = b + a * hstate[0:1, :]
    hstate[...] = jnp.broadcast_to(h[ts - 1:ts, :], hstate.shape)
    o_ref[:, d_pool:d_pool + d_lru] = (h * _gelu(ug)).astype(o_ref.dtype)


def even_mix(z3d, w_pool, pool_scale, conv_w, conv_b, w_a, b_a, w_x, b_x, lam, *, ts=256):
    batch, seq, dz = z3d.shape
    d_pool = pool_scale.shape[-1]
    d_lru = lam.shape[-1]
    ts = min(ts, seq)
    group = d_pool // len(POOL_WINDOWS)
    blk = d_lru // LRU_BLOCKS
    kern = functools.partial(_even_mix_kernel, ts=ts, d_pool=d_pool, d_lru=d_lru)
    const2 = lambda b, s: (0, 0)
    const3 = lambda b, s: (0, 0, 0)
    return pl.pallas_call(
        kern,
        grid=(batch, seq // ts),
        in_specs=[
            pl.BlockSpec((None, ts, dz), lambda b, s: (b, s, 0)),
            _resident((len(POOL_WINDOWS), group, group), const3),
            _resident((1, d_pool), const2),
            _resident((CONV_WIDTH, d_lru), const2),
            _resident((1, d_lru), const2),
            _resident((LRU_BLOCKS, blk, blk), const3),
            _resident((1, d_lru), const2),
            _resident((LRU_BLOCKS, blk, blk), const3),
            _resident((1, d_lru), const2),
            _resident((1, d_lru), const2),
        ],
        out_specs=pl.BlockSpec((None, ts, d_pool + d_lru), lambda b, s: (b, s, 0)),
        out_shape=jax.ShapeDtypeStruct((batch, seq, d_pool + d_lru), BF16),
        scratch_shapes=[
            pltpu.VMEM((HALO + ts, d_pool), F32),
            pltpu.VMEM((HALO + ts, d_lru), F32),
            pltpu.VMEM((8, d_lru), F32),
        ],
        compiler_params=_params(("arbitrary", "arbitrary")),
        name="even_mix",
    )(z3d, w_pool, pool_scale, conv_w, conv_b, w_a, b_a, w_x, b_x, lam)


def _proj_ln_kernel(a_ref, w_ref, x_ref, mod_ref, g_ref, b_ref, o_ref, *, alpha, gate_row):
    y = jnp.dot(a_ref[...], w_ref[...], preferred_element_type=F32)
    gate = mod_ref[gate_row:gate_row + 1, :]
    r = alpha * x_ref[...] + (1.0 + gate) * y
    o_ref[...] = _layer_norm(r, g_ref[...], b_ref[...])


def proj_ln(a2d, w_bf16, x2d, mod_l, ln_g, ln_b, *, seq, alpha, gate_row, tm=512):
    n, d = x2d.shape
    k = a2d.shape[1]
    tm = min(tm, seq)
    blocks_per_batch = seq // tm
    kern = functools.partial(_proj_ln_kernel, alpha=alpha, gate_row=gate_row)
    return pl.pallas_call(
        kern,
        grid=(n // tm,),
        in_specs=[
            pl.BlockSpec((tm, k), lambda i: (i, 0)),
            _resident((k, d), lambda i: (0, 0)),
            pl.BlockSpec((tm, d), lambda i: (i, 0)),
            pl.BlockSpec((None, 6, d), lambda i: (i // blocks_per_batch, 0, 0)),
            _resident((1, d), lambda i: (0, 0)),
            _resident((1, d), lambda i: (0, 0)),
        ],
        out_specs=pl.BlockSpec((tm, d), lambda i: (i, 0)),
        out_shape=jax.ShapeDtypeStruct((n, d), F32),
        compiler_params=_params(("arbitrary",)),
        name="proj_ln",
    )(a2d, w_bf16, x2d, mod_l, ln_g, ln_b)


def _top16_rows(s, kidx, sentinel):
    t = s.shape[1]
    r16 = lax.broadcasted_iota(I32, (PEER_TOPK, t), 0)
    vals = jnp.zeros((PEER_TOPK, t), F32)
    idxs = jnp.zeros((PEER_TOPK, t), I32)
    for r in range(PEER_TOPK):
        m = jnp.max(s, axis=0, keepdims=True)
        am = jnp.min(jnp.where(s == m, kidx, sentinel), axis=0, keepdims=True)
        s = jnp.where(kidx == am, NEG_INF, s)
        vals = jnp.where(r16 == r, m, vals)
        idxs = jnp.where(r16 == r, am, idxs)
    return vals, idxs


def _take_rows16(table, sel):
    out = jnp.zeros(sel.shape, table.dtype)
    for s in range(PEER_TOPK):
        out = jnp.where(sel == s, table[s:s + 1, :], out)
    return out


SUBLANES = 8
BUILD_UNROLL = 32


def _peer_route_kernel(q_ref, keys_ref, g_ref,
                       gs_ref, sc_ref, val_ref, idx_ref, e_ref, w_ref, et_ref, wt_ref, *, tr):
    nhp = 2 * PEER_HEADS
    for hp in range(nhp):
        sc_ref[hp] = _dot_nt(keys_ref[hp], q_ref[:, hp * N_KEYS:(hp + 1) * N_KEYS])

    kidx = lax.broadcasted_iota(I32, (N_KEYS, tr), 0)

    def first_topk(hp, carry):
        vals, idxs = _top16_rows(sc_ref[hp], kidx, N_KEYS)
        val_ref[hp] = vals
        idx_ref[hp] = idxs
        return carry

    lax.fori_loop(0, nhp, first_topk, 0)

    i16 = lax.broadcasted_iota(I32, (16, tr), 0)
    i8 = lax.broadcasted_iota(I32, (8, tr), 0)
    flat = jnp.concatenate([
        i16 * 16, i8 * 16 + 1, i8 * 16 + 2, i8 * 16 + 3,
        i16, 16 + i8, 32 + i8,
    ], axis=0)
    valid = jnp.concatenate([
        i16 < 16, i8 < 8, i8 < 5, i8 < 4,
        i16 >= 4, i8 >= 4, i8 == 4,
    ], axis=0)

    def second_topk(hd, carry):
        s1 = val_ref[2 * hd]
        s2 = val_ref[2 * hd + 1]
        cand = jnp.concatenate([
            s1 + s2[0:1, :], s1[0:8, :] + s2[1:2, :], s1[0:8, :] + s2[2:3, :], s1[0:8, :] + s2[3:4, :],
            s1[0:1, :] + s2, s1[1:2, :] + s2[0:8, :], s1[2:3, :] + s2[0:8, :],
        ], axis=0)
        cand = jnp.where(valid, cand, NEG_INF)
        top, fsel = _top16_rows(cand, flat, 256)
        e = jnp.exp(top - top[0:1, :])
        gate = e / jnp.sum(e, axis=0, keepdims=True)
        a = _take_rows16(idx_ref[2 * hd], fsel >> 4)
        b = _take_rows16(idx_ref[2 * hd + 1], fsel & 15)
        row0 = pl.multiple_of(hd * PEER_TOPK, PEER_TOPK)
        e_ref[pl.ds(row0, PEER_TOPK), :] = a * N_KEYS + b
        w_ref[pl.ds(row0, PEER_TOPK), :] = gate
        return carry

    lax.fori_loop(0, PEER_HEADS, second_topk, 0)

    n_act = PEER_HEADS * PEER_TOPK
    for c in range(tr // n_act):
        et_ref[c * n_act:(c + 1) * n_act, :] = e_ref[:, c * n_act:(c + 1) * n_act].T
        wt_ref[c * n_act:(c + 1) * n_act, :] = w_ref[:, c * n_act:(c + 1) * n_act].T

    iota_rows = lax.broadcasted_iota(I32, (N_KEYS, n_act), 0)
    sub_iota = lax.broadcasted_iota(I32, (SUBLANES, N_KEYS), 0)

    def build(step, carry):
        base = pl.multiple_of(step * BUILD_UNROLL, BUILD_UNROLL)
        erows = et_ref[pl.ds(base, BUILD_UNROLL), :]
        wrows = wt_ref[pl.ds(base, BUILD_UNROLL), :]
        for u in range(BUILD_UNROLL):
            erow = erows[u:u + 1, :]
            wrow = wrows[u:u + 1, :]
            pt = jnp.where(iota_rows == (erow >> 7), wrow, 0.0).astype(BF16)
            qt = jnp.where(iota_rows == (erow & (N_KEYS - 1)), 1.0, 0.0).astype(BF16)
            gs_ref[u * N_KEYS:(u + 1) * N_KEYS, :] = _dot_nt(pt, qt)
        for gi in range(BUILD_UNROLL // SUBLANES):
            grp = step * (BUILD_UNROLL // SUBLANES) + gi
            for k in range(N_KEYS // SUBLANES):
                tiles = []
                for t in range(SUBLANES):
                    row0 = (gi * SUBLANES + t) * N_KEYS + k * SUBLANES
                    tiles.append(gs_ref[row0:row0 + SUBLANES, :])
                for dist in (4, 2, 1):
                    take_hi = (sub_iota & dist) != 0
                    for t in range(SUBLANES):
                        if t & dist:
                            continue
                        lo, hi = tiles[t], tiles[t + dist]
                        tiles[t] = jnp.where(take_hi, pltpu.roll(hi, dist, 0), lo)
                        tiles[t + dist] = jnp.where(take_hi, hi, pltpu.roll(lo, SUBLANES - dist, 0))
                for s in range(SUBLANES):
                    g_ref[grp, k * SUBLANES + s] = tiles[s]
        return carry

    lax.fori_loop(0, tr // BUILD_UNROLL, build, 0)


def peer_route(q2d, keys_bf16, *, seq, tr=256):
    n, d = q2d.shape
    tr = min(tr, seq)
    nhp = 2 * PEER_HEADS
    n_act = PEER_HEADS * PEER_TOPK
    kern = functools.partial(_peer_route_kernel, tr=tr)
    return pl.pallas_call(
        kern,
        grid=(n // tr,),
        in_specs=[
            pl.BlockSpec((tr, d), lambda i: (i, 0)),
            _resident(keys_bf16.shape, lambda i: (0, 0, 0)),
        ],
        out_specs=pl.BlockSpec((tr // SUBLANES, N_KEYS, SUBLANES, N_KEYS), lambda i: (i, 0, 0, 0)),
        out_shape=jax.ShapeDtypeStruct((n // SUBLANES, N_KEYS, SUBLANES, N_KEYS), F32),
        scratch_shapes=[
            pltpu.VMEM((BUILD_UNROLL * N_KEYS, N_KEYS), F32),
            pltpu.VMEM((nhp, N_KEYS, tr), F32),
            pltpu.VMEM((nhp, PEER_TOPK, tr), F32),
            pltpu.VMEM((nhp, PEER_TOPK, tr), I32),
            pltpu.VMEM((n_act, tr), I32),
            pltpu.VMEM((n_act, tr), F32),
            pltpu.VMEM((tr, n_act), I32),
            pltpu.VMEM((tr, n_act), F32),
        ],
        compiler_params=_params(("arbitrary",)),
        name="peer_route",
    )(q2d, keys_bf16)


DENSE_CHUNK = 256


def _peer_dense_kernel(h_ref, g_ref, u_ref, v_ref, o_ref, act_ref):
    j = pl.program_id(1)
    rows = g_ref.shape[1]
    tm = h_ref.shape[0]

    @pl.when(j == 0)
    def _():
        o_ref[...] = jnp.zeros_like(o_ref)

    per = DENSE_CHUNK // N_KEYS
    for c in range(rows // per):
        ccols = slice(c * DENSE_CHUNK, (c + 1) * DENSE_CHUNK)
        z = _dot_nt(h_ref[...], u_ref[ccols, :])
        for r in range(per):
            row = c * per + r
            gate = g_ref[:, row, :, :].reshape(tm, N_KEYS)
            act_ref[:, row * N_KEYS:(row + 1) * N_KEYS] = (
                _gelu(z[:, r * N_KEYS:(r + 1) * N_KEYS]) * gate).astype(BF16)
    half = act_ref.shape[1] // 2
    o_ref[...] += jnp.dot(act_ref[:, :half], v_ref[:half, :], preferred_element_type=F32)
    o_ref[...] += jnp.dot(act_ref[:, half:], v_ref[half:, :], preferred_element_type=F32)


def peer_dense(h2d, g3d, u_bf16, v_bf16, *, seq, tm=1024, te=1024):
    n, d = h2d.shape
    n_exp = v_bf16.shape[0]
    tm = min(tm, seq)
    return pl.pallas_call(
        _peer_dense_kernel,
        grid=(n // tm, n_exp // te),
        in_specs=[
            pl.BlockSpec((tm, d), lambda i, j: (i, 0), pipeline_mode=pl.Buffered(1)),
            pl.BlockSpec((tm // SUBLANES, te // N_KEYS, SUBLANES, N_KEYS), lambda i, j: (i, j, 0, 0)),
            pl.BlockSpec((te, d), lambda i, j: (j, 0)),
            pl.BlockSpec((te, d), lambda i, j: (j, 0)),
        ],
        out_specs=pl.BlockSpec((tm, d), lambda i, j: (i, 0)),
        out_shape=jax.ShapeDtypeStruct((n, d), F32),
        scratch_shapes=[pltpu.VMEM((tm, te), BF16)],
        compiler_params=_params(("arbitrary", "arbitrary")),
        name="peer_dense",
    )(h2d, g3d, u_bf16, v_bf16)


def _residual_ln_kernel(x_ref, y_ref, mod_ref, g_ref, b_ref, o_ref, *, alpha, gate_row):
    gate = mod_ref[gate_row:gate_row + 1, :]
    r = alpha * x_ref[...] + (1.0 + gate) * y_ref[...]
    o_ref[...] = _layer_norm(r, g_ref[...], b_ref[...])


def residual_ln(x2d, y2d, mod_l, ln_g, ln_b, *, seq, alpha, gate_row, tm=512):
    n, d = x2d.shape
    tm = min(tm, seq)
    blocks_per_batch = seq // tm
    kern = functools.partial(_residual_ln_kernel, alpha=alpha, gate_row=gate_row)
    return pl.pallas_call(
        kern,
        grid=(n // tm,),
        in_specs=[
            pl.BlockSpec((tm, d), lambda i: (i, 0)),
            pl.BlockSpec((tm, d), lambda i: (i, 0)),
            pl.BlockSpec((None, 6, d), lambda i: (i // blocks_per_batch, 0, 0)),
            _resident((1, d), lambda i: (0, 0)),
            _resident((1, d), lambda i: (0, 0)),
        ],
        out_specs=pl.BlockSpec((tm, d), lambda i: (i, 0)),
        out_shape=jax.ShapeDtypeStruct((n, d), F32),
        compiler_params=_params(("arbitrary",)),
        name="residual_ln",
    )(x2d, y2d, mod_l, ln_g, ln_b)


def kernel(x, c, ada_w, ada_b, ln_g, ln_b, peer_wq, peer_keys, peer_u, peer_v, ev_w_in, ev_w_pool,
           ev_pool_scale, ev_conv_w, ev_conv_b, ev_w_a, ev_b_a, ev_w_x, ev_b_x, ev_lam, ev_w_out,
           od_w_in, od_b_f, od_w_out):
    batch, seq, d = x.shape
    depth = ada_w.shape[0]
    n = batch * seq
    alpha = (2.0 * depth) ** 0.25
    d_mix = od_w_out.shape[1]

    c_pad = jnp.zeros((8, d), F32).at[:batch].set(c)
    mod = ada_mod(c_pad, ada_w, ada_b)[:, :batch].reshape(depth, batch, 6, d)

    xf = x.reshape(n, d)
    for l in range(depth):
        mod_l = mod[l]
        g0 = ln_g[l, 0].reshape(1, d)
        b0 = ln_b[l, 0].reshape(1, d)
        g1 = ln_g[l, 1].reshape(1, d)
        b1 = ln_b[l, 1].reshape(1, d)
        if l % 2 == 0:
            e = l // 2
            z = mod_matmul(xf, mod_l, ev_w_in[e].astype(BF16), seq=seq, shift_row=0, scale_row=1,
                           out_dtype=F32)
            mixed = even_mix(
                z.reshape(batch, seq, -1), ev_w_pool[e].astype(BF16), ev_pool_scale[e].reshape(1, -1),
                ev_conv_w[e], ev_conv_b[e].reshape(1, -1), ev_w_a[e].astype(BF16),
                ev_b_a[e].reshape(1, -1), ev_w_x[e].astype(BF16), ev_b_x[e].reshape(1, -1),
                ev_lam[e].reshape(1, -1))
            xf = proj_ln(mixed.reshape(n, -1), ev_w_out[e].astype(BF16), xf, mod_l, g0, b0,
                         seq=seq, alpha=alpha, gate_row=2)
        else:
            o = l // 2
            w_in = od_w_in[o]
            w_f = jnp.zeros((d, 128), F32).at[:, :N_HEADS_ATTN].set(w_in[:, 3 * d_mix:])
            wf_hi = w_f.astype(BF16)
            wf_lo = (w_f - wf_hi.astype(F32)).astype(BF16)
            b_f = jnp.zeros((1, 128), F32).at[0, :N_HEADS_ATTN].set(od_b_f[o])
            w_qv = jnp.concatenate([w_in[:, :d_mix], w_in[:, 2 * d_mix:3 * d_mix]], axis=1).astype(BF16)
            w_kt = w_in[:, d_mix:2 * d_mix].T.astype(BF16)
            qv, k_t, f_cum = odd_in(xf, mod_l, w_qv, w_kt, wf_hi, wf_lo, b_f, seq=seq)
            f_keys = f_cum[:, :N_HEADS_ATTN].reshape(batch, seq, N_HEADS_ATTN)
            f_keys = jnp.transpose(f_keys, (0, 2, 1)).reshape(batch * N_HEADS_ATTN, 1, seq)
            attn = attention(qv, k_t, f_keys, batch=batch, seq=seq)
            xf = proj_ln(attn, od_w_out[o].astype(BF16), xf, mod_l, g0, b0,
                         seq=seq, alpha=alpha, gate_row=2)

        keys = peer_keys[l].reshape(2 * PEER_HEADS, N_KEYS, -1).astype(BF16)
        q, h2 = mod_matmul(xf, mod_l, peer_wq[l].astype(BF16), seq=seq, shift_row=3, scale_row=4,
                           out_dtype=BF16, emit_h=True)
        gmat = peer_route(q, keys, seq=seq)
        y = peer_dense(h2, gmat, peer_u[l].astype(BF16), peer_v[l].astype(BF16), seq=seq)
        xf = residual_ln(xf, y, mod_l, g1, b1, seq=seq, alpha=alpha, gate_row=5)
    return xf.reshape(batch, seq, d)
```

```python
import functools
import math

import jax
import jax.numpy as jnp
from jax import lax
from jax.experimental import pallas as pl
from jax.experimental.pallas import tpu as pltpu

F32 = jnp.float32
BF16 = jnp.bfloat16
I32 = jnp.int32

LN_EPS = 1e-5
POOL_WINDOWS = (2, 4, 8, 16)
CONV_WIDTH = 4
LRU_BLOCKS = 8
LRU_C = 8.0
N_HEADS_ATTN = 16
PEER_HEADS = 8
PEER_TOPK = 16
N_KEYS = 128

V7X_VMEM_BYTES = 64 * 1024 * 1024
VMEM_LIMIT = 56 * 1024 * 1024
NEG_INF = float("-inf")
LOG2E = math.log2(math.e)


def _params(sem):
    return pltpu.CompilerParams(dimension_semantics=sem, vmem_limit_bytes=VMEM_LIMIT)


def _resident(shape, index_map):
    return pl.BlockSpec(shape, index_map, pipeline_mode=pl.Buffered(1))


def _gelu(x):
    c = math.sqrt(2.0 / math.pi)
    return 0.5 * x * (1.0 + jnp.tanh(c * (x + 0.044715 * (x * x * x))))


def _log_sigmoid(x):
    return jnp.minimum(x, 0.0) - jnp.log1p(jnp.exp(-jnp.abs(x)))


def _sigmoid(x):
    return 1.0 / (1.0 + jnp.exp(-x))


def _layer_norm(r, g, b):
    mu = jnp.mean(r, axis=-1, keepdims=True)
    d = r - mu
    var = jnp.mean(d * d, axis=-1, keepdims=True)
    return d * lax.rsqrt(var + LN_EPS) * g + b


def _dot_nt(a, b):
    return lax.dot_general(a, b, (((1,), (1,)), ((), ())), preferred_element_type=F32)


def _cast_kernel(w_ref, o_ref):
    o_ref[...] = w_ref[...].astype(o_ref.dtype)


def cast_bf16(w2d, *, tm=1024):
    rows, cols = w2d.shape
    return pl.pallas_call(
        _cast_kernel,
        grid=(rows // tm,),
        in_specs=[pl.BlockSpec((tm, cols), lambda i: (i, 0))],
        out_specs=pl.BlockSpec((tm, cols), lambda i: (i, 0)),
        out_shape=jax.ShapeDtypeStruct((rows, cols), BF16),
        compiler_params=_params(("arbitrary",)),
        name="cast_bf16",
    )(w2d)


def _ada_kernel(c_ref, w_ref, b_ref, o_ref):
    c = c_ref[...]
    ca = c * _sigmoid(c)
    o_ref[...] = jnp.dot(ca, w_ref[...], preferred_element_type=F32) + b_ref[...]


def ada_mod(c_pad, ada_w, ada_b):
    depth, d, n6 = ada_w.shape
    rows = c_pad.shape[0]
    tn = 1024
    return pl.pallas_call(
        _ada_kernel,
        grid=(depth, n6 // tn),
        in_specs=[
            pl.BlockSpec((rows, d), lambda l, j: (0, 0)),
            pl.BlockSpec((None, d, tn), lambda l, j: (l, 0, j)),
            pl.BlockSpec((None, 1, tn), lambda l, j: (l, 0, j)),
        ],
        out_specs=pl.BlockSpec((None, rows, tn), lambda l, j: (l, 0, j)),
        out_shape=jax.ShapeDtypeStruct((depth, rows, n6), F32),
        compiler_params=_params(("arbitrary", "arbitrary")),
        name="ada_mod",
    )(c_pad, ada_w, ada_b.reshape(depth, 1, n6))


def _mod_matmul_kernel(x_ref, mod_ref, w_ref, o_ref, h_ref, *, shift_row, scale_row):
    @pl.when(pl.program_id(1) == 0)
    def _():
        sh = mod_ref[shift_row:shift_row + 1, :]
        sc = mod_ref[scale_row:scale_row + 1, :]
        h_ref[...] = (x_ref[...] * (1.0 + sc) + sh).astype(BF16)

    o_ref[...] = jnp.dot(h_ref[...], w_ref[...], preferred_element_type=F32).astype(o_ref.dtype)


def mod_matmul(x2d, mod_l, w_bf16, *, seq, shift_row, scale_row, out_dtype, emit_h=False,
               tm=1024, tn=1024):
    n, d = x2d.shape
    nout = w_bf16.shape[1]
    tm = min(tm, seq)
    blocks_per_batch = seq // tm
    kern = functools.partial(_mod_matmul_kernel, shift_row=shift_row, scale_row=scale_row)
    out_specs = [pl.BlockSpec((tm, tn), lambda i, j: (i, j))]
    out_shape = [jax.ShapeDtypeStruct((n, nout), out_dtype)]
    scratch = []
    if emit_h:
        out_specs.append(pl.BlockSpec((tm, d), lambda i, j: (i, 0)))
        out_shape.append(jax.ShapeDtypeStruct((n, d), BF16))
    else:
        scratch.append(pltpu.VMEM((tm, d), BF16))
    res = pl.pallas_call(
        kern,
        grid=(n // tm, nout // tn),
        in_specs=[
            pl.BlockSpec((tm, d), lambda i, j: (i, 0)),
            pl.BlockSpec((None, 6, d), lambda i, j: (i // blocks_per_batch, 0, 0)),
            pl.BlockSpec((d, tn), lambda i, j: (0, j)),
        ],
        out_specs=out_specs,
        out_shape=out_shape,
        scratch_shapes=scratch,
        compiler_params=_params(("arbitrary", "arbitrary")),
        name="mod_matmul",
    )(x2d, mod_l, w_bf16)
    return res if emit_h else res[0]


def _cumsum_rows(x):
    rows = x.shape[0]
    ridx = lax.broadcasted_iota(I32, x.shape, 0)
    d = 1
    while d < rows:
        x = x + jnp.where(ridx >= d, pltpu.roll(x, d, 0), 0.0)
        d *= 2
    return x


def _odd_in_kernel(x_ref, mod_ref, w_ref, wkt_ref, wfh_ref, wfl_ref, bf_ref, o_ref, kt_ref, f_ref,
                   h_ref, carry_ref, *, blocks_per_batch, nb, q_scale):
    i = pl.program_id(0)
    j = pl.program_id(1)

    @pl.when(j == 0)
    def _():
        sh = mod_ref[0:1, :]
        sc = mod_ref[1:2, :]
        h = x_ref[...] * (1.0 + sc) + sh
        h_hi = h.astype(BF16)
        h_ref[...] = h_hi
        h_lo = (h - h_hi.astype(F32)).astype(BF16)
        zf = (jnp.dot(h_hi, wfh_ref[...], preferred_element_type=F32)
              + jnp.dot(h_lo, wfh_ref[...], preferred_element_type=F32)
              + jnp.dot(h_hi, wfl_ref[...], preferred_element_type=F32))
        logf = _log_sigmoid(zf + bf_ref[...])

        @pl.when(i % blocks_per_batch == 0)
        def _():
            carry_ref[...] = jnp.zeros_like(carry_ref)

        cs = _cumsum_rows(logf) + carry_ref[0:1, :]
        f_ref[...] = cs
        carry_ref[...] = jnp.broadcast_to(cs[cs.shape[0] - 1:, :], carry_ref.shape)

    is_key = jnp.logical_and(j >= nb, j < 2 * nb)

    @pl.when(jnp.logical_not(is_key))
    def _():
        z = jnp.dot(h_ref[...], w_ref[...], preferred_element_type=F32)
        scale = jnp.where(j < nb, q_scale, 1.0)
        o_ref[...] = (z * scale).astype(o_ref.dtype)

    @pl.when(is_key)
    def _():
        kt_ref[...] = _dot_nt(wkt_ref[...], h_ref[...]).astype(kt_ref.dtype)


def odd_in(x2d, mod_l, w_qv, w_kt, wf_hi, wf_lo, b_f_pad, *, seq, tm=1024, tn=1024):
    n, d = x2d.shape
    d_mix = w_kt.shape[0]
    tm = min(tm, seq)
    blocks_per_batch = seq // tm
    head_dim = d_mix // N_HEADS_ATTN
    nb = d_mix // tn
    kern = functools.partial(_odd_in_kernel, blocks_per_batch=blocks_per_batch, nb=nb,
                             q_scale=head_dim ** -0.5 * LOG2E)

    def qv_block(j):
        return jnp.where(j < nb, j, jnp.where(j < 2 * nb, nb - 1, j - nb))

    def k_block(j):
        return jnp.clip(j - nb, 0, nb - 1)

    return pl.pallas_call(
        kern,
        grid=(n // tm, 3 * nb),
        in_specs=[
            pl.BlockSpec((tm, d), lambda i, j: (i, 0)),
            pl.BlockSpec((None, 6, d), lambda i, j: (i // blocks_per_batch, 0, 0)),
            pl.BlockSpec((d, tn), lambda i, j: (0, qv_block(j))),
            pl.BlockSpec((tn, d), lambda i, j: (k_block(j), 0)),
            _resident((d, 128), lambda i, j: (0, 0)),
            _resident((d, 128), lambda i, j: (0, 0)),
            _resident((1, 128), lambda i, j: (0, 0)),
        ],
        out_specs=[
            pl.BlockSpec((tm, tn), lambda i, j: (i, qv_block(j))),
            pl.BlockSpec((tn, tm), lambda i, j: (k_block(j), i)),
            pl.BlockSpec((tm, 128), lambda i, j: (i, 0)),
        ],
        out_shape=[
            jax.ShapeDtypeStruct((n, 2 * d_mix), BF16),
            jax.ShapeDtypeStruct((d_mix, n), BF16),
            jax.ShapeDtypeStruct((n, 128), F32),
        ],
        scratch_shapes=[pltpu.VMEM((tm, d), BF16), pltpu.VMEM((8, 128), F32)],
        compiler_params=_params(("arbitrary", "arbitrary")),
        name="odd_in",
    )(x2d, mod_l, w_qv, w_kt, wf_hi, wf_lo, b_f_pad)


def _attn_kernel(q_ref, kt_ref, v_ref, fk_ref, o_ref, sa_ref, sb_ref, vaug_ref, *, tq, tk):
    qi = pl.program_id(2)
    f_ref0 = fk_ref[0:1, pl.ds(pl.multiple_of(qi * tq, tq), tk)][:, 0:1]
    dh = q_ref.shape[1]

    @pl.when(qi == 0)
    def _():
        vaug_ref[:, 0:dh] = v_ref[...]
        vaug_ref[:, dh:2 * dh] = jnp.ones((v_ref.shape[0], dh), BF16)
    per_q = tq // tk
    n_full = qi * per_q

    def scores(c, s_ref):
        start = pl.multiple_of(c * tk, tk)
        s = jnp.dot(q_ref[...], kt_ref[:, pl.ds(start, tk)], preferred_element_type=F32)
        s_ref[...] = s + (f_ref0 - fk_ref[0:1, pl.ds(start, tk)]) * LOG2E

    def update(c, s_ref, carry, diag_offset=None):
        m, l, acc = carry
        start = pl.multiple_of(c * tk, tk)
        if diag_offset is not None:
            row = lax.broadcasted_iota(I32, (tq, tk), 0)
            col = lax.broadcasted_iota(I32, (tq, tk), 1) + diag_offset
            s_ref[...] = jnp.where(col <= row, s_ref[...], NEG_INF)
        m_new = jnp.maximum(m, jnp.max(s_ref[...], axis=-1, keepdims=True))
        alpha = jnp.exp2(m - m_new)
        p = jnp.exp2(s_ref[...] - m_new)
        v = vaug_ref[pl.ds(start, tk), :]
        acc = alpha * acc + jnp.dot(p.astype(BF16), v, preferred_element_type=F32)
        return m_new, l, acc

    scores(0, sa_ref)

    def pair(p, carry):
        scores(2 * p + 1, sb_ref)
        carry = update(2 * p, sa_ref, carry)
        scores(2 * p + 2, sa_ref)
        return update(2 * p + 1, sb_ref, carry)

    init = (jnp.full((tq, 1), NEG_INF, F32), jnp.zeros((tq, 1), F32), jnp.zeros((tq, 2 * dh), F32))
    carry = lax.fori_loop(0, n_full // 2, pair, init)
    def diagonal(carry, bufs):
        for dchunk in range(per_q):
            if dchunk + 1 < per_q:
                scores(n_full + dchunk + 1, bufs[(dchunk + 1) % 2])
            carry = update(n_full + dchunk, bufs[dchunk % 2], carry, diag_offset=dchunk * tk)
        return carry

    def even_tail(carry):
        return diagonal(carry, (sa_ref, sb_ref))

    def odd_tail(carry):
        scores(n_full, sb_ref)
        carry = update(n_full - 1, sa_ref, carry)
        return diagonal(carry, (sb_ref, sa_ref))

    if per_q % 2 == 0:
        m, l, acc = even_tail(carry)
    else:
        m, l, acc = lax.cond(n_full % 2 == 1, odd_tail, even_tail, carry)
    o_ref[...] = (acc[:, 0:dh] / acc[:, dh:dh + 1]).astype(o_ref.dtype)


def attention(qv, k_t, f_keys, *, batch, seq, tq=512, tk=512):
    n = qv.shape[0]
    d_mix = qv.shape[1] // 2
    heads = N_HEADS_ATTN
    dh = d_mix // heads
    tq = min(tq, seq)
    tk = min(tk, tq)
    nq = seq // tq
    kern = functools.partial(_attn_kernel, tq=tq, tk=tk)
    return pl.pallas_call(
        kern,
        grid=(batch, heads, nq),
        in_specs=[
            pl.BlockSpec((tq, dh), lambda b, h, i: (b * nq + i, h)),
            pl.BlockSpec((dh, seq), lambda b, h, i: (h, b)),
            pl.BlockSpec((seq, dh), lambda b, h, i: (b, heads + h)),
            pl.BlockSpec((None, 1, seq), lambda b, h, i: (b * heads + h, 0, 0)),
        ],
        out_specs=pl.BlockSpec((tq, dh), lambda b, h, i: (b * nq + i, h)),
        out_shape=jax.ShapeDtypeStruct((n, d_mix), BF16),
        scratch_shapes=[pltpu.VMEM((tq, tk), F32), pltpu.VMEM((tq, tk), F32),
                        pltpu.VMEM((seq, 2 * dh), BF16)],
        compiler_params=_params(("arbitrary", "arbitrary", "arbitrary")),
        name="attention",
    )(qv, k_t, qv, f_keys)


HALO = 16


def _neg_expm1(x):
    p = 1.0 + x / 10.0
    for k in range(9, 1, -1):
        p = 1.0 + (x / k) * p
    return jnp.where(x > -0.35, -(x * p), 1.0 - jnp.exp(x))


def _even_mix_kernel(z_ref, wpool_ref, pscale_ref, convw_ref, convb_ref, wa_ref, ba_ref,
                     wx_ref, bx_ref, lam_ref, o_ref, pbuf, cbuf, hstate, *, ts, d_pool, d_lru):
    sb = pl.program_id(1)
    group = d_pool // len(POOL_WINDOWS)
    blk = d_lru // LRU_BLOCKS

    @pl.when(sb == 0)
    def _():
        pbuf[0:HALO, :] = jnp.zeros((HALO, d_pool), F32)
        cbuf[0:HALO, :] = jnp.zeros((HALO, d_lru), F32)
        hstate[...] = jnp.zeros_like(hstate)

    up = z_ref[:, 0:d_pool]
    pbuf[HALO:HALO + ts, :] = up
    pos = sb * ts + lax.broadcasted_iota(I32, (ts, group), 0)
    ya = []
    for g, w in enumerate(POOL_WINDOWS):
        lo, hi = g * group, (g + 1) * group
        need = w - 1
        cur = pbuf[HALO - need:HALO + ts, lo:hi]
        span = 1
        while span < w:
            rows = cur.shape[0] - span
            cur = cur[span:span + rows, :] + cur[0:rows, :]
            span *= 2
        cnt = jnp.minimum(pos + 1, w).astype(F32)
        pooled = cur / cnt - up[:, lo:hi]
        y = jnp.dot(pooled.astype(BF16), wpool_ref[g], preferred_element_type=F32)
        ya.append(y * pscale_ref[0:1, lo:hi])
    o_ref[:, 0:d_pool] = jnp.concatenate(ya, axis=-1).astype(o_ref.dtype)
    pbuf[0:HALO, :] = pbuf[ts:ts + HALO, :]

    ul = z_ref[:, d_pool:d_pool + d_lru]
    ug = z_ref[:, d_pool + d_lru:d_pool + 2 * d_lru]
    cbuf[HALO:HALO + ts, :] = ul
    xc = jnp.broadcast_to(convb_ref[0:1, :], (ts, d_lru))
    for k in range(CONV_WIDTH):
        off = HALO - (CONV_WIDTH - 1) + k
        xc = xc + cbuf[off:off + ts, :] * convw_ref[k:k + 1, :]
    cbuf[0:HALO, :] = cbuf[ts:ts + HALO, :]

    xb = xc.astype(BF16)
    ra, ia = [], []
    for hb in range(LRU_BLOCKS):
        xs = xb[:, hb * blk:(hb + 1) * blk]
        ra.append(jnp.dot(xs, wa_ref[hb], preferred_element_type=F32))
        ia.append(jnp.dot(xs, wx_ref[hb], preferred_element_type=F32))
    r_gate = _sigmoid(jnp.concatenate(ra, axis=-1) + ba_ref[0:1, :])
    i_gate = _sigmoid(jnp.concatenate(ia, axis=-1) + bx_ref[0:1, :])
    log_a = LRU_C * r_gate * _log_sigmoid(lam_ref[0:1, :])
    a = jnp.exp(log_a)
    mult = jnp.sqrt(jnp.maximum(_neg_expm1(2.0 * log_a), 0.0))
    b = mult * i_gate * xc

    ridx = lax.broadcasted_iota(I32, (ts, d_lru), 0)
    d = 1
    while d < ts:
        keep = ridx >= d
        a_sh = jnp.where(keep, pltpu.roll(a, d, 0), 1.0)
        b_sh = jnp.where(keep, pltpu.roll(b, d, 0), 0.0)
        b = a * b_sh + b
        a = a * a_sh
        d *= 2
    h = b + a * hstate[0:1, :]
    hstate[...] = jnp.broadcast_to(h[ts - 1:ts, :], hstate.shape)
    o_ref[:, d_pool:d_pool + d_lru] = (h * _gelu(ug)).astype(o_ref.dtype)


def even_mix(z3d, w_pool, pool_scale, conv_w, conv_b, w_a, b_a, w_x, b_x, lam, *, ts=256):
    batch, seq, dz = z3d.shape
    d_pool = pool_scale.shape[-1]
    d_lru = lam.shape[-1]
    ts = min(ts, seq)
    group = d_pool // len(POOL_WINDOWS)
    blk = d_lru // LRU_BLOCKS
    kern = functools.partial(_even_mix_kernel, ts=ts, d_pool=d_pool, d_lru=d_lru)
    const2 = lambda b, s: (0, 0)
    const3 = lambda b, s: (0, 0, 0)
    return pl.pallas_call(
        kern,
        grid=(batch, seq // ts),
        in_specs=[
            pl.BlockSpec((None, ts, dz), lambda b, s: (b, s, 0)),
            _resident((len(POOL_WINDOWS), group, group), const3),
            _resident((1, d_pool), const2),
            _resident((CONV_WIDTH, d_lru), const2),
            _resident((1, d_lru), const2),
            _resident((LRU_BLOCKS, blk, blk), const3),
            _resident((1, d_lru), const2),
            _resident((LRU_BLOCKS, blk, blk), const3),
            _resident((1, d_lru), const2),
            _resident((1, d_lru), const2),
        ],
        out_specs=pl.BlockSpec((None, ts, d_pool + d_lru), lambda b, s: (b, s, 0)),
        out_shape=jax.ShapeDtypeStruct((batch, seq, d_pool + d_lru), BF16),
        scratch_shapes=[
            pltpu.VMEM((HALO + ts, d_pool), F32),
            pltpu.VMEM((HALO + ts, d_lru), F32),
            pltpu.VMEM((8, d_lru), F32),
        ],
        compiler_params=_params(("arbitrary", "arbitrary")),
        name="even_mix",
    )(z3d, w_pool, pool_scale, conv_w, conv_b, w_a, b_a, w_x, b_x, lam)


def _proj_ln_kernel(a_ref, w_ref, x_ref, mod_ref, g_ref, b_ref, o_ref, *, alpha, gate_row):
    y = jnp.dot(a_ref[...], w_ref[...], preferred_element_type=F32)
    gate = mod_ref[gate_row:gate_row + 1, :]
    r = alpha * x_ref[...] + (1.0 + gate) * y
    o_ref[...] = _layer_norm(r, g_ref[...], b_ref[...])


def proj_ln(a2d, w_bf16, x2d, mod_l, ln_g, ln_b, *, seq, alpha, gate_row, tm=512):
    n, d = x2d.shape
    k = a2d.shape[1]
    tm = min(tm, seq)
    blocks_per_batch = seq // tm
    kern = functools.partial(_proj_ln_kernel, alpha=alpha, gate_row=gate_row)
    return pl.pallas_call(
        kern,
        grid=(n // tm,),
        in_specs=[
            pl.BlockSpec((tm, k), lambda i: (i, 0)),
            _resident((k, d), lambda i: (0, 0)),
            pl.BlockSpec((tm, d), lambda i: (i, 0)),
            pl.BlockSpec((None, 6, d), lambda i: (i // blocks_per_batch, 0, 0)),
            _resident((1, d), lambda i: (0, 0)),
            _resident((1, d), lambda i: (0, 0)),
        ],
        out_specs=pl.BlockSpec((tm, d), lambda i: (i, 0)),
        out_shape=jax.ShapeDtypeStruct((n, d), F32),
        compiler_params=_params(("arbitrary",)),
        name="proj_ln",
    )(a2d, w_bf16, x2d, mod_l, ln_g, ln_b)


def _top16_rows(s, kidx, sentinel):
    t = s.shape[1]
    r16 = lax.broadcasted_iota(I32, (PEER_TOPK, t), 0)
    vals = jnp.zeros((PEER_TOPK, t), F32)
    idxs = jnp.zeros((PEER_TOPK, t), I32)
    for r in range(PEER_TOPK):
        m = jnp.max(s, axis=0, keepdims=True)
        am = jnp.min(jnp.where(s == m, kidx, sentinel), axis=0, keepdims=True)
        s = jnp.where(kidx == am, NEG_INF, s)
        vals = jnp.where(r16 == r, m, vals)
        idxs = jnp.where(r16 == r, am, idxs)
    return vals, idxs


def _take_rows16(table, sel):
    out = jnp.zeros(sel.shape, table.dtype)
    for s in range(PEER_TOPK):
        out = jnp.where(sel == s, table[s:s + 1, :], out)
    return out


SUBLANES = 8
BUILD_UNROLL = 32


def _peer_route_kernel(q_ref, keys_ref, g_ref,
                       gs_ref, sc_ref, val_ref, idx_ref, e_ref, w_ref, et_ref, wt_ref, *, tr):
    nhp = 2 * PEER_HEADS
    for hp in range(nhp):
        sc_ref[hp] = _dot_nt(keys_ref[hp], q_ref[:, hp * N_KEYS:(hp + 1) * N_KEYS])

    kidx = lax.broadcasted_iota(I32, (N_KEYS, tr), 0)

    def first_topk(hd, carry):
        for half in range(2):
            hp = 2 * hd + half
            vals, idxs = _top16_rows(sc_ref[hp], kidx, N_KEYS)
            val_ref[hp] = vals
            idx_ref[hp] = idxs
        return carry

    lax.fori_loop(0, PEER_HEADS, first_topk, 0)

    i16 = lax.broadcasted_iota(I32, (16, tr), 0)
    i8 = lax.broadcasted_iota(I32, (8, tr), 0)
    flat = jnp.concatenate([
        i16 * 16, i8 * 16 + 1, i8 * 16 + 2, i8 * 16 + 3,
        i16, 16 + i8, 32 + i8,
    ], axis=0)
    valid = jnp.concatenate([
        i16 < 16, i8 < 8, i8 < 5, i8 < 4,
        i16 >= 4, i8 >= 4, i8 == 4,
    ], axis=0)

    def second_topk(hd, carry):
        s1 = val_ref[2 * hd]
        s2 = val_ref[2 * hd + 1]
        cand = jnp.concatenate([
            s1 + s2[0:1, :], s1[0:8, :] + s2[1:2, :], s1[0:8, :] + s2[2:3, :], s1[0:8, :] + s2[3:4, :],
            s1[0:1, :] + s2, s1[1:2, :] + s2[0:8, :], s1[2:3, :] + s2[0:8, :],
        ], axis=0)
        cand = jnp.where(valid, cand, NEG_INF)
        top, fsel = _top16_rows(cand, flat, 256)
        e = jnp.exp(top - top[0:1, :])
        gate = e / jnp.sum(e, axis=0, keepdims=True)
        a = _take_rows16(idx_ref[2 * hd], fsel >> 4)
        b = _take_rows16(idx_ref[2 * hd + 1], fsel & 15)
        row0 = pl.multiple_of(hd * PEER_TOPK, PEER_TOPK)
        e_ref[pl.ds(row0, PEER_TOPK), :] = a * N_KEYS + b
        w_ref[pl.ds(row0, PEER_TOPK), :] = gate
        return carry

    lax.fori_loop(0, PEER_HEADS, second_topk, 0)

    n_act = PEER_HEADS * PEER_TOPK
    for c in range(tr // n_act):
        et_ref[c * n_act:(c + 1) * n_act, :] = e_ref[:, c * n_act:(c + 1) * n_act].T
        wt_ref[c * n_act:(c + 1) * n_act, :] = w_ref[:, c * n_act:(c + 1) * n_act].T

    iota_rows = lax.broadcasted_iota(I32, (N_KEYS, n_act), 0)
    sub_iota = lax.broadcasted_iota(I32, (SUBLANES, N_KEYS), 0)

    def build(step, carry):
        base = pl.multiple_of(step * BUILD_UNROLL, BUILD_UNROLL)
        erows = et_ref[pl.ds(base, BUILD_UNROLL), :]
        wrows = wt_ref[pl.ds(base, BUILD_UNROLL), :]
        for u in range(BUILD_UNROLL):
            erow = erows[u:u + 1, :]
            wrow = wrows[u:u + 1, :]
            pt = jnp.where(iota_rows == (erow >> 7), wrow, 0.0).astype(BF16)
            qt = jnp.where(iota_rows == (erow & (N_KEYS - 1)), 1.0, 0.0).astype(BF16)
            gs_ref[u * N_KEYS:(u + 1) * N_KEYS, :] = _dot_nt(pt, qt)
        for gi in range(BUILD_UNROLL // SUBLANES):
            grp = step * (BUILD_UNROLL // SUBLANES) + gi
            for k in range(N_KEYS // SUBLANES):
                tiles = []
                for t in range(SUBLANES):
                    row0 = (gi * SUBLANES + t) * N_KEYS + k * SUBLANES
                    tiles.append(gs_ref[row0:row0 + SUBLANES, :])
                for dist in (4, 2, 1):
                    take_hi = (sub_iota & dist) != 0
                    for t in range(SUBLANES):
                        if t & dist:
                            continue
                        lo, hi = tiles[t], tiles[t + dist]
                        tiles[t] = jnp.where(take_hi, pltpu.roll(hi, dist, 0), lo)
                        tiles[t + dist] = jnp.where(take_hi, hi, pltpu.roll(lo, SUBLANES - dist, 0))
                for s in range(SUBLANES):
                    g_ref[grp, k * SUBLANES + s] = tiles[s]
        return carry

    lax.fori_loop(0, tr // BUILD_UNROLL, build, 0)


def peer_route(q2d, keys_bf16, *, seq, tr=256):
    n, d = q2d.shape
    tr = min(tr, seq)
    nhp = 2 * PEER_HEADS
    n_act = PEER_HEADS * PEER_TOPK
    kern = functools.partial(_peer_route_kernel, tr=tr)
    return pl.pallas_call(
        kern,
        grid=(n // tr,),
        in_specs=[
            pl.BlockSpec((tr, d), lambda i: (i, 0)),
            _resident(keys_bf16.shape, lambda i: (0, 0, 0)),
        ],
        out_specs=pl.BlockSpec((tr // SUBLANES, N_KEYS, SUBLANES, N_KEYS), lambda i: (i, 0, 0, 0)),
        out_shape=jax.ShapeDtypeStruct((n // SUBLANES, N_KEYS, SUBLANES, N_KEYS), F32),
        scratch_shapes=[
            pltpu.VMEM((BUILD_UNROLL * N_KEYS, N_KEYS), F32),
            pltpu.VMEM((nhp, N_KEYS, tr), F32),
            pltpu.VMEM((nhp, PEER_TOPK, tr), F32),
            pltpu.VMEM((nhp, PEER_TOPK, tr), I32),
            pltpu.VMEM((n_act, tr), I32),
            pltpu.VMEM((n_act, tr), F32),
            pltpu.VMEM((tr, n_act), I32),
            pltpu.VMEM((tr, n_act), F32),
        ],
        compiler_params=_params(("arbitrary",)),
        name="peer_route",
    )(q2d, keys_bf16)


DENSE_CHUNK = 256


def _peer_dense_kernel(h_ref, g_ref, u_ref, v_ref, o_ref, act_ref):
    j = pl.program_id(1)
    rows = g_ref.shape[1]
    tm = h_ref.shape[0]

    @pl.when(j == 0)
    def _():
        o_ref[...] = jnp.zeros_like(o_ref)

    per = DENSE_CHUNK // N_KEYS
    for c in range(rows // per):
        ccols = slice(c * DENSE_CHUNK, (c + 1) * DENSE_CHUNK)
        z = _dot_nt(h_ref[...], u_ref[ccols, :])
        for r in range(per):
            row = c * per + r
            gate = g_ref[:, row, :, :].reshape(tm, N_KEYS)
            act_ref[:, row * N_KEYS:(row + 1) * N_KEYS] = (
                _gelu(z[:, r * N_KEYS:(r + 1) * N_KEYS]) * gate).astype(BF16)
    half = act_ref.shape[1] // 2
    o_ref[...] += jnp.dot(act_ref[:, :half], v_ref[:half, :], preferred_element_type=F32)
    o_ref[...] += jnp.dot(act_ref[:, half:], v_ref[half:, :], preferred_element_type=F32)


def peer_dense(h2d, g3d, u_bf16, v_bf16, *, layer, seq, tm=1024, te=1024):
    n, d = h2d.shape
    n_exp = v_bf16.shape[1]
    tm = min(tm, seq)
    return pl.pallas_call(
        _peer_dense_kernel,
        grid=(n // tm, n_exp // te),
        in_specs=[
            pl.BlockSpec((tm, d), lambda i, j: (i, 0), pipeline_mode=pl.Buffered(1)),
            pl.BlockSpec((tm // SUBLANES, te // N_KEYS, SUBLANES, N_KEYS), lambda i, j: (i, j, 0, 0)),
            pl.BlockSpec((None, te, d), lambda i, j: (layer, j, 0)),
            pl.BlockSpec((None, te, d), lambda i, j: (layer, j, 0)),
        ],
        out_specs=pl.BlockSpec((tm, d), lambda i, j: (i, 0)),
        out_shape=jax.ShapeDtypeStruct((n, d), F32),
        scratch_shapes=[pltpu.VMEM((tm, te), BF16)],
        compiler_params=_params(("arbitrary", "arbitrary")),
        name="peer_dense",
    )(h2d, g3d, u_bf16, v_bf16)


def _residual_ln_kernel(x_ref, y_ref, mod_ref, g_ref, b_ref, o_ref, *, alpha, gate_row):
    gate = mod_ref[gate_row:gate_row + 1, :]
    r = alpha * x_ref[...] + (1.0 + gate) * y_ref[...]
    o_ref[...] = _layer_norm(r, g_ref[...], b_ref[...])


def residual_ln(x2d, y2d, mod_l, ln_g, ln_b, *, seq, alpha, gate_row, tm=512):
    n, d = x2d.shape
    tm = min(tm, seq)
    blocks_per_batch = seq // tm
    kern = functools.partial(_residual_ln_kernel, alpha=alpha, gate_row=gate_row)
    return pl.pallas_call(
        kern,
        grid=(n // tm,),
        in_specs=[
            pl.BlockSpec((tm, d), lambda i: (i, 0)),
            pl.BlockSpec((tm, d), lambda i: (i, 0)),
            pl.BlockSpec((None, 6, d), lambda i: (i // blocks_per_batch, 0, 0)),
            _resident((1, d), lambda i: (0, 0)),
            _resident((1, d), lambda i: (0, 0)),
        ],
        out_specs=pl.BlockSpec((tm, d), lambda i: (i, 0)),
        out_shape=jax.ShapeDtypeStruct((n, d), F32),
        compiler_params=_params(("arbitrary",)),
        name="residual_ln",
    )(x2d, y2d, mod_l, ln_g, ln_b)


def kernel(x, c, ada_w, ada_b, ln_g, ln_b, peer_wq, peer_keys, peer_u, peer_v, ev_w_in, ev_w_pool,
           ev_pool_scale, ev_conv_w, ev_conv_b, ev_w_a, ev_b_a, ev_w_x, ev_b_x, ev_lam, ev_w_out,
           od_w_in, od_b_f, od_w_out):
    batch, seq, d = x.shape
    depth = ada_w.shape[0]
    n = batch * seq
    alpha = (2.0 * depth) ** 0.25
    d_mix = od_w_out.shape[1]

    c_pad = jnp.zeros((8, d), F32).at[:batch].set(c)
    mod = ada_mod(c_pad, ada_w, ada_b)[:, :batch].reshape(depth, batch, 6, d)

    n_exp = peer_u.shape[1]
    u_bf = cast_bf16(peer_u.reshape(depth * n_exp, d)).reshape(depth, n_exp, d)
    v_bf = cast_bf16(peer_v.reshape(depth * n_exp, d)).reshape(depth, n_exp, d)

    xf = x.reshape(n, d)
    for l in range(depth):
        mod_l = mod[l]
        g0 = ln_g[l, 0].reshape(1, d)
        b0 = ln_b[l, 0].reshape(1, d)
        g1 = ln_g[l, 1].reshape(1, d)
        b1 = ln_b[l, 1].reshape(1, d)
        if l % 2 == 0:
            e = l // 2
            z = mod_matmul(xf, mod_l, ev_w_in[e].astype(BF16), seq=seq, shift_row=0, scale_row=1,
                           out_dtype=F32)
            mixed = even_mix(
                z.reshape(batch, seq, -1), ev_w_pool[e].astype(BF16), ev_pool_scale[e].reshape(1, -1),
                ev_conv_w[e], ev_conv_b[e].reshape(1, -1), ev_w_a[e].astype(BF16),
                ev_b_a[e].reshape(1, -1), ev_w_x[e].astype(BF16), ev_b_x[e].reshape(1, -1),
                ev_lam[e].reshape(1, -1))
            xf = proj_ln(mixed.reshape(n, -1), ev_w_out[e].astype(BF16), xf, mod_l, g0, b0,
                         seq=seq, alpha=alpha, gate_row=2)
        else:
            o = l // 2
            w_in = od_w_in[o]
            w_f = jnp.zeros((d, 128), F32).at[:, :N_HEADS_ATTN].set(w_in[:, 3 * d_mix:])
            wf_hi = w_f.astype(BF16)
            wf_lo = (w_f - wf_hi.astype(F32)).astype(BF16)
            b_f = jnp.zeros((1, 128), F32).at[0, :N_HEADS_ATTN].set(od_b_f[o])
            w_qv = jnp.concatenate([w_in[:, :d_mix], w_in[:, 2 * d_mix:3 * d_mix]], axis=1).astype(BF16)
            w_kt = w_in[:, d_mix:2 * d_mix].T.astype(BF16)
            qv, k_t, f_cum = odd_in(xf, mod_l, w_qv, w_kt, wf_hi, wf_lo, b_f, seq=seq)
            f_keys = f_cum[:, :N_HEADS_ATTN].reshape(batch, seq, N_HEADS_ATTN)
            f_keys = jnp.transpose(f_keys, (0, 2, 1)).reshape(batch * N_HEADS_ATTN, 1, seq)
            attn = attention(qv, k_t, f_keys, batch=batch, seq=seq)
            xf = proj_ln(attn, od_w_out[o].astype(BF16), xf, mod_l, g0, b0,
                         seq=seq, alpha=alpha, gate_row=2)

        keys = peer_keys[l].reshape(2 * PEER_HEADS, N_KEYS, -1).astype(BF16)
        q, h2 = mod_matmul(xf, mod_l, peer_wq[l].astype(BF16), seq=seq, shift_row=3, scale_row=4,
                           out_dtype=BF16, emit_h=True)
        gmat = peer_route(q, keys, seq=seq)
        y = peer_dense(h2, gmat, u_bf, v_bf, layer=l, seq=seq)
        xf = residual_ln(xf, y, mod_l, g1, b1, seq=seq, alpha=alpha, gate_row=5)
    return xf.reshape(batch, seq, d)
```

```python
import functools
import math

import jax
import jax.numpy as jnp
from jax import lax
from jax.experimental import pallas as pl
from jax.experimental.pallas import tpu as pltpu

F32 = jnp.float32
BF16 = jnp.bfloat16
I32 = jnp.int32

LN_EPS = 1e-5
POOL_WINDOWS = (2, 4, 8, 16)
CONV_WIDTH = 4
LRU_BLOCKS = 8
LRU_C = 8.0
N_HEADS_ATTN = 16
PEER_HEADS = 8
PEER_TOPK = 16
N_KEYS = 128

V7X_VMEM_BYTES = 64 * 1024 * 1024
VMEM_LIMIT = 56 * 1024 * 1024
NEG_INF = float("-inf")
LOG2E = math.log2(math.e)


def _params(sem):
    return pltpu.CompilerParams(dimension_semantics=sem, vmem_limit_bytes=VMEM_LIMIT)


def _resident(shape, index_map):
    return pl.BlockSpec(shape, index_map, pipeline_mode=pl.Buffered(1))


def _gelu(x):
    c = math.sqrt(2.0 / math.pi)
    return 0.5 * x * (1.0 + jnp.tanh(c * (x + 0.044715 * (x * x * x))))


def _log_sigmoid(x):
    return jnp.minimum(x, 0.0) - jnp.log1p(jnp.exp(-jnp.abs(x)))


def _sigmoid(x):
    return 1.0 / (1.0 + jnp.exp(-x))


def _layer_norm(r, g, b):
    mu = jnp.mean(r, axis=-1, keepdims=True)
    d = r - mu
    var = jnp.mean(d * d, axis=-1, keepdims=True)
    return d * lax.rsqrt(var + LN_EPS) * g + b


def _dot_nt(a, b):
    return lax.dot_general(a, b, (((1,), (1,)), ((), ())), preferred_element_type=F32)


def _cast_kernel(w_ref, o_ref):
    o_ref[...] = w_ref[...].astype(o_ref.dtype)


def cast_bf16(w2d, *, tm=1024):
    rows, cols = w2d.shape
    return pl.pallas_call(
        _cast_kernel,
        grid=(rows // tm,),
        in_specs=[pl.BlockSpec((tm, cols), lambda i: (i, 0))],
        out_specs=pl.BlockSpec((tm, cols), lambda i: (i, 0)),
        out_shape=jax.ShapeDtypeStruct((rows, cols), BF16),
        compiler_params=_params(("arbitrary",)),
        name="cast_bf16",
    )(w2d)


def _ada_kernel(c_ref, w_ref, b_ref, o_ref):
    c = c_ref[...]
    ca = c * _sigmoid(c)
    o_ref[...] = jnp.dot(ca, w_ref[...], preferred_element_type=F32) + b_ref[...]


def ada_mod(c_pad, ada_w, ada_b):
    depth, d, n6 = ada_w.shape
    rows = c_pad.shape[0]
    tn = 1024
    return pl.pallas_call(
        _ada_kernel,
        grid=(depth, n6 // tn),
        in_specs=[
            pl.BlockSpec((rows, d), lambda l, j: (0, 0)),
            pl.BlockSpec((None, d, tn), lambda l, j: (l, 0, j)),
            pl.BlockSpec((None, 1, tn), lambda l, j: (l, 0, j)),
        ],
        out_specs=pl.BlockSpec((None, rows, tn), lambda l, j: (l, 0, j)),
        out_shape=jax.ShapeDtypeStruct((depth, rows, n6), F32),
        compiler_params=_params(("arbitrary", "arbitrary")),
        name="ada_mod",
    )(c_pad, ada_w, ada_b.reshape(depth, 1, n6))


def _mod_matmul_kernel(x_ref, mod_ref, w_ref, o_ref, h_ref, *, shift_row, scale_row):
    @pl.when(pl.program_id(1) == 0)
    def _():
        sh = mod_ref[shift_row:shift_row + 1, :]
        sc = mod_ref[scale_row:scale_row + 1, :]
        h_ref[...] = (x_ref[...] * (1.0 + sc) + sh).astype(BF16)

    o_ref[...] = jnp.dot(h_ref[...], w_ref[...], preferred_element_type=F32).astype(o_ref.dtype)


def mod_matmul(x2d, mod_l, w_bf16, *, seq, shift_row, scale_row, out_dtype, emit_h=False,
               tm=1024, tn=1024):
    n, d = x2d.shape
    nout = w_bf16.shape[1]
    tm = min(tm, seq)
    blocks_per_batch = seq // tm
    kern = functools.partial(_mod_matmul_kernel, shift_row=shift_row, scale_row=scale_row)
    out_specs = [pl.BlockSpec((tm, tn), lambda i, j: (i, j))]
    out_shape = [jax.ShapeDtypeStruct((n, nout), out_dtype)]
    scratch = []
    if emit_h:
        out_specs.append(pl.BlockSpec((tm, d), lambda i, j: (i, 0)))
        out_shape.append(jax.ShapeDtypeStruct((n, d), BF16))
    else:
        scratch.append(pltpu.VMEM((tm, d), BF16))
    res = pl.pallas_call(
        kern,
        grid=(n // tm, nout // tn),
        in_specs=[
            pl.BlockSpec((tm, d), lambda i, j: (i, 0)),
            pl.BlockSpec((None, 6, d), lambda i, j: (i // blocks_per_batch, 0, 0)),
            pl.BlockSpec((d, tn), lambda i, j: (0, j)),
        ],
        out_specs=out_specs,
        out_shape=out_shape,
        scratch_shapes=scratch,
        compiler_params=_params(("arbitrary", "arbitrary")),
        name="mod_matmul",
    )(x2d, mod_l, w_bf16)
    return res if emit_h else res[0]


def _cumsum_rows(x):
    rows = x.shape[0]
    ridx = lax.broadcasted_iota(I32, x.shape, 0)
    d = 1
    while d < rows:
        x = x + jnp.where(ridx >= d, pltpu.roll(x, d, 0), 0.0)
        d *= 2
    return x


def _odd_in_kernel(x_ref, mod_ref, w_ref, wkt_ref, wfh_ref, wfl_ref, bf_ref, o_ref, kt_ref, f_ref,
                   h_ref, carry_ref, *, blocks_per_batch, nb, q_scale):
    i = pl.program_id(0)
    j = pl.program_id(1)

    @pl.when(j == 0)
    def _():
        sh = mod_ref[0:1, :]
        sc = mod_ref[1:2, :]
        h = x_ref[...] * (1.0 + sc) + sh
        h_hi = h.astype(BF16)
        h_ref[...] = h_hi
        h_lo = (h - h_hi.astype(F32)).astype(BF16)
        zf = (jnp.dot(h_hi, wfh_ref[...], preferred_element_type=F32)
              + jnp.dot(h_lo, wfh_ref[...], preferred_element_type=F32)
              + jnp.dot(h_hi, wfl_ref[...], preferred_element_type=F32))
        logf = _log_sigmoid(zf + bf_ref[...])

        @pl.when(i % blocks_per_batch == 0)
        def _():
            carry_ref[...] = jnp.zeros_like(carry_ref)

        cs = _cumsum_rows(logf) + carry_ref[0:1, :]
        f_ref[...] = cs
        carry_ref[...] = jnp.broadcast_to(cs[cs.shape[0] - 1:, :], carry_ref.shape)

    is_key = jnp.logical_and(j >= nb, j < 2 * nb)

    @pl.when(jnp.logical_not(is_key))
    def _():
        z = jnp.dot(h_ref[...], w_ref[...], preferred_element_type=F32)
        scale = jnp.where(j < nb, q_scale, 1.0)
        o_ref[...] = (z * scale).astype(o_ref.dtype)

    @pl.when(is_key)
    def _():
        kt_ref[...] = _dot_nt(wkt_ref[...], h_ref[...]).astype(kt_ref.dtype)


def odd_in(x2d, mod_l, w_qv, w_kt, wf_hi, wf_lo, b_f_pad, *, seq, tm=1024, tn=1024):
    n, d = x2d.shape
    d_mix = w_kt.shape[0]
    tm = min(tm, seq)
    blocks_per_batch = seq // tm
    head_dim = d_mix // N_HEADS_ATTN
    nb = d_mix // tn
    kern = functools.partial(_odd_in_kernel, blocks_per_batch=blocks_per_batch, nb=nb,
                             q_scale=head_dim ** -0.5 * LOG2E)

    def qv_block(j):
        return jnp.where(j < nb, j, jnp.where(j < 2 * nb, nb - 1, j - nb))

    def k_block(j):
        return jnp.clip(j - nb, 0, nb - 1)

    return pl.pallas_call(
        kern,
        grid=(n // tm, 3 * nb),
        in_specs=[
            pl.BlockSpec((tm, d), lambda i, j: (i, 0)),
            pl.BlockSpec((None, 6, d), lambda i, j: (i // blocks_per_batch, 0, 0)),
            pl.BlockSpec((d, tn), lambda i, j: (0, qv_block(j))),
            pl.BlockSpec((tn, d), lambda i, j: (k_block(j), 0)),
            _resident((d, 128), lambda i, j: (0, 0)),
            _resident((d, 128), lambda i, j: (0, 0)),
            _resident((1, 128), lambda i, j: (0, 0)),
        ],
        out_specs=[
            pl.BlockSpec((tm, tn), lambda i, j: (i, qv_block(j))),
            pl.BlockSpec((tn, tm), lambda i, j: (k_block(j), i)),
            pl.BlockSpec((tm, 128), lambda i, j: (i, 0)),
        ],
        out_shape=[
            jax.ShapeDtypeStruct((n, 2 * d_mix), BF16),
            jax.ShapeDtypeStruct((d_mix, n), BF16),
            jax.ShapeDtypeStruct((n, 128), F32),
        ],
        scratch_shapes=[pltpu.VMEM((tm, d), BF16), pltpu.VMEM((8, 128), F32)],
        compiler_params=_params(("arbitrary", "arbitrary")),
        name="odd_in",
    )(x2d, mod_l, w_qv, w_kt, wf_hi, wf_lo, b_f_pad)


def _attn_kernel(q_ref, kt_ref, v_ref, fk_ref, o_ref, sa_ref, sb_ref, vaug_ref, pb_ref, *, tq, tk):
    qi = pl.program_id(2)
    f_ref0 = fk_ref[0:1, pl.ds(pl.multiple_of(qi * tq, tq), tk)][:, 0:1]
    dh = q_ref.shape[1]

    @pl.when(qi == 0)
    def _():
        vaug_ref[:, 0:dh] = v_ref[...]
        vaug_ref[:, dh:2 * dh] = jnp.ones((v_ref.shape[0], dh), BF16)
    per_q = tq // tk
    n_full = qi * per_q

    def scores(c, s_ref):
        start = pl.multiple_of(c * tk, tk)
        s = jnp.dot(q_ref[...], kt_ref[:, pl.ds(start, tk)], preferred_element_type=F32)
        s_ref[...] = s + (f_ref0 - fk_ref[0:1, pl.ds(start, tk)]) * LOG2E

    def values(c, p):
        start = pl.multiple_of(c * tk, tk)
        return jnp.dot(p, vaug_ref[pl.ds(start, tk), :], preferred_element_type=F32)

    def softmax_step(s_ref, m, diag_offset=None):
        if diag_offset is not None:
            row = lax.broadcasted_iota(I32, (tq, tk), 0)
            col = lax.broadcasted_iota(I32, (tq, tk), 1) + diag_offset
            s_ref[...] = jnp.where(col <= row, s_ref[...], NEG_INF)
        m_new = jnp.maximum(m, jnp.max(s_ref[...], axis=-1, keepdims=True))
        alpha = jnp.exp2(m - m_new)
        p = jnp.exp2(s_ref[...] - m_new).astype(BF16)
        return m_new, alpha, p

    def update(c, s_ref, carry, diag_offset=None):
        m, acc = carry
        m, alpha, p = softmax_step(s_ref, m, diag_offset)
        return m, alpha * acc + values(c, p)

    pb_ref[...] = jnp.zeros(pb_ref.shape, BF16)
    scores(0, sa_ref)

    def group(c0, count, carry):
        m, acc = carry
        acc = acc + values(jnp.maximum(c0 - 1, 0), pb_ref[...])
        bufs = (sa_ref, sb_ref)
        for t in range(count):
            scores(c0 + t + 1, bufs[(t + 1) % 2])
            m, alpha, p = softmax_step(bufs[t % 2], m)
            if t + 1 < count:
                acc = alpha * acc + values(c0 + t, p)
            else:
                pb_ref[...] = p
                acc = alpha * acc
        return m, acc

    init = (jnp.full((tq, 1), NEG_INF, F32), jnp.zeros((tq, 2 * dh), F32))
    carry = init
    done = 0
    for count in (8, 4, 2):
        steps = (n_full - done) // count
        carry = lax.fori_loop(
            0, steps, lambda p, cr, base=done, count=count: group(base + count * p, count, cr), carry)
        done = done + count * steps
    m, acc = carry
    acc = acc + values(jnp.maximum(done - 1, 0), pb_ref[...])
    carry = (m, acc)

    def diagonal(carry, bufs):
        for dchunk in range(per_q):
            if dchunk + 1 < per_q:
                scores(n_full + dchunk + 1, bufs[(dchunk + 1) % 2])
            carry = update(n_full + dchunk, bufs[dchunk % 2], carry, diag_offset=dchunk * tk)
        return carry

    def even_tail(carry):
        return diagonal(carry, (sa_ref, sb_ref))

    def odd_tail(carry):
        scores(n_full, sb_ref)
        carry = update(n_full - 1, sa_ref, carry)
        return diagonal(carry, (sb_ref, sa_ref))

    if per_q % 2 == 0:
        m, acc = even_tail(carry)
    else:
        m, acc = lax.cond(n_full % 2 == 1, odd_tail, even_tail, carry)
    o_ref[...] = (acc[:, 0:dh] / acc[:, dh:dh + 1]).astype(o_ref.dtype)


def attention(qv, k_t, f_keys, *, batch, seq, tq=512, tk=512):
    n = qv.shape[0]
    d_mix = qv.shape[1] // 2
    heads = N_HEADS_ATTN
    dh = d_mix // heads
    tq = min(tq, seq)
    tk = min(tk, tq)
    nq = seq // tq
    kern = functools.partial(_attn_kernel, tq=tq, tk=tk)
    return pl.pallas_call(
        kern,
        grid=(batch, heads, nq),
        in_specs=[
            pl.BlockSpec((tq, dh), lambda b, h, i: (b * nq + i, h)),
            pl.BlockSpec((dh, seq), lambda b, h, i: (h, b)),
            pl.BlockSpec((seq, dh), lambda b, h, i: (b, heads + h)),
            pl.BlockSpec((None, 1, seq), lambda b, h, i: (b * heads + h, 0, 0)),
        ],
        out_specs=pl.BlockSpec((tq, dh), lambda b, h, i: (b * nq + i, h)),
        out_shape=jax.ShapeDtypeStruct((n, d_mix), BF16),
        scratch_shapes=[pltpu.VMEM((tq, tk), F32), pltpu.VMEM((tq, tk), F32),
                        pltpu.VMEM((seq, 2 * dh), BF16), pltpu.VMEM((tq, tk), BF16)],
        compiler_params=_params(("arbitrary", "arbitrary", "arbitrary")),
        name="attention",
    )(qv, k_t, qv, f_keys)


HALO = 16


def _neg_expm1(x):
    p = 1.0 + x / 10.0
    for k in range(9, 1, -1):
        p = 1.0 + (x / k) * p
    return jnp.where(x > -0.35, -(x * p), 1.0 - jnp.exp(x))


def _even_mix_kernel(z_ref, wpool_ref, pscale_ref, convw_ref, convb_ref, wa_ref, ba_ref,
                     wx_ref, bx_ref, lam_ref, o_ref, pbuf, cbuf, hstate, *, ts, d_pool, d_lru):
    sb = pl.program_id(1)
    group = d_pool // len(POOL_WINDOWS)
    blk = d_lru // LRU_BLOCKS

    @pl.when(sb == 0)
    def _():
        pbuf[0:HALO, :] = jnp.zeros((HALO, d_pool), F32)
        cbuf[0:HALO, :] = jnp.zeros((HALO, d_lru), F32)
        hstate[...] = jnp.zeros_like(hstate)

    up = z_ref[:, 0:d_pool]
    pbuf[HALO:HALO + ts, :] = up
    pos = sb * ts + lax.broadcasted_iota(I32, (ts, group), 0)
    ya = []
    for g, w in enumerate(POOL_WINDOWS):
        lo, hi = g * group, (g + 1) * group
        need = w - 1
        cur = pbuf[HALO - need:HALO + ts, lo:hi]
        span = 1
        while span < w:
            rows = cur.shape[0] - span
            cur = cur[span:span + rows, :] + cur[0:rows, :]
            span *= 2
        cnt = jnp.minimum(pos + 1, w).astype(F32)
        pooled = cur / cnt - up[:, lo:hi]
        y = jnp.dot(pooled.astype(BF16), wpool_ref[g], preferred_element_type=F32)
        ya.append(y * pscale_ref[0:1, lo:hi])
    o_ref[:, 0:d_pool] = jnp.concatenate(ya, axis=-1).astype(o_ref.dtype)
    pbuf[0:HALO, :] = pbuf[ts:ts + HALO, :]

    ul = z_ref[:, d_pool:d_pool + d_lru]
    ug = z_ref[:, d_pool + d_lru:d_pool + 2 * d_lru]
    cbuf[HALO:HALO + ts, :] = ul
    xc = jnp.broadcast_to(convb_ref[0:1, :], (ts, d_lru))
    for k in range(CONV_WIDTH):
        off = HALO - (CONV_WIDTH - 1) + k
        xc = xc + cbuf[off:off + ts, :] * convw_ref[k:k + 1, :]
    cbuf[0:HALO, :] = cbuf[ts:ts + HALO, :]

    xb = xc.astype(BF16)
    ra, ia = [], []
    for hb in range(LRU_BLOCKS):
        xs = xb[:, hb * blk:(hb + 1) * blk]
        ra.append(jnp.dot(xs, wa_ref[hb], preferred_element_type=F32))
        ia.append(jnp.dot(xs, wx_ref[hb], preferred_element_type=F32))
    r_gate = _sigmoid(jnp.concatenate(ra, axis=-1) + ba_ref[0:1, :])
    i_gate = _sigmoid(jnp.concatenate(ia, axis=-1) + bx_ref[0:1, :])
    log_a = LRU_C * r_gate * _log_sigmoid(lam_ref[0:1, :])
    a = jnp.exp(log_a)
    mult = jnp.sqrt(jnp.maximum(_neg_expm1(2.0 * log_a), 0.0))
    b = mult * i_gate * xc

    ridx = lax.broadcasted_iota(I32, (ts, d_lru), 0)
    d = 1
    while d < ts:
        keep = ridx >= d
        a_sh = jnp.where(keep, pltpu.roll(a, d, 0), 1.0)
        b_sh = jnp.where(keep, pltpu.roll(b, d, 0), 0.0)
        b = a * b_sh + b
        a = a * a_sh
        d *= 2
    h = b + a * hstate[0:1, :]
    hstate[...] = jnp.broadcast_to(h[ts - 1:ts, :], hstate.shape)
    o_ref[:, d_pool:d_pool + d_lru] = (h * _gelu(ug)).astype(o_ref.dtype)


def even_mix(z3d, w_pool, pool_scale, conv_w, conv_b, w_a, b_a, w_x, b_x, lam, *, ts=256):
    batch, seq, dz = z3d.shape
    d_pool = pool_scale.shape[-1]
    d_lru = lam.shape[-1]
    ts = min(ts, seq)
    group = d_pool // len(POOL_WINDOWS)
    blk = d_lru // LRU_BLOCKS
    kern = functools.partial(_even_mix_kernel, ts=ts, d_pool=d_pool, d_lru=d_lru)
    const2 = lambda b, s: (0, 0)
    const3 = lambda b, s: (0, 0, 0)
    return pl.pallas_call(
        kern,
        grid=(batch, seq // ts),
        in_specs=[
            pl.BlockSpec((None, ts, dz), lambda b, s: (b, s, 0)),
            _resident((len(POOL_WINDOWS), group, group), const3),
            _resident((1, d_pool), const2),
            _resident((CONV_WIDTH, d_lru), const2),
            _resident((1, d_lru), const2),
            _resident((LRU_BLOCKS, blk, blk), const3),
            _resident((1, d_lru), const2),
            _resident((LRU_BLOCKS, blk, blk), const3),
            _resident((1, d_lru), const2),
            _resident((1, d_lru), const2),
        ],
        out_specs=pl.BlockSpec((None, ts, d_pool + d_lru), lambda b, s: (b, s, 0)),
        out_shape=jax.ShapeDtypeStruct((batch, seq, d_pool + d_lru), BF16),
        scratch_shapes=[
            pltpu.VMEM((HALO + ts, d_pool), F32),
            pltpu.VMEM((HALO + ts, d_lru), F32),
            pltpu.VMEM((8, d_lru), F32),
        ],
        compiler_params=_params(("arbitrary", "arbitrary")),
        name="even_mix",
    )(z3d, w_pool, pool_scale, conv_w, conv_b, w_a, b_a, w_x, b_x, lam)


def _proj_ln_kernel(a_ref, w_ref, x_ref, mod_ref, g_ref, b_ref, o_ref, *, alpha, gate_row):
    y = jnp.dot(a_ref[...], w_ref[...], preferred_element_type=F32)
    gate = mod_ref[gate_row:gate_row + 1, :]
    r = alpha * x_ref[...] + (1.0 + gate) * y
    o_ref[...] = _layer_norm(r, g_ref[...], b_ref[...])


def proj_ln(a2d, w_bf16, x2d, mod_l, ln_g, ln_b, *, seq, alpha, gate_row, tm=512):
    n, d = x2d.shape
    k = a2d.shape[1]
    tm = min(tm, seq)
    blocks_per_batch = seq // tm
    kern = functools.partial(_proj_ln_kernel, alpha=alpha, gate_row=gate_row)
    return pl.pallas_call(
        kern,
        grid=(n // tm,),
        in_specs=[
            pl.BlockSpec((tm, k), lambda i: (i, 0)),
            _resident((k, d), lambda i: (0, 0)),
            pl.BlockSpec((tm, d), lambda i: (i, 0)),
            pl.BlockSpec((None, 6, d), lambda i: (i // blocks_per_batch, 0, 0)),
            _resident((1, d), lambda i: (0, 0)),
            _resident((1, d), lambda i: (0, 0)),
        ],
        out_specs=pl.BlockSpec((tm, d), lambda i: (i, 0)),
        out_shape=jax.ShapeDtypeStruct((n, d), F32),
        compiler_params=_params(("arbitrary",)),
        name="proj_ln",
    )(a2d, w_bf16, x2d, mod_l, ln_g, ln_b)


def _top16_rows(s, kidx, sentinel):
    t = s.shape[1]
    r16 = lax.broadcasted_iota(I32, (PEER_TOPK, t), 0)
    vals = jnp.zeros((PEER_TOPK, t), F32)
    idxs = jnp.zeros((PEER_TOPK, t), I32)
    for r in range(PEER_TOPK):
        m = jnp.max(s, axis=0, keepdims=True)
        am = jnp.min(jnp.where(s == m, kidx, sentinel), axis=0, keepdims=True)
        s = jnp.where(kidx == am, NEG_INF, s)
        vals = jnp.where(r16 == r, m, vals)
        idxs = jnp.where(r16 == r, am, idxs)
    return vals, idxs


def _take_rows16(table, sel):
    out = jnp.zeros(sel.shape, table.dtype)
    for s in range(PEER_TOPK):
        out = jnp.where(sel == s, table[s:s + 1, :], out)
    return out


SUBLANES = 8
BUILD_UNROLL = 32


def _peer_route_kernel(q_ref, keys_ref, g_ref,
                       gs_ref, sc_ref, val_ref, idx_ref, e_ref, w_ref, et_ref, wt_ref, *, tr):
    nhp = 2 * PEER_HEADS
    for hp in range(nhp):
        sc_ref[hp] = _dot_nt(keys_ref[hp], q_ref[:, hp * N_KEYS:(hp + 1) * N_KEYS])

    kidx = lax.broadcasted_iota(I32, (N_KEYS, tr), 0)

    def first_topk(hd, carry):
        for half in range(2):
            hp = 2 * hd + half
            vals, idxs = _top16_rows(sc_ref[hp], kidx, N_KEYS)
            val_ref[hp] = vals
            idx_ref[hp] = idxs
        return carry

    lax.fori_loop(0, PEER_HEADS, first_topk, 0)

    i16 = lax.broadcasted_iota(I32, (16, tr), 0)
    i8 = lax.broadcasted_iota(I32, (8, tr), 0)
    flat = jnp.concatenate([
        i16 * 16, i8 * 16 + 1, i8 * 16 + 2, i8 * 16 + 3,
        i16, 16 + i8, 32 + i8,
    ], axis=0)
    valid = jnp.concatenate([
        i16 < 16, i8 < 8, i8 < 5, i8 < 4,
        i16 >= 4, i8 >= 4, i8 == 4,
    ], axis=0)

    def second_topk(hd, carry):
        s1 = val_ref[2 * hd]
        s2 = val_ref[2 * hd + 1]
        cand = jnp.concatenate([
            s1 + s2[0:1, :], s1[0:8, :] + s2[1:2, :], s1[0:8, :] + s2[2:3, :], s1[0:8, :] + s2[3:4, :],
            s1[0:1, :] + s2, s1[1:2, :] + s2[0:8, :], s1[2:3, :] + s2[0:8, :],
        ], axis=0)
        cand = jnp.where(valid, cand, NEG_INF)
        top, fsel = _top16_rows(cand, flat, 256)
        e = jnp.exp(top - top[0:1, :])
        gate = e / jnp.sum(e, axis=0, keepdims=True)
        a = _take_rows16(idx_ref[2 * hd], fsel >> 4)
        b = _take_rows16(idx_ref[2 * hd + 1], fsel & 15)
        row0 = pl.multiple_of(hd * PEER_TOPK, PEER_TOPK)
        e_ref[pl.ds(row0, PEER_TOPK), :] = a * N_KEYS + b
        w_ref[pl.ds(row0, PEER_TOPK), :] = gate
        return carry

    lax.fori_loop(0, PEER_HEADS, second_topk, 0)

    n_act = PEER_HEADS * PEER_TOPK
    for c in range(tr // n_act):
        et_ref[c * n_act:(c + 1) * n_act, :] = e_ref[:, c * n_act:(c + 1) * n_act].T
        wt_ref[c * n_act:(c + 1) * n_act, :] = w_ref[:, c * n_act:(c + 1) * n_act].T

    iota_rows = lax.broadcasted_iota(I32, (N_KEYS, n_act), 0)
    sub_iota = lax.broadcasted_iota(I32, (SUBLANES, N_KEYS), 0)

    def build(step, carry):
        base = pl.multiple_of(step * BUILD_UNROLL, BUILD_UNROLL)
        erows = et_ref[pl.ds(base, BUILD_UNROLL), :]
        wrows = wt_ref[pl.ds(base, BUILD_UNROLL), :]
        for u in range(BUILD_UNROLL):
            erow = erows[u:u + 1, :]
            wrow = wrows[u:u + 1, :]
            pt = jnp.where(iota_rows == (erow >> 7), wrow, 0.0).astype(BF16)
            qt = jnp.where(iota_rows == (erow & (N_KEYS - 1)), 1.0, 0.0).astype(BF16)
            gs_ref[u * N_KEYS:(u + 1) * N_KEYS, :] = _dot_nt(pt, qt)
        for gi in range(BUILD_UNROLL // SUBLANES):
            grp = step * (BUILD_UNROLL // SUBLANES) + gi
            for k in range(N_KEYS // SUBLANES):
                tiles = []
                for t in range(SUBLANES):
                    row0 = (gi * SUBLANES + t) * N_KEYS + k * SUBLANES
                    tiles.append(gs_ref[row0:row0 + SUBLANES, :])
                for dist in (4, 2, 1):
                    take_hi = (sub_iota & dist) != 0
                    for t in range(SUBLANES):
                        if t & dist:
                            continue
                        lo, hi = tiles[t], tiles[t + dist]
                        tiles[t] = jnp.where(take_hi, pltpu.roll(hi, dist, 0), lo)
                        tiles[t + dist] = jnp.where(take_hi, hi, pltpu.roll(lo, SUBLANES - dist, 0))
                for s in range(SUBLANES):
                    g_ref[grp, k * SUBLANES + s] = tiles[s]
        return carry

    lax.fori_loop(0, tr // BUILD_UNROLL, build, 0)


def peer_route(q2d, keys_bf16, *, seq, tr=256):
    n, d = q2d.shape
    tr = min(tr, seq)
    nhp = 2 * PEER_HEADS
    n_act = PEER_HEADS * PEER_TOPK
    kern = functools.partial(_peer_route_kernel, tr=tr)
    return pl.pallas_call(
        kern,
        grid=(n // tr,),
        in_specs=[
            pl.BlockSpec((tr, d), lambda i: (i, 0)),
            _resident(keys_bf16.shape, lambda i: (0, 0, 0)),
        ],
        out_specs=pl.BlockSpec((tr // SUBLANES, N_KEYS, SUBLANES, N_KEYS), lambda i: (i, 0, 0, 0)),
        out_shape=jax.ShapeDtypeStruct((n // SUBLANES, N_KEYS, SUBLANES, N_KEYS), F32),
        scratch_shapes=[
            pltpu.VMEM((BUILD_UNROLL * N_KEYS, N_KEYS), F32),
            pltpu.VMEM((nhp, N_KEYS, tr), F32),
            pltpu.VMEM((nhp, PEER_TOPK, tr), F32),
            pltpu.VMEM((nhp, PEER_TOPK, tr), I32),
            pltpu.VMEM((n_act, tr), I32),
            pltpu.VMEM((n_act, tr), F32),
            pltpu.VMEM((tr, n_act), I32),
            pltpu.VMEM((tr, n_act), F32),
        ],
        compiler_params=_params(("arbitrary",)),
        name="peer_route",
    )(q2d, keys_bf16)


DENSE_CHUNK = 256


def _peer_dense_kernel(h_ref, g_ref, u_ref, v_ref, o_ref, act_ref):
    j = pl.program_id(1)
    rows = g_ref.shape[1]
    tm = h_ref.shape[0]

    @pl.when(j == 0)
    def _():
        o_ref[...] = jnp.zeros_like(o_ref)

    per = DENSE_CHUNK // N_KEYS
    for c in range(rows // per):
        ccols = slice(c * DENSE_CHUNK, (c + 1) * DENSE_CHUNK)
        z = _dot_nt(h_ref[...], u_ref[ccols, :])
        for r in range(per):
            row = c * per + r
            gate = g_ref[:, row, :, :].reshape(tm, N_KEYS)
            act_ref[:, row * N_KEYS:(row + 1) * N_KEYS] = (
                _gelu(z[:, r * N_KEYS:(r + 1) * N_KEYS]) * gate).astype(BF16)
    half = act_ref.shape[1] // 2
    o_ref[...] += jnp.dot(act_ref[:, :half], v_ref[:half, :], preferred_element_type=F32)
    o_ref[...] += jnp.dot(act_ref[:, half:], v_ref[half:, :], preferred_element_type=F32)


def peer_dense(h2d, g3d, u_bf16, v_bf16, *, layer, seq, tm=1024, te=1024):
    n, d = h2d.shape
    n_exp = v_bf16.shape[1]
    tm = min(tm, seq)
    return pl.pallas_call(
        _peer_dense_kernel,
        grid=(n // tm, n_exp // te),
        in_specs=[
            pl.BlockSpec((tm, d), lambda i, j: (i, 0), pipeline_mode=pl.Buffered(1)),
            pl.BlockSpec((tm // SUBLANES, te // N_KEYS, SUBLANES, N_KEYS), lambda i, j: (i, j, 0, 0)),
            pl.BlockSpec((None, te, d), lambda i, j: (layer, j, 0)),
            pl.BlockSpec((None, te, d), lambda i, j: (layer, j, 0)),
        ],
        out_specs=pl.BlockSpec((tm, d), lambda i, j: (i, 0)),
        out_shape=jax.ShapeDtypeStruct((n, d), F32),
        scratch_shapes=[pltpu.VMEM((tm, te), BF16)],
        compiler_params=_params(("arbitrary", "arbitrary")),
        name="peer_dense",
    )(h2d, g3d, u_bf16, v_bf16)


def _residual_ln_kernel(x_ref, y_ref, mod_ref, g_ref, b_ref, o_ref, *, alpha, gate_row):
    gate = mod_ref[gate_row:gate_row + 1, :]
    r = alpha * x_ref[...] + (1.0 + gate) * y_ref[...]
    o_ref[...] = _layer_norm(r, g_ref[...], b_ref[...])


def residual_ln(x2d, y2d, mod_l, ln_g, ln_b, *, seq, alpha, gate_row, tm=512):
    n, d = x2d.shape
    tm = min(tm, seq)
    blocks_per_batch = seq // tm
    kern = functools.partial(_residual_ln_kernel, alpha=alpha, gate_row=gate_row)
    return pl.pallas_call(
        kern,
        grid=(n // tm,),
        in_specs=[
            pl.BlockSpec((tm, d), lambda i: (i, 0)),
            pl.BlockSpec((tm, d), lambda i: (i, 0)),
            pl.BlockSpec((None, 6, d), lambda i: (i // blocks_per_batch, 0, 0)),
            _resident((1, d), lambda i: (0, 0)),
            _resident((1, d), lambda i: (0, 0)),
        ],
        out_specs=pl.BlockSpec((tm, d), lambda i: (i, 0)),
        out_shape=jax.ShapeDtypeStruct((n, d), F32),
        compiler_params=_params(("arbitrary",)),
        name="residual_ln",
    )(x2d, y2d, mod_l, ln_g, ln_b)


def kernel(x, c, ada_w, ada_b, ln_g, ln_b, peer_wq, peer_keys, peer_u, peer_v, ev_w_in, ev_w_pool,
           ev_pool_scale, ev_conv_w, ev_conv_b, ev_w_a, ev_b_a, ev_w_x, ev_b_x, ev_lam, ev_w_out,
           od_w_in, od_b_f, od_w_out):
    batch, seq, d = x.shape
    depth = ada_w.shape[0]
    n = batch * seq
    alpha = (2.0 * depth) ** 0.25
    d_mix = od_w_out.shape[1]

    c_pad = jnp.zeros((8, d), F32).at[:batch].set(c)
    mod = ada_mod(c_pad, ada_w, ada_b)[:, :batch].reshape(depth, batch, 6, d)

    n_exp = peer_u.shape[1]
    u_bf = cast_bf16(peer_u.reshape(depth * n_exp, d)).reshape(depth, n_exp, d)
    v_bf = cast_bf16(peer_v.reshape(depth * n_exp, d)).reshape(depth, n_exp, d)

    xf = x.reshape(n, d)
    for l in range(depth):
        mod_l = mod[l]
        g0 = ln_g[l, 0].reshape(1, d)
        b0 = ln_b[l, 0].reshape(1, d)
        g1 = ln_g[l, 1].reshape(1, d)
        b1 = ln_b[l, 1].reshape(1, d)
        if l % 2 == 0:
            e = l // 2
            z = mod_matmul(xf, mod_l, ev_w_in[e].astype(BF16), seq=seq, shift_row=0, scale_row=1,
                           out_dtype=F32)
            mixed = even_mix(
                z.reshape(batch, seq, -1), ev_w_pool[e].astype(BF16), ev_pool_scale[e].reshape(1, -1),
                ev_conv_w[e], ev_conv_b[e].reshape(1, -1), ev_w_a[e].astype(BF16),
                ev_b_a[e].reshape(1, -1), ev_w_x[e].astype(BF16), ev_b_x[e].reshape(1, -1),
                ev_lam[e].reshape(1, -1))
            xf = proj_ln(mixed.reshape(n, -1), ev_w_out[e].astype(BF16), xf, mod_l, g0, b0,
                         seq=seq, alpha=alpha, gate_row=2)
        else:
            o = l // 2
            w_in = od_w_in[o]
            w_f = jnp.zeros((d, 128), F32).at[:, :N_HEADS_ATTN].set(w_in[:, 3 * d_mix:])
            wf_hi = w_f.astype(BF16)
            wf_lo = (w_f - wf_hi.astype(F32)).astype(BF16)
            b_f = jnp.zeros((1, 128), F32).at[0, :N_HEADS_ATTN].set(od_b_f[o])
            w_qv = jnp.concatenate([w_in[:, :d_mix], w_in[:, 2 * d_mix:3 * d_mix]], axis=1).astype(BF16)
            w_kt = w_in[:, d_mix:2 * d_mix].T.astype(BF16)
            qv, k_t, f_cum = odd_in(xf, mod_l, w_qv, w_kt, wf_hi, wf_lo, b_f, seq=seq)
            f_keys = f_cum[:, :N_HEADS_ATTN].reshape(batch, seq, N_HEADS_ATTN)
            f_keys = jnp.transpose(f_keys, (0, 2, 1)).reshape(batch * N_HEADS_ATTN, 1, seq)
            attn = attention(qv, k_t, f_keys, batch=batch, seq=seq)
            xf = proj_ln(attn, od_w_out[o].astype(BF16), xf, mod_l, g0, b0,
                         seq=seq, alpha=alpha, gate_row=2)

        keys = peer_keys[l].reshape(2 * PEER_HEADS, N_KEYS, -1).astype(BF16)
        q, h2 = mod_matmul(xf, mod_l, peer_wq[l].astype(BF16), seq=seq, shift_row=3, scale_row=4,
                           out_dtype=BF16, emit_h=True)
        gmat = peer_route(q, keys, seq=seq)
        y = peer_dense(h2, gmat, u_bf, v_bf, layer=l, seq=seq)
        xf = residual_ln(xf, y, mod_l, g1, b1, seq=seq, alpha=alpha, gate_row=5)
    return xf.reshape(batch, seq, d)
```

```python
import functools
import math

import jax
import jax.numpy as jnp
from jax import lax
from jax.experimental import pallas as pl
from jax.experimental.pallas import tpu as pltpu

F32 = jnp.float32
BF16 = jnp.bfloat16
I32 = jnp.int32

LN_EPS = 1e-5
POOL_WINDOWS = (2, 4, 8, 16)
CONV_WIDTH = 4
LRU_BLOCKS = 8
LRU_C = 8.0
N_HEADS_ATTN = 16
PEER_HEADS = 8
PEER_TOPK = 16
N_KEYS = 128

V7X_VMEM_BYTES = 64 * 1024 * 1024
VMEM_LIMIT = 56 * 1024 * 1024
NEG_INF = float("-inf")
LOG2E = math.log2(math.e)


def _params(sem):
    return pltpu.CompilerParams(dimension_semantics=sem, vmem_limit_bytes=VMEM_LIMIT)


def _resident(shape, index_map):
    return pl.BlockSpec(shape, index_map, pipeline_mode=pl.Buffered(1))


def _gelu(x):
    c = math.sqrt(2.0 / math.pi)
    return 0.5 * x * (1.0 + jnp.tanh(c * (x + 0.044715 * (x * x * x))))


def _log_sigmoid(x):
    return jnp.minimum(x, 0.0) - jnp.log1p(jnp.exp(-jnp.abs(x)))


def _sigmoid(x):
    return 0.5 * jnp.tanh(0.5 * x) + 0.5


def _layer_norm(r, g, b):
    mu = jnp.mean(r, axis=-1, keepdims=True)
    d = r - mu
    var = jnp.mean(d * d, axis=-1, keepdims=True)
    return d * lax.rsqrt(var + LN_EPS) * g + b


def _dot_nt(a, b):
    return lax.dot_general(a, b, (((1,), (1,)), ((), ())), preferred_element_type=F32)


def _cast_kernel(w_ref, o_ref):
    o_ref[...] = w_ref[...].astype(o_ref.dtype)


def cast_bf16(w2d, *, tm=1024):
    rows, cols = w2d.shape
    return pl.pallas_call(
        _cast_kernel,
        grid=(rows // tm,),
        in_specs=[pl.BlockSpec((tm, cols), lambda i: (i, 0))],
        out_specs=pl.BlockSpec((tm, cols), lambda i: (i, 0)),
        out_shape=jax.ShapeDtypeStruct((rows, cols), BF16),
        compiler_params=_params(("arbitrary",)),
        name="cast_bf16",
    )(w2d)


def _ada_kernel(c_ref, w_ref, b_ref, o_ref):
    c = c_ref[...]
    ca = c * _sigmoid(c)
    o_ref[...] = jnp.dot(ca, w_ref[...], preferred_element_type=F32) + b_ref[...]


def ada_mod(c_pad, ada_w, ada_b):
    depth, d, n6 = ada_w.shape
    rows = c_pad.shape[0]
    tn = 1024
    return pl.pallas_call(
        _ada_kernel,
        grid=(depth, n6 // tn),
        in_specs=[
            pl.BlockSpec((rows, d), lambda l, j: (0, 0)),
            pl.BlockSpec((None, d, tn), lambda l, j: (l, 0, j)),
            pl.BlockSpec((None, 1, tn), lambda l, j: (l, 0, j)),
        ],
        out_specs=pl.BlockSpec((None, rows, tn), lambda l, j: (l, 0, j)),
        out_shape=jax.ShapeDtypeStruct((depth, rows, n6), F32),
        compiler_params=_params(("arbitrary", "arbitrary")),
        name="ada_mod",
    )(c_pad, ada_w, ada_b.reshape(depth, 1, n6))


def _mod_matmul_kernel(x_ref, mod_ref, w_ref, o_ref, h_ref, *, shift_row, scale_row):
    @pl.when(pl.program_id(1) == 0)
    def _():
        sh = mod_ref[shift_row:shift_row + 1, :]
        sc = mod_ref[scale_row:scale_row + 1, :]
        h_ref[...] = (x_ref[...] * (1.0 + sc) + sh).astype(BF16)

    o_ref[...] = jnp.dot(h_ref[...], w_ref[...], preferred_element_type=F32).astype(o_ref.dtype)


def mod_matmul(x2d, mod_l, w_bf16, *, seq, shift_row, scale_row, out_dtype, emit_h=False,
               tm=1024, tn=1024):
    n, d = x2d.shape
    nout = w_bf16.shape[1]
    tm = min(tm, seq)
    blocks_per_batch = seq // tm
    kern = functools.partial(_mod_matmul_kernel, shift_row=shift_row, scale_row=scale_row)
    out_specs = [pl.BlockSpec((tm, tn), lambda i, j: (i, j))]
    out_shape = [jax.ShapeDtypeStruct((n, nout), out_dtype)]
    scratch = []
    if emit_h:
        out_specs.append(pl.BlockSpec((tm, d), lambda i, j: (i, 0)))
        out_shape.append(jax.ShapeDtypeStruct((n, d), BF16))
    else:
        scratch.append(pltpu.VMEM((tm, d), BF16))
    res = pl.pallas_call(
        kern,
        grid=(n // tm, nout // tn),
        in_specs=[
            pl.BlockSpec((tm, d), lambda i, j: (i, 0)),
            pl.BlockSpec((None, 6, d), lambda i, j: (i // blocks_per_batch, 0, 0)),
            pl.BlockSpec((d, tn), lambda i, j: (0, j)),
        ],
        out_specs=out_specs,
        out_shape=out_shape,
        scratch_shapes=scratch,
        compiler_params=_params(("arbitrary", "arbitrary")),
        name="mod_matmul",
    )(x2d, mod_l, w_bf16)
    return res if emit_h else res[0]


def _cumsum_rows(x):
    rows = x.shape[0]
    ridx = lax.broadcasted_iota(I32, x.shape, 0)
    d = 1
    while d < rows:
        x = x + jnp.where(ridx >= d, pltpu.roll(x, d, 0), 0.0)
        d *= 2
    return x


def _odd_in_kernel(x_ref, mod_ref, w_ref, wkt_ref, wfh_ref, wfl_ref, bf_ref, o_ref, kt_ref, f_ref,
                   h_ref, carry_ref, *, blocks_per_batch, nb, q_scale):
    i = pl.program_id(0)
    j = pl.program_id(1)

    @pl.when(j == 0)
    def _():
        sh = mod_ref[0:1, :]
        sc = mod_ref[1:2, :]
        h = x_ref[...] * (1.0 + sc) + sh
        h_hi = h.astype(BF16)
        h_ref[...] = h_hi
        h_lo = (h - h_hi.astype(F32)).astype(BF16)
        zf = (jnp.dot(h_hi, wfh_ref[...], preferred_element_type=F32)
              + jnp.dot(h_lo, wfh_ref[...], preferred_element_type=F32)
              + jnp.dot(h_hi, wfl_ref[...], preferred_element_type=F32))
        logf = _log_sigmoid(zf + bf_ref[...])

        @pl.when(i % blocks_per_batch == 0)
        def _():
            carry_ref[...] = jnp.zeros_like(carry_ref)

        cs = _cumsum_rows(logf) + carry_ref[0:1, :]
        f_ref[...] = cs
        carry_ref[...] = jnp.broadcast_to(cs[cs.shape[0] - 1:, :], carry_ref.shape)

    is_key = jnp.logical_and(j >= nb, j < 2 * nb)

    @pl.when(jnp.logical_not(is_key))
    def _():
        z = jnp.dot(h_ref[...], w_ref[...], preferred_element_type=F32)
        scale = jnp.where(j < nb, q_scale, 1.0)
        o_ref[...] = (z * scale).astype(o_ref.dtype)

    @pl.when(is_key)
    def _():
        kt_ref[...] = _dot_nt(wkt_ref[...], h_ref[...]).astype(kt_ref.dtype)


def odd_in(x2d, mod_l, w_qv, w_kt, wf_hi, wf_lo, b_f_pad, *, seq, tm=1024, tn=1024):
    n, d = x2d.shape
    d_mix = w_kt.shape[0]
    tm = min(tm, seq)
    blocks_per_batch = seq // tm
    head_dim = d_mix // N_HEADS_ATTN
    nb = d_mix // tn
    kern = functools.partial(_odd_in_kernel, blocks_per_batch=blocks_per_batch, nb=nb,
                             q_scale=head_dim ** -0.5 * LOG2E)

    def qv_block(j):
        return jnp.where(j < nb, j, jnp.where(j < 2 * nb, nb - 1, j - nb))

    def k_block(j):
        return jnp.clip(j - nb, 0, nb - 1)

    return pl.pallas_call(
        kern,
        grid=(n // tm, 3 * nb),
        in_specs=[
            pl.BlockSpec((tm, d), lambda i, j: (i, 0)),
            pl.BlockSpec((None, 6, d), lambda i, j: (i // blocks_per_batch, 0, 0)),
            pl.BlockSpec((d, tn), lambda i, j: (0, qv_block(j))),
            pl.BlockSpec((tn, d), lambda i, j: (k_block(j), 0)),
            _resident((d, 128), lambda i, j: (0, 0)),
            _resident((d, 128), lambda i, j: (0, 0)),
            _resident((1, 128), lambda i, j: (0, 0)),
        ],
        out_specs=[
            pl.BlockSpec((tm, tn), lambda i, j: (i, qv_block(j))),
            pl.BlockSpec((tn, tm), lambda i, j: (k_block(j), i)),
            pl.BlockSpec((tm, 128), lambda i, j: (i, 0)),
        ],
        out_shape=[
            jax.ShapeDtypeStruct((n, 2 * d_mix), BF16),
            jax.ShapeDtypeStruct((d_mix, n), BF16),
            jax.ShapeDtypeStruct((n, 128), F32),
        ],
        scratch_shapes=[pltpu.VMEM((tm, d), BF16), pltpu.VMEM((8, 128), F32)],
        compiler_params=_params(("arbitrary", "arbitrary")),
        name="odd_in",
    )(x2d, mod_l, w_qv, w_kt, wf_hi, wf_lo, b_f_pad)


def _attn_kernel(q_ref, kt_ref, v_ref, fk_ref, o_ref, sa_ref, sb_ref, vaug_ref, pb_ref, *, tq, tk):
    qi = pl.program_id(2)
    f_ref0 = fk_ref[0:1, pl.ds(pl.multiple_of(qi * tq, tq), tk)][:, 0:1]
    dh = q_ref.shape[1]

    @pl.when(qi == 0)
    def _():
        vaug_ref[:, 0:dh] = v_ref[...]
        vaug_ref[:, dh:2 * dh] = jnp.ones((v_ref.shape[0], dh), BF16)
    per_q = tq // tk
    n_full = qi * per_q

    def scores(c, s_ref):
        start = pl.multiple_of(c * tk, tk)
        s = jnp.dot(q_ref[...], kt_ref[:, pl.ds(start, tk)], preferred_element_type=F32)
        s_ref[...] = s + (f_ref0 - fk_ref[0:1, pl.ds(start, tk)]) * LOG2E

    def values(c, p):
        start = pl.multiple_of(c * tk, tk)
        return jnp.dot(p, vaug_ref[pl.ds(start, tk), :], preferred_element_type=F32)

    def softmax_step(s_ref, m, diag_offset=None):
        if diag_offset is not None:
            row = lax.broadcasted_iota(I32, (tq, tk), 0)
            col = lax.broadcasted_iota(I32, (tq, tk), 1) + diag_offset
            s_ref[...] = jnp.where(col <= row, s_ref[...], NEG_INF)
        m_new = jnp.maximum(m, jnp.max(s_ref[...], axis=-1, keepdims=True))
        alpha = jnp.exp2(m - m_new)
        p = jnp.exp2(s_ref[...] - m_new).astype(BF16)
        return m_new, alpha, p

    def update(c, s_ref, carry, diag_offset=None):
        m, acc = carry
        m, alpha, p = softmax_step(s_ref, m, diag_offset)
        return m, alpha * acc + values(c, p)

    pb_ref[...] = jnp.zeros(pb_ref.shape, BF16)
    scores(0, sa_ref)

    def group(c0, count, carry):
        m, acc = carry
        acc = acc + values(jnp.maximum(c0 - 1, 0), pb_ref[...])
        bufs = (sa_ref, sb_ref)
        for t in range(count):
            scores(c0 + t + 1, bufs[(t + 1) % 2])
            m, alpha, p = softmax_step(bufs[t % 2], m)
            if t + 1 < count:
                acc = alpha * acc + values(c0 + t, p)
            else:
                pb_ref[...] = p
                acc = alpha * acc
        return m, acc

    init = (jnp.full((tq, 1), NEG_INF, F32), jnp.zeros((tq, 2 * dh), F32))
    carry = init
    done = 0
    for count in (8, 4, 2):
        steps = (n_full - done) // count
        carry = lax.fori_loop(
            0, steps, lambda p, cr, base=done, count=count: group(base + count * p, count, cr), carry)
        done = done + count * steps
    pending = jnp.maximum(done - 1, 0)

    def flush(carry):
        m, acc = carry
        return m, acc + values(pending, pb_ref[...])

    def diagonal(carry, bufs):
        for dchunk in range(per_q):
            if dchunk + 1 < per_q:
                scores(n_full + dchunk + 1, bufs[(dchunk + 1) % 2])
            carry = update(n_full + dchunk, bufs[dchunk % 2], carry, diag_offset=dchunk * tk)
        return carry

    def even_tail(carry):
        return diagonal(flush(carry), (sa_ref, sb_ref))

    def odd_tail(carry):
        scores(n_full, sb_ref)
        carry = update(n_full - 1, sa_ref, flush(carry))
        return diagonal(carry, (sb_ref, sa_ref))

    if per_q % 2 == 0:
        m, acc = even_tail(carry)
    else:
        m, acc = lax.cond(n_full % 2 == 1, odd_tail, even_tail, carry)
    o_ref[...] = (acc[:, 0:dh] / acc[:, dh:dh + 1]).astype(o_ref.dtype)


def attention(qv, k_t, f_keys, *, batch, seq, tq=512, tk=512):
    n = qv.shape[0]
    d_mix = qv.shape[1] // 2
    heads = N_HEADS_ATTN
    dh = d_mix // heads
    tq = min(tq, seq)
    tk = min(tk, tq)
    nq = seq // tq
    kern = functools.partial(_attn_kernel, tq=tq, tk=tk)
    return pl.pallas_call(
        kern,
        grid=(batch, heads, nq),
        in_specs=[
            pl.BlockSpec((tq, dh), lambda b, h, i: (b * nq + i, h)),
            pl.BlockSpec((dh, seq), lambda b, h, i: (h, b)),
            pl.BlockSpec((seq, dh), lambda b, h, i: (b, heads + h)),
            pl.BlockSpec((None, 1, seq), lambda b, h, i: (b * heads + h, 0, 0)),
        ],
        out_specs=pl.BlockSpec((tq, dh), lambda b, h, i: (b * nq + i, h)),
        out_shape=jax.ShapeDtypeStruct((n, d_mix), BF16),
        scratch_shapes=[pltpu.VMEM((tq, tk), F32), pltpu.VMEM((tq, tk), F32),
                        pltpu.VMEM((seq, 2 * dh), BF16), pltpu.VMEM((tq, tk), BF16)],
        compiler_params=_params(("arbitrary", "arbitrary", "arbitrary")),
        name="attention",
    )(qv, k_t, qv, f_keys)


HALO = 16


def _even_mix_kernel(z_ref, wpool_ref, pscale_ref, convw_ref, convb_ref, wa_ref, ba_ref,
                     wx_ref, bx_ref, lam_ref, o_ref, pbuf, cbuf, hstate, *, ts, d_pool, d_lru):
    sb = pl.program_id(1)
    group = d_pool // len(POOL_WINDOWS)
    blk = d_lru // LRU_BLOCKS

    @pl.when(sb == 0)
    def _():
        pbuf[0:HALO, :] = jnp.zeros((HALO, d_pool), F32)
        cbuf[0:HALO, :] = jnp.zeros((HALO, d_lru), F32)
        hstate[...] = jnp.zeros_like(hstate)

    up = z_ref[:, 0:d_pool]
    pbuf[HALO:HALO + ts, :] = up
    pos = sb * ts + lax.broadcasted_iota(I32, (ts, group), 0)
    ya = []
    for g, w in enumerate(POOL_WINDOWS):
        lo, hi = g * group, (g + 1) * group
        need = w - 1
        cur = pbuf[HALO - need:HALO + ts, lo:hi]
        span = 1
        while span < w:
            rows = cur.shape[0] - span
            cur = cur[span:span + rows, :] + cur[0:rows, :]
            span *= 2
        cnt = jnp.minimum(pos + 1, w).astype(F32)
        pooled = cur / cnt - up[:, lo:hi]
        y = jnp.dot(pooled.astype(BF16), wpool_ref[g], preferred_element_type=F32)
        ya.append(y * pscale_ref[0:1, lo:hi])
    o_ref[:, 0:d_pool] = jnp.concatenate(ya, axis=-1).astype(o_ref.dtype)
    pbuf[0:HALO, :] = pbuf[ts:ts + HALO, :]

    ul = z_ref[:, d_pool:d_pool + d_lru]
    ug = z_ref[:, d_pool + d_lru:d_pool + 2 * d_lru]
    cbuf[HALO:HALO + ts, :] = ul
    xc = jnp.broadcast_to(convb_ref[0:1, :], (ts, d_lru))
    for k in range(CONV_WIDTH):
        off = HALO - (CONV_WIDTH - 1) + k
        xc = xc + cbuf[off:off + ts, :] * convw_ref[k:k + 1, :]
    cbuf[0:HALO, :] = cbuf[ts:ts + HALO, :]

    xb = xc.astype(BF16)
    ra, ia = [], []
    for hb in range(LRU_BLOCKS):
        xs = xb[:, hb * blk:(hb + 1) * blk]
        ra.append(jnp.dot(xs, wa_ref[hb], preferred_element_type=F32))
        ia.append(jnp.dot(xs, wx_ref[hb], preferred_element_type=F32))
    r_gate = _sigmoid(jnp.concatenate(ra, axis=-1) + ba_ref[0:1, :])
    i_gate = _sigmoid(jnp.concatenate(ia, axis=-1) + bx_ref[0:1, :])
    log_a = LRU_C * r_gate * _log_sigmoid(lam_ref[0:1, :])
    a = jnp.exp(log_a)
    m2 = -jnp.tanh(log_a) * (1.0 + a * a)
    mult = jnp.where(m2 > 0.0, m2 * lax.rsqrt(m2), 0.0)
    b = mult * i_gate * xc

    n_grp = ts // SUBLANES
    a = a.reshape(n_grp, SUBLANES, d_lru)
    b = b.reshape(n_grp, SUBLANES, d_lru)
    rsub = lax.broadcasted_iota(I32, (n_grp, SUBLANES, d_lru), 1)
    d = 1
    while d < SUBLANES:
        keep = rsub >= d
        a_sh = jnp.where(keep, pltpu.roll(a, d, 1), 1.0)
        b_sh = jnp.where(keep, pltpu.roll(b, d, 1), 0.0)
        b = a * b_sh + b
        a = a * a_sh
        d *= 2
    state = hstate[0:1, :]
    groups = []
    for v in range(n_grp):
        hv = b[v] + a[v] * state
        state = hv[SUBLANES - 1:SUBLANES, :]
        groups.append(hv)
    h = jnp.concatenate(groups, axis=0)
    hstate[...] = jnp.broadcast_to(state, hstate.shape)
    o_ref[:, d_pool:d_pool + d_lru] = (h * _gelu(ug)).astype(o_ref.dtype)


def even_mix(z3d, w_pool, pool_scale, conv_w, conv_b, w_a, b_a, w_x, b_x, lam, *, ts=256):
    batch, seq, dz = z3d.shape
    d_pool = pool_scale.shape[-1]
    d_lru = lam.shape[-1]
    ts = min(ts, seq)
    group = d_pool // len(POOL_WINDOWS)
    blk = d_lru // LRU_BLOCKS
    kern = functools.partial(_even_mix_kernel, ts=ts, d_pool=d_pool, d_lru=d_lru)
    const2 = lambda b, s: (0, 0)
    const3 = lambda b, s: (0, 0, 0)
    return pl.pallas_call(
        kern,
        grid=(batch, seq // ts),
        in_specs=[
            pl.BlockSpec((None, ts, dz), lambda b, s: (b, s, 0)),
            _resident((len(POOL_WINDOWS), group, group), const3),
            _resident((1, d_pool), const2),
            _resident((CONV_WIDTH, d_lru), const2),
            _resident((1, d_lru), const2),
            _resident((LRU_BLOCKS, blk, blk), const3),
            _resident((1, d_lru), const2),
            _resident((LRU_BLOCKS, blk, blk), const3),
            _resident((1, d_lru), const2),
            _resident((1, d_lru), const2),
        ],
        out_specs=pl.BlockSpec((None, ts, d_pool + d_lru), lambda b, s: (b, s, 0)),
        out_shape=jax.ShapeDtypeStruct((batch, seq, d_pool + d_lru), BF16),
        scratch_shapes=[
            pltpu.VMEM((HALO + ts, d_pool), F32),
            pltpu.VMEM((HALO + ts, d_lru), F32),
            pltpu.VMEM((8, d_lru), F32),
        ],
        compiler_params=_params(("arbitrary", "arbitrary")),
        name="even_mix",
    )(z3d, w_pool, pool_scale, conv_w, conv_b, w_a, b_a, w_x, b_x, lam)


def _proj_ln_kernel(a_ref, w_ref, x_ref, mod_ref, g_ref, b_ref, o_ref, *, alpha, gate_row):
    y = jnp.dot(a_ref[...], w_ref[...], preferred_element_type=F32)
    gate = mod_ref[gate_row:gate_row + 1, :]
    r = alpha * x_ref[...] + (1.0 + gate) * y
    o_ref[...] = _layer_norm(r, g_ref[...], b_ref[...])


def proj_ln(a2d, w_bf16, x2d, mod_l, ln_g, ln_b, *, seq, alpha, gate_row, tm=512):
    n, d = x2d.shape
    k = a2d.shape[1]
    tm = min(tm, seq)
    blocks_per_batch = seq // tm
    kern = functools.partial(_proj_ln_kernel, alpha=alpha, gate_row=gate_row)
    return pl.pallas_call(
        kern,
        grid=(n // tm,),
        in_specs=[
            pl.BlockSpec((tm, k), lambda i: (i, 0)),
            _resident((k, d), lambda i: (0, 0)),
            pl.BlockSpec((tm, d), lambda i: (i, 0)),
            pl.BlockSpec((None, 6, d), lambda i: (i // blocks_per_batch, 0, 0)),
            _resident((1, d), lambda i: (0, 0)),
            _resident((1, d), lambda i: (0, 0)),
        ],
        out_specs=pl.BlockSpec((tm, d), lambda i: (i, 0)),
        out_shape=jax.ShapeDtypeStruct((n, d), F32),
        compiler_params=_params(("arbitrary",)),
        name="proj_ln",
    )(a2d, w_bf16, x2d, mod_l, ln_g, ln_b)


def _top16_rows(s, kidx, sentinel):
    t = s.shape[1]
    r16 = lax.broadcasted_iota(I32, (PEER_TOPK, t), 0)
    vals = jnp.zeros((PEER_TOPK, t), F32)
    idxs = jnp.zeros((PEER_TOPK, t), I32)
    for r in range(PEER_TOPK):
        m = jnp.max(s, axis=0, keepdims=True)
        am = jnp.min(jnp.where(s == m, kidx, sentinel), axis=0, keepdims=True)
        s = jnp.where(kidx == am, NEG_INF, s)
        vals = jnp.where(r16 == r, m, vals)
        idxs = jnp.where(r16 == r, am, idxs)
    return vals, idxs


def _take_rows16(table, sel):
    out = jnp.zeros(sel.shape, table.dtype)
    for s in range(PEER_TOPK):
        out = jnp.where(sel == s, table[s:s + 1, :], out)
    return out


SUBLANES = 8
BUILD_UNROLL = 32


def _peer_route_kernel(q_ref, keys_ref, g_ref,
                       gs_ref, sc_ref, val_ref, idx_ref, e_ref, w_ref, et_ref, wt_ref, *, tr):
    nhp = 2 * PEER_HEADS
    for hp in range(nhp):
        sc_ref[hp] = _dot_nt(keys_ref[hp], q_ref[:, hp * N_KEYS:(hp + 1) * N_KEYS])

    kidx = lax.broadcasted_iota(I32, (N_KEYS, tr), 0)

    def first_topk(hd, carry):
        for half in range(2):
            hp = 2 * hd + half
            vals, idxs = _top16_rows(sc_ref[hp], kidx, N_KEYS)
            val_ref[hp] = vals
            idx_ref[hp] = idxs
        return carry

    lax.fori_loop(0, PEER_HEADS, first_topk, 0)

    i16 = lax.broadcasted_iota(I32, (16, tr), 0)
    i8 = lax.broadcasted_iota(I32, (8, tr), 0)
    flat = jnp.concatenate([
        i16 * 16, i8 * 16 + 1, i8 * 16 + 2, i8 * 16 + 3,
        i16, 16 + i8, 32 + i8,
    ], axis=0)
    valid = jnp.concatenate([
        i16 < 16, i8 < 8, i8 < 5, i8 < 4,
        i16 >= 4, i8 >= 4, i8 == 4,
    ], axis=0)

    def second_topk(hd, carry):
        s1 = val_ref[2 * hd]
        s2 = val_ref[2 * hd + 1]
        cand = jnp.concatenate([
            s1 + s2[0:1, :], s1[0:8, :] + s2[1:2, :], s1[0:8, :] + s2[2:3, :], s1[0:8, :] + s2[3:4, :],
            s1[0:1, :] + s2, s1[1:2, :] + s2[0:8, :], s1[2:3, :] + s2[0:8, :],
        ], axis=0)
        cand = jnp.where(valid, cand, NEG_INF)
        top, fsel = _top16_rows(cand, flat, 256)
        e = jnp.exp(top - top[0:1, :])
        gate = e / jnp.sum(e, axis=0, keepdims=True)
        a = _take_rows16(idx_ref[2 * hd], fsel >> 4)
        b = _take_rows16(idx_ref[2 * hd + 1], fsel & 15)
        row0 = pl.multiple_of(hd * PEER_TOPK, PEER_TOPK)
        e_ref[pl.ds(row0, PEER_TOPK), :] = a * N_KEYS + b
        w_ref[pl.ds(row0, PEER_TOPK), :] = gate
        return carry

    lax.fori_loop(0, PEER_HEADS, second_topk, 0)

    n_act = PEER_HEADS * PEER_TOPK
    for c in range(tr // n_act):
        et_ref[c * n_act:(c + 1) * n_act, :] = e_ref[:, c * n_act:(c + 1) * n_act].T
        wt_ref[c * n_act:(c + 1) * n_act, :] = w_ref[:, c * n_act:(c + 1) * n_act].T

    iota_rows = lax.broadcasted_iota(I32, (N_KEYS, n_act), 0)
    sub_iota = lax.broadcasted_iota(I32, (SUBLANES, N_KEYS), 0)

    def build(step, carry):
        base = pl.multiple_of(step * BUILD_UNROLL, BUILD_UNROLL)
        erows = et_ref[pl.ds(base, BUILD_UNROLL), :]
        wrows = wt_ref[pl.ds(base, BUILD_UNROLL), :]
        for u in range(BUILD_UNROLL):
            erow = erows[u:u + 1, :]
            wrow = wrows[u:u + 1, :]
            pt = jnp.where(iota_rows == (erow >> 7), wrow, 0.0).astype(BF16)
            qt = jnp.where(iota_rows == (erow & (N_KEYS - 1)), 1.0, 0.0).astype(BF16)
            gs_ref[u * N_KEYS:(u + 1) * N_KEYS, :] = _dot_nt(pt, qt)
        for gi in range(BUILD_UNROLL // SUBLANES):
            grp = step * (BUILD_UNROLL // SUBLANES) + gi
            for k in range(N_KEYS // SUBLANES):
                tiles = []
                for t in range(SUBLANES):
                    row0 = (gi * SUBLANES + t) * N_KEYS + k * SUBLANES
                    tiles.append(gs_ref[row0:row0 + SUBLANES, :])
                for dist in (4, 2, 1):
                    take_hi = (sub_iota & dist) != 0
                    for t in range(SUBLANES):
                        if t & dist:
                            continue
                        lo, hi = tiles[t], tiles[t + dist]
                        tiles[t] = jnp.where(take_hi, pltpu.roll(hi, dist, 0), lo)
                        tiles[t + dist] = jnp.where(take_hi, hi, pltpu.roll(lo, SUBLANES - dist, 0))
                for s in range(SUBLANES):
                    g_ref[grp, k * SUBLANES + s] = tiles[s]
        return carry

    lax.fori_loop(0, tr // BUILD_UNROLL, build, 0)


def peer_route(q2d, keys_bf16, *, seq, tr=256):
    n, d = q2d.shape
    tr = min(tr, seq)
    nhp = 2 * PEER_HEADS
    n_act = PEER_HEADS * PEER_TOPK
    kern = functools.partial(_peer_route_kernel, tr=tr)
    return pl.pallas_call(
        kern,
        grid=(n // tr,),
        in_specs=[
            pl.BlockSpec((tr, d), lambda i: (i, 0)),
            _resident(keys_bf16.shape, lambda i: (0, 0, 0)),
        ],
        out_specs=pl.BlockSpec((tr // SUBLANES, N_KEYS, SUBLANES, N_KEYS), lambda i: (i, 0, 0, 0)),
        out_shape=jax.ShapeDtypeStruct((n // SUBLANES, N_KEYS, SUBLANES, N_KEYS), F32),
        scratch_shapes=[
            pltpu.VMEM((BUILD_UNROLL * N_KEYS, N_KEYS), F32),
            pltpu.VMEM((nhp, N_KEYS, tr), F32),
            pltpu.VMEM((nhp, PEER_TOPK, tr), F32),
            pltpu.VMEM((nhp, PEER_TOPK, tr), I32),
            pltpu.VMEM((n_act, tr), I32),
            pltpu.VMEM((n_act, tr), F32),
            pltpu.VMEM((tr, n_act), I32),
            pltpu.VMEM((tr, n_act), F32),
        ],
        compiler_params=_params(("arbitrary",)),
        name="peer_route",
    )(q2d, keys_bf16)


DENSE_CHUNK = 256


def _peer_dense_kernel(h_ref, g_ref, u_ref, v_ref, o_ref, act_ref):
    j = pl.program_id(1)
    rows = g_ref.shape[1]
    tm = h_ref.shape[0]

    @pl.when(j == 0)
    def _():
        o_ref[...] = jnp.zeros_like(o_ref)

    per = DENSE_CHUNK // N_KEYS
    for c in range(rows // per):
        ccols = slice(c * DENSE_CHUNK, (c + 1) * DENSE_CHUNK)
        z = _dot_nt(h_ref[...], u_ref[ccols, :])
        for r in range(per):
            row = c * per + r
            gate = g_ref[:, row, :, :].reshape(tm, N_KEYS)
            act_ref[:, row * N_KEYS:(row + 1) * N_KEYS] = (
                _gelu(z[:, r * N_KEYS:(r + 1) * N_KEYS]) * gate).astype(BF16)
    half = act_ref.shape[1] // 2
    o_ref[...] += jnp.dot(act_ref[:, :half], v_ref[:half, :], preferred_element_type=F32)
    o_ref[...] += jnp.dot(act_ref[:, half:], v_ref[half:, :], preferred_element_type=F32)


def peer_dense(h2d, g3d, u_bf16, v_bf16, *, layer, seq, tm=1024, te=1024):
    n, d = h2d.shape
    n_exp = v_bf16.shape[1]
    tm = min(tm, seq)
    return pl.pallas_call(
        _peer_dense_kernel,
        grid=(n // tm, n_exp // te),
        in_specs=[
            pl.BlockSpec((tm, d), lambda i, j: (i, 0), pipeline_mode=pl.Buffered(1)),
            pl.BlockSpec((tm // SUBLANES, te // N_KEYS, SUBLANES, N_KEYS), lambda i, j: (i, j, 0, 0)),
            pl.BlockSpec((None, te, d), lambda i, j: (layer, j, 0)),
            pl.BlockSpec((None, te, d), lambda i, j: (layer, j, 0)),
        ],
        out_specs=pl.BlockSpec((tm, d), lambda i, j: (i, 0)),
        out_shape=jax.ShapeDtypeStruct((n, d), F32),
        scratch_shapes=[pltpu.VMEM((tm, te), BF16)],
        compiler_params=_params(("arbitrary", "arbitrary")),
        name="peer_dense",
    )(h2d, g3d, u_bf16, v_bf16)


def _residual_ln_kernel(x_ref, y_ref, mod_ref, g_ref, b_ref, o_ref, *, alpha, gate_row):
    gate = mod_ref[gate_row:gate_row + 1, :]
    r = alpha * x_ref[...] + (1.0 + gate) * y_ref[...]
    o_ref[...] = _layer_norm(r, g_ref[...], b_ref[...])


def residual_ln(x2d, y2d, mod_l, ln_g, ln_b, *, seq, alpha, gate_row, tm=512):
    n, d = x2d.shape
    tm = min(tm, seq)
    blocks_per_batch = seq // tm
    kern = functools.partial(_residual_ln_kernel, alpha=alpha, gate_row=gate_row)
    return pl.pallas_call(
        kern,
        grid=(n // tm,),
        in_specs=[
            pl.BlockSpec((tm, d), lambda i: (i, 0)),
            pl.BlockSpec((tm, d), lambda i: (i, 0)),
            pl.BlockSpec((None, 6, d), lambda i: (i // blocks_per_batch, 0, 0)),
            _resident((1, d), lambda i: (0, 0)),
            _resident((1, d), lambda i: (0, 0)),
        ],
        out_specs=pl.BlockSpec((tm, d), lambda i: (i, 0)),
        out_shape=jax.ShapeDtypeStruct((n, d), F32),
        compiler_params=_params(("arbitrary",)),
        name="residual_ln",
    )(x2d, y2d, mod_l, ln_g, ln_b)


def kernel(x, c, ada_w, ada_b, ln_g, ln_b, peer_wq, peer_keys, peer_u, peer_v, ev_w_in, ev_w_pool,
           ev_pool_scale, ev_conv_w, ev_conv_b, ev_w_a, ev_b_a, ev_w_x, ev_b_x, ev_lam, ev_w_out,
           od_w_in, od_b_f, od_w_out):
    batch, seq, d = x.shape
    depth = ada_w.shape[0]
    n = batch * seq
    alpha = (2.0 * depth) ** 0.25
    d_mix = od_w_out.shape[1]

    c_pad = jnp.zeros((8, d), F32).at[:batch].set(c)
    mod = ada_mod(c_pad, ada_w, ada_b)[:, :batch].reshape(depth, batch, 6, d)

    n_exp = peer_u.shape[1]
    u_bf = cast_bf16(peer_u.reshape(depth * n_exp, d)).reshape(depth, n_exp, d)
    v_bf = cast_bf16(peer_v.reshape(depth * n_exp, d)).reshape(depth, n_exp, d)

    xf = x.reshape(n, d)
    for l in range(depth):
        mod_l = mod[l]
        g0 = ln_g[l, 0].reshape(1, d)
        b0 = ln_b[l, 0].reshape(1, d)
        g1 = ln_g[l, 1].reshape(1, d)
        b1 = ln_b[l, 1].reshape(1, d)
        if l % 2 == 0:
            e = l // 2
            z = mod_matmul(xf, mod_l, ev_w_in[e].astype(BF16), seq=seq, shift_row=0, scale_row=1,
                           out_dtype=F32)
            mixed = even_mix(
                z.reshape(batch, seq, -1), ev_w_pool[e].astype(BF16), ev_pool_scale[e].reshape(1, -1),
                ev_conv_w[e], ev_conv_b[e].reshape(1, -1), ev_w_a[e].astype(BF16),
                ev_b_a[e].reshape(1, -1), ev_w_x[e].astype(BF16), ev_b_x[e].reshape(1, -1),
                ev_lam[e].reshape(1, -1))
            xf = proj_ln(mixed.reshape(n, -1), ev_w_out[e].astype(BF16), xf, mod_l, g0, b0,
                         seq=seq, alpha=alpha, gate_row=2)
        else:
            o = l // 2
            w_in = od_w_in[o]
            w_f = jnp.zeros((d, 128), F32).at[:, :N_HEADS_ATTN].set(w_in[:, 3 * d_mix:])
            wf_hi = w_f.astype(BF16)
            wf_lo = (w_f - wf_hi.astype(F32)).astype(BF16)
            b_f = jnp.zeros((1, 128), F32).at[0, :N_HEADS_ATTN].set(od_b_f[o])
            w_qv = jnp.concatenate([w_in[:, :d_mix], w_in[:, 2 * d_mix:3 * d_mix]], axis=1).astype(BF16)
            w_kt = w_in[:, d_mix:2 * d_mix].T.astype(BF16)
            qv, k_t, f_cum = odd_in(xf, mod_l, w_qv, w_kt, wf_hi, wf_lo, b_f, seq=seq)
            f_keys = f_cum[:, :N_HEADS_ATTN].reshape(batch, seq, N_HEADS_ATTN)
            f_keys = jnp.transpose(f_keys, (0, 2, 1)).reshape(batch * N_HEADS_ATTN, 1, seq)
            attn = attention(qv, k_t, f_keys, batch=batch, seq=seq)
            xf = proj_ln(attn, od_w_out[o].astype(BF16), xf, mod_l, g0, b0,
                         seq=seq, alpha=alpha, gate_row=2)

        keys = peer_keys[l].reshape(2 * PEER_HEADS, N_KEYS, -1).astype(BF16)
        q, h2 = mod_matmul(xf, mod_l, peer_wq[l].astype(BF16), seq=seq, shift_row=3, scale_row=4,
                           out_dtype=BF16, emit_h=True)
        gmat = peer_route(q, keys, seq=seq)
        y = peer_dense(h2, gmat, u_bf, v_bf, layer=l, seq=seq)
        xf = residual_ln(xf, y, mod_l, g1, b1, seq=seq, alpha=alpha, gate_row=5)
    return xf.reshape(batch, seq, d)
```

```python
import functools
import math

import jax
import jax.numpy as jnp
from jax import lax
from jax.experimental import pallas as pl
from jax.experimental.pallas import tpu as pltpu

F32 = jnp.float32
BF16 = jnp.bfloat16
I32 = jnp.int32

LN_EPS = 1e-5
POOL_WINDOWS = (2, 4, 8, 16)
CONV_WIDTH = 4
LRU_BLOCKS = 8
LRU_C = 8.0
N_HEADS_ATTN = 16
PEER_HEADS = 8
PEER_TOPK = 16
N_KEYS = 128

V7X_VMEM_BYTES = 64 * 1024 * 1024
VMEM_LIMIT = V7X_VMEM_BYTES * 7 // 8
SUBLANES = 8
LANES = 128
NEG_INF = float("-inf")
LOG2E = math.log2(math.e)


def _params(sem):
    return pltpu.CompilerParams(dimension_semantics=sem, vmem_limit_bytes=VMEM_LIMIT)


def _resident(shape, index_map):
    return pl.BlockSpec(shape, index_map, pipeline_mode=pl.Buffered(1))


def _gelu(x):
    c = math.sqrt(2.0 / math.pi)
    return 0.5 * x * (1.0 + jnp.tanh(c * (x + 0.044715 * (x * x * x))))


def _log_sigmoid(x):
    return jnp.minimum(x, 0.0) - jnp.log1p(jnp.exp(-jnp.abs(x)))


def _sigmoid(x):
    return 0.5 * jnp.tanh(0.5 * x) + 0.5


def _layer_norm(r, g, b):
    mu = jnp.mean(r, axis=-1, keepdims=True)
    d = r - mu
    var = jnp.mean(d * d, axis=-1, keepdims=True)
    return d * lax.rsqrt(var + LN_EPS) * g + b


def _dot_nt(a, b):
    return lax.dot_general(a, b, (((1,), (1,)), ((), ())), preferred_element_type=F32)


def _cast_kernel(w_ref, o_ref):
    o_ref[...] = w_ref[...].astype(o_ref.dtype)


def cast_bf16(w2d, *, tm=1024):
    rows, cols = w2d.shape
    return pl.pallas_call(
        _cast_kernel,
        grid=(rows // tm,),
        in_specs=[pl.BlockSpec((tm, cols), lambda i: (i, 0))],
        out_specs=pl.BlockSpec((tm, cols), lambda i: (i, 0)),
        out_shape=jax.ShapeDtypeStruct((rows, cols), BF16),
        compiler_params=_params(("arbitrary",)),
        name="cast_bf16",
    )(w2d)


def _ada_kernel(c_ref, w_ref, b_ref, o_ref):
    c = c_ref[...]
    ca = c * _sigmoid(c)
    o_ref[...] = jnp.dot(ca, w_ref[...], preferred_element_type=F32) + b_ref[...]


def ada_mod(c_pad, ada_w, ada_b):
    depth, d, n6 = ada_w.shape
    rows = c_pad.shape[0]
    tn = 1024
    return pl.pallas_call(
        _ada_kernel,
        grid=(depth, n6 // tn),
        in_specs=[
            pl.BlockSpec((rows, d), lambda l, j: (0, 0)),
            pl.BlockSpec((None, d, tn), lambda l, j: (l, 0, j)),
            pl.BlockSpec((None, 1, tn), lambda l, j: (l, 0, j)),
        ],
        out_specs=pl.BlockSpec((None, rows, tn), lambda l, j: (l, 0, j)),
        out_shape=jax.ShapeDtypeStruct((depth, rows, n6), F32),
        compiler_params=_params(("arbitrary", "arbitrary")),
        name="ada_mod",
    )(c_pad, ada_w, ada_b.reshape(depth, 1, n6))


def _mod_matmul_kernel(x_ref, mod_ref, w_ref, o_ref, h_ref, *, shift_row, scale_row):
    @pl.when(pl.program_id(1) == 0)
    def _():
        sh = mod_ref[shift_row:shift_row + 1, :]
        sc = mod_ref[scale_row:scale_row + 1, :]
        h_ref[...] = (x_ref[...] * (1.0 + sc) + sh).astype(BF16)

    o_ref[...] = jnp.dot(h_ref[...], w_ref[...], preferred_element_type=F32).astype(o_ref.dtype)


def mod_matmul(x2d, mod_l, w_bf16, *, seq, shift_row, scale_row, out_dtype, emit_h=False,
               tm=1024, tn=1024):
    n, d = x2d.shape
    nout = w_bf16.shape[1]
    tm = min(tm, seq)
    blocks_per_batch = seq // tm
    kern = functools.partial(_mod_matmul_kernel, shift_row=shift_row, scale_row=scale_row)
    out_specs = [pl.BlockSpec((tm, tn), lambda i, j: (i, j))]
    out_shape = [jax.ShapeDtypeStruct((n, nout), out_dtype)]
    scratch = []
    if emit_h:
        out_specs.append(pl.BlockSpec((tm, d), lambda i, j: (i, 0)))
        out_shape.append(jax.ShapeDtypeStruct((n, d), BF16))
    else:
        scratch.append(pltpu.VMEM((tm, d), BF16))
    res = pl.pallas_call(
        kern,
        grid=(n // tm, nout // tn),
        in_specs=[
            pl.BlockSpec((tm, d), lambda i, j: (i, 0)),
            pl.BlockSpec((None, 6, d), lambda i, j: (i // blocks_per_batch, 0, 0)),
            pl.BlockSpec((d, tn), lambda i, j: (0, j)),
        ],
        out_specs=out_specs,
        out_shape=out_shape,
        scratch_shapes=scratch,
        compiler_params=_params(("arbitrary", "arbitrary")),
        name="mod_matmul",
    )(x2d, mod_l, w_bf16)
    return res if emit_h else res[0]


def _cumsum_rows(x):
    rows = x.shape[0]
    ridx = lax.broadcasted_iota(I32, x.shape, 0)
    d = 1
    while d < rows:
        x = x + jnp.where(ridx >= d, pltpu.roll(x, d, 0), 0.0)
        d *= 2
    return x


def _odd_in_kernel(x_ref, mod_ref, w_ref, wkt_ref, wfh_ref, wfl_ref, bf_ref, o_ref, kt_ref, f_ref,
                   h_ref, carry_ref, *, blocks_per_batch, nb, q_scale):
    i = pl.program_id(0)
    j = pl.program_id(1)

    @pl.when(j == 0)
    def _():
        sh = mod_ref[0:1, :]
        sc = mod_ref[1:2, :]
        h = x_ref[...] * (1.0 + sc) + sh
        h_hi = h.astype(BF16)
        h_ref[...] = h_hi
        h_lo = (h - h_hi.astype(F32)).astype(BF16)
        zf = (jnp.dot(h_hi, wfh_ref[...], preferred_element_type=F32)
              + jnp.dot(h_lo, wfh_ref[...], preferred_element_type=F32)
              + jnp.dot(h_hi, wfl_ref[...], preferred_element_type=F32))
        logf = _log_sigmoid(zf + bf_ref[...])

        @pl.when(i % blocks_per_batch == 0)
        def _():
            carry_ref[...] = jnp.zeros_like(carry_ref)

        cs = _cumsum_rows(logf) + carry_ref[0:1, :]
        f_ref[...] = cs
        carry_ref[...] = jnp.broadcast_to(cs[cs.shape[0] - 1:, :], carry_ref.shape)

    is_key = jnp.logical_and(j >= nb, j < 2 * nb)

    @pl.when(jnp.logical_not(is_key))
    def _():
        z = jnp.dot(h_ref[...], w_ref[...], preferred_element_type=F32)
        scale = jnp.where(j < nb, q_scale, 1.0)
        o_ref[...] = (z * scale).astype(o_ref.dtype)

    @pl.when(is_key)
    def _():
        kt_ref[...] = _dot_nt(wkt_ref[...], h_ref[...]).astype(kt_ref.dtype)


def odd_in(x2d, mod_l, w_qv, w_kt, wf_hi, wf_lo, b_f_pad, *, seq, tm=1024, tn=1024):
    n, d = x2d.shape
    d_mix = w_kt.shape[0]
    tm = min(tm, seq)
    blocks_per_batch = seq // tm
    head_dim = d_mix // N_HEADS_ATTN
    nb = d_mix // tn
    kern = functools.partial(_odd_in_kernel, blocks_per_batch=blocks_per_batch, nb=nb,
                             q_scale=head_dim ** -0.5 * LOG2E)

    def qv_block(j):
        return jnp.where(j < nb, j, jnp.where(j < 2 * nb, nb - 1, j - nb))

    def k_block(j):
        return jnp.clip(j - nb, 0, nb - 1)

    return pl.pallas_call(
        kern,
        grid=(n // tm, 3 * nb),
        in_specs=[
            pl.BlockSpec((tm, d), lambda i, j: (i, 0)),
            pl.BlockSpec((None, 6, d), lambda i, j: (i // blocks_per_batch, 0, 0)),
            pl.BlockSpec((d, tn), lambda i, j: (0, qv_block(j))),
            pl.BlockSpec((tn, d), lambda i, j: (k_block(j), 0)),
            _resident((d, LANES), lambda i, j: (0, 0)),
            _resident((d, LANES), lambda i, j: (0, 0)),
            _resident((1, LANES), lambda i, j: (0, 0)),
        ],
        out_specs=[
            pl.BlockSpec((tm, tn), lambda i, j: (i, qv_block(j))),
            pl.BlockSpec((tn, tm), lambda i, j: (k_block(j), i)),
            pl.BlockSpec((tm, LANES), lambda i, j: (i, 0)),
        ],
        out_shape=[
            jax.ShapeDtypeStruct((n, 2 * d_mix), BF16),
            jax.ShapeDtypeStruct((d_mix, n), BF16),
            jax.ShapeDtypeStruct((n, LANES), F32),
        ],
        scratch_shapes=[pltpu.VMEM((tm, d), BF16), pltpu.VMEM((SUBLANES, LANES), F32)],
        compiler_params=_params(("arbitrary", "arbitrary")),
        name="odd_in",
    )(x2d, mod_l, w_qv, w_kt, wf_hi, wf_lo, b_f_pad)


def _attn_kernel(q_ref, kt_ref, v_ref, fk_ref, o_ref, sa_ref, sb_ref, vaug_ref, pb_ref, *, tq, tk):
    qi = pl.program_id(2)
    f_ref0 = fk_ref[0:1, pl.ds(pl.multiple_of(qi * tq, tq), tk)][:, 0:1]
    dh = q_ref.shape[1]

    @pl.when(qi == 0)
    def _():
        vaug_ref[:, 0:dh] = v_ref[...]
        vaug_ref[:, dh:2 * dh] = jnp.ones((v_ref.shape[0], dh), BF16)
    per_q = tq // tk
    n_full = qi * per_q

    def scores(c, s_ref):
        start = pl.multiple_of(c * tk, tk)
        s = jnp.dot(q_ref[...], kt_ref[:, pl.ds(start, tk)], preferred_element_type=F32)
        s_ref[...] = s + (f_ref0 - fk_ref[0:1, pl.ds(start, tk)]) * LOG2E

    def values(c, p):
        start = pl.multiple_of(c * tk, tk)
        return jnp.dot(p, vaug_ref[pl.ds(start, tk), :], preferred_element_type=F32)

    def softmax_step(s_ref, m, diag_offset=None):
        if diag_offset is not None:
            row = lax.broadcasted_iota(I32, (tq, tk), 0)
            col = lax.broadcasted_iota(I32, (tq, tk), 1) + diag_offset
            s_ref[...] = jnp.where(col <= row, s_ref[...], NEG_INF)
        m_new = jnp.maximum(m, jnp.max(s_ref[...], axis=-1, keepdims=True))
        alpha = jnp.exp2(m - m_new)
        p = jnp.exp2(s_ref[...] - m_new).astype(BF16)
        return m_new, alpha, p

    def update(c, s_ref, carry, diag_offset=None):
        m, acc = carry
        m, alpha, p = softmax_step(s_ref, m, diag_offset)
        return m, alpha * acc + values(c, p)

    pb_ref[...] = jnp.zeros(pb_ref.shape, BF16)
    scores(0, sa_ref)

    def group(c0, count, carry):
        m, acc = carry
        acc = acc + values(jnp.maximum(c0 - 1, 0), pb_ref[...])
        bufs = (sa_ref, sb_ref)
        for t in range(count):
            scores(c0 + t + 1, bufs[(t + 1) % 2])
            m, alpha, p = softmax_step(bufs[t % 2], m)
            if t + 1 < count:
                acc = alpha * acc + values(c0 + t, p)
            else:
                pb_ref[...] = p
                acc = alpha * acc
        return m, acc

    init = (jnp.full((tq, 1), NEG_INF, F32), jnp.zeros((tq, 2 * dh), F32))
    carry = init
    done = 0
    for count in (8, 4, 2):
        steps = (n_full - done) // count
        carry = lax.fori_loop(
            0, steps, lambda p, cr, base=done, count=count: group(base + count * p, count, cr), carry)
        done = done + count * steps
    pending = jnp.maximum(done - 1, 0)

    def flush(carry):
        m, acc = carry
        return m, acc + values(pending, pb_ref[...])

    def diagonal(carry, bufs):
        for dchunk in range(per_q):
            if dchunk + 1 < per_q:
                scores(n_full + dchunk + 1, bufs[(dchunk + 1) % 2])
            carry = update(n_full + dchunk, bufs[dchunk % 2], carry, diag_offset=dchunk * tk)
        return carry

    def even_tail(carry):
        return diagonal(flush(carry), (sa_ref, sb_ref))

    def odd_tail(carry):
        scores(n_full, sb_ref)
        carry = update(n_full - 1, sa_ref, flush(carry))
        return diagonal(carry, (sb_ref, sa_ref))

    if per_q % 2 == 0:
        m, acc = even_tail(carry)
    else:
        m, acc = lax.cond(n_full % 2 == 1, odd_tail, even_tail, carry)
    o_ref[...] = (acc[:, 0:dh] / acc[:, dh:dh + 1]).astype(o_ref.dtype)


def attention(qv, k_t, f_keys, *, batch, seq, tq=512, tk=512):
    n = qv.shape[0]
    d_mix = qv.shape[1] // 2
    heads = N_HEADS_ATTN
    dh = d_mix // heads
    tq = min(tq, seq)
    tk = min(tk, tq)
    nq = seq // tq
    kern = functools.partial(_attn_kernel, tq=tq, tk=tk)
    return pl.pallas_call(
        kern,
        grid=(batch, heads, nq),
        in_specs=[
            pl.BlockSpec((tq, dh), lambda b, h, i: (b * nq + i, h)),
            pl.BlockSpec((dh, seq), lambda b, h, i: (h, b)),
            pl.BlockSpec((seq, dh), lambda b, h, i: (b, heads + h)),
            pl.BlockSpec((None, 1, seq), lambda b, h, i: (b * heads + h, 0, 0)),
        ],
        out_specs=pl.BlockSpec((tq, dh), lambda b, h, i: (b * nq + i, h)),
        out_shape=jax.ShapeDtypeStruct((n, d_mix), BF16),
        scratch_shapes=[pltpu.VMEM((tq, tk), F32), pltpu.VMEM((tq, tk), F32),
                        pltpu.VMEM((seq, 2 * dh), BF16), pltpu.VMEM((tq, tk), BF16)],
        compiler_params=_params(("arbitrary", "arbitrary", "arbitrary")),
        name="attention",
    )(qv, k_t, qv, f_keys)


HALO = 16


def _even_mix_kernel(z_ref, wpool_ref, pscale_ref, convw_ref, convb_ref, wa_ref, ba_ref,
                     wx_ref, bx_ref, lam_ref, o_ref, pbuf, cbuf, hstate, *, ts, d_pool, d_lru):
    sb = pl.program_id(1)
    group = d_pool // len(POOL_WINDOWS)
    blk = d_lru // LRU_BLOCKS

    @pl.when(sb == 0)
    def _():
        pbuf[0:HALO, :] = jnp.zeros((HALO, d_pool), F32)
        cbuf[0:HALO, :] = jnp.zeros((HALO, d_lru), F32)
        hstate[...] = jnp.zeros_like(hstate)

    up = z_ref[:, 0:d_pool]
    pbuf[HALO:HALO + ts, :] = up
    pos = sb * ts + lax.broadcasted_iota(I32, (ts, group), 0)
    ya = []
    for g, w in enumerate(POOL_WINDOWS):
        lo, hi = g * group, (g + 1) * group
        need = w - 1
        cur = pbuf[HALO - need:HALO + ts, lo:hi]
        span = 1
        while span < w:
            rows = cur.shape[0] - span
            cur = cur[span:span + rows, :] + cur[0:rows, :]
            span *= 2
        cnt = jnp.minimum(pos + 1, w).astype(F32)
        pooled = cur / cnt - up[:, lo:hi]
        y = jnp.dot(pooled.astype(BF16), wpool_ref[g], preferred_element_type=F32)
        ya.append(y * pscale_ref[0:1, lo:hi])
    o_ref[:, 0:d_pool] = jnp.concatenate(ya, axis=-1).astype(o_ref.dtype)
    pbuf[0:HALO, :] = pbuf[ts:ts + HALO, :]

    ul = z_ref[:, d_pool:d_pool + d_lru]
    ug = z_ref[:, d_pool + d_lru:d_pool + 2 * d_lru]
    cbuf[HALO:HALO + ts, :] = ul
    xc = jnp.broadcast_to(convb_ref[0:1, :], (ts, d_lru))
    for k in range(CONV_WIDTH):
        off = HALO - (CONV_WIDTH - 1) + k
        xc = xc + cbuf[off:off + ts, :] * convw_ref[k:k + 1, :]
    cbuf[0:HALO, :] = cbuf[ts:ts + HALO, :]

    xb = xc.astype(BF16)
    ra, ia = [], []
    for hb in range(LRU_BLOCKS):
        xs = xb[:, hb * blk:(hb + 1) * blk]
        ra.append(jnp.dot(xs, wa_ref[hb], preferred_element_type=F32))
        ia.append(jnp.dot(xs, wx_ref[hb], preferred_element_type=F32))
    r_gate = _sigmoid(jnp.concatenate(ra, axis=-1) + ba_ref[0:1, :])
    i_gate = _sigmoid(jnp.concatenate(ia, axis=-1) + bx_ref[0:1, :])
    log_a = LRU_C * r_gate * _log_sigmoid(lam_ref[0:1, :])
    a = jnp.exp(log_a)
    m2 = -jnp.tanh(log_a) * (1.0 + a * a)
    mult = jnp.where(m2 > 0.0, m2 * lax.rsqrt(m2), 0.0)
    b = mult * i_gate * xc

    n_grp = ts // SUBLANES
    a = a.reshape(n_grp, SUBLANES, d_lru)
    b = b.reshape(n_grp, SUBLANES, d_lru)
    rsub = lax.broadcasted_iota(I32, (n_grp, SUBLANES, d_lru), 1)
    d = 1
    while d < SUBLANES:
        keep = rsub >= d
        a_sh = jnp.where(keep, pltpu.roll(a, d, 1), 1.0)
        b_sh = jnp.where(keep, pltpu.roll(b, d, 1), 0.0)
        b = a * b_sh + b
        a = a * a_sh
        d *= 2
    state = hstate[0:1, :]
    groups = []
    for v in range(n_grp):
        hv = b[v] + a[v] * state
        state = hv[SUBLANES - 1:SUBLANES, :]
        groups.append(hv)
    h = jnp.concatenate(groups, axis=0)
    hstate[...] = jnp.broadcast_to(state, hstate.shape)
    o_ref[:, d_pool:d_pool + d_lru] = (h * _gelu(ug)).astype(o_ref.dtype)


def even_mix(z3d, w_pool, pool_scale, conv_w, conv_b, w_a, b_a, w_x, b_x, lam, *, ts=256):
    batch, seq, dz = z3d.shape
    d_pool = pool_scale.shape[-1]
    d_lru = lam.shape[-1]
    ts = min(ts, seq)
    group = d_pool // len(POOL_WINDOWS)
    blk = d_lru // LRU_BLOCKS
    kern = functools.partial(_even_mix_kernel, ts=ts, d_pool=d_pool, d_lru=d_lru)
    const2 = lambda b, s: (0, 0)
    const3 = lambda b, s: (0, 0, 0)
    return pl.pallas_call(
        kern,
        grid=(batch, seq // ts),
        in_specs=[
            pl.BlockSpec((None, ts, dz), lambda b, s: (b, s, 0)),
            _resident((len(POOL_WINDOWS), group, group), const3),
            _resident((1, d_pool), const2),
            _resident((CONV_WIDTH, d_lru), const2),
            _resident((1, d_lru), const2),
            _resident((LRU_BLOCKS, blk, blk), const3),
            _resident((1, d_lru), const2),
            _resident((LRU_BLOCKS, blk, blk), const3),
            _resident((1, d_lru), const2),
            _resident((1, d_lru), const2),
        ],
        out_specs=pl.BlockSpec((None, ts, d_pool + d_lru), lambda b, s: (b, s, 0)),
        out_shape=jax.ShapeDtypeStruct((batch, seq, d_pool + d_lru), BF16),
        scratch_shapes=[
            pltpu.VMEM((HALO + ts, d_pool), F32),
            pltpu.VMEM((HALO + ts, d_lru), F32),
            pltpu.VMEM((8, d_lru), F32),
        ],
        compiler_params=_params(("arbitrary", "arbitrary")),
        name="even_mix",
    )(z3d, w_pool, pool_scale, conv_w, conv_b, w_a, b_a, w_x, b_x, lam)


def _proj_ln_kernel(a_ref, w_ref, x_ref, mod_ref, g_ref, b_ref, o_ref, *, alpha, gate_row):
    y = jnp.dot(a_ref[...], w_ref[...], preferred_element_type=F32)
    gate = mod_ref[gate_row:gate_row + 1, :]
    r = alpha * x_ref[...] + (1.0 + gate) * y
    o_ref[...] = _layer_norm(r, g_ref[...], b_ref[...])


def proj_ln(a2d, w_bf16, x2d, mod_l, ln_g, ln_b, *, seq, alpha, gate_row, tm=512):
    n, d = x2d.shape
    k = a2d.shape[1]
    tm = min(tm, seq)
    blocks_per_batch = seq // tm
    kern = functools.partial(_proj_ln_kernel, alpha=alpha, gate_row=gate_row)
    return pl.pallas_call(
        kern,
        grid=(n // tm,),
        in_specs=[
            pl.BlockSpec((tm, k), lambda i: (i, 0)),
            _resident((k, d), lambda i: (0, 0)),
            pl.BlockSpec((tm, d), lambda i: (i, 0)),
            pl.BlockSpec((None, 6, d), lambda i: (i // blocks_per_batch, 0, 0)),
            _resident((1, d), lambda i: (0, 0)),
            _resident((1, d), lambda i: (0, 0)),
        ],
        out_specs=pl.BlockSpec((tm, d), lambda i: (i, 0)),
        out_shape=jax.ShapeDtypeStruct((n, d), F32),
        compiler_params=_params(("arbitrary",)),
        name="proj_ln",
    )(a2d, w_bf16, x2d, mod_l, ln_g, ln_b)


def _top16_rows(s, kidx, sentinel):
    t = s.shape[1]
    r16 = lax.broadcasted_iota(I32, (PEER_TOPK, t), 0)
    vals = jnp.zeros((PEER_TOPK, t), F32)
    idxs = jnp.zeros((PEER_TOPK, t), I32)
    for r in range(PEER_TOPK):
        m = jnp.max(s, axis=0, keepdims=True)
        am = jnp.min(jnp.where(s == m, kidx, sentinel), axis=0, keepdims=True)
        s = jnp.where(kidx == am, NEG_INF, s)
        vals = jnp.where(r16 == r, m, vals)
        idxs = jnp.where(r16 == r, am, idxs)
    return vals, idxs


def _take_rows16(table, sel):
    out = jnp.zeros(sel.shape, table.dtype)
    for s in range(PEER_TOPK):
        out = jnp.where(sel == s, table[s:s + 1, :], out)
    return out


BUILD_UNROLL = 64


def _peer_route_kernel(q_ref, keys_ref, g_ref,
                       gs_ref, sc_ref, val_ref, idx_ref, e_ref, w_ref, et_ref, wt_ref, *, tr):
    nhp = 2 * PEER_HEADS
    for hp in range(nhp):
        sc_ref[hp] = _dot_nt(keys_ref[hp], q_ref[:, hp * N_KEYS:(hp + 1) * N_KEYS])

    kidx = lax.broadcasted_iota(I32, (N_KEYS, tr), 0)

    def first_topk(hd, carry):
        for half in range(2):
            hp = 2 * hd + half
            vals, idxs = _top16_rows(sc_ref[hp], kidx, N_KEYS)
            val_ref[hp] = vals
            idx_ref[hp] = idxs
        return carry

    lax.fori_loop(0, PEER_HEADS, first_topk, 0)

    i16 = lax.broadcasted_iota(I32, (16, tr), 0)
    i8 = lax.broadcasted_iota(I32, (8, tr), 0)
    flat = jnp.concatenate([
        i16 * 16, i8 * 16 + 1, i8 * 16 + 2, i8 * 16 + 3,
        i16, 16 + i8, 32 + i8,
    ], axis=0)
    valid = jnp.concatenate([
        i16 < 16, i8 < 8, i8 < 5, i8 < 4,
        i16 >= 4, i8 >= 4, i8 == 4,
    ], axis=0)

    def second_topk(step, carry):
        for sub in range(2):
            hd = 2 * step + sub
            s1 = val_ref[2 * hd]
            s2 = val_ref[2 * hd + 1]
            cand = jnp.concatenate([
                s1 + s2[0:1, :], s1[0:8, :] + s2[1:2, :], s1[0:8, :] + s2[2:3, :],
                s1[0:8, :] + s2[3:4, :],
                s1[0:1, :] + s2, s1[1:2, :] + s2[0:8, :], s1[2:3, :] + s2[0:8, :],
            ], axis=0)
            cand = jnp.where(valid, cand, NEG_INF)
            top, fsel = _top16_rows(cand, flat, PEER_TOPK * PEER_TOPK)
            e = jnp.exp(top - top[0:1, :])
            gate = e / jnp.sum(e, axis=0, keepdims=True)
            a = _take_rows16(idx_ref[2 * hd], fsel >> 4)
            b = _take_rows16(idx_ref[2 * hd + 1], fsel & (PEER_TOPK - 1))
            row0 = pl.multiple_of(hd * PEER_TOPK, PEER_TOPK)
            e_ref[pl.ds(row0, PEER_TOPK), :] = a * N_KEYS + b
            w_ref[pl.ds(row0, PEER_TOPK), :] = gate
        return carry

    lax.fori_loop(0, PEER_HEADS // 2, second_topk, 0)

    n_act = PEER_HEADS * PEER_TOPK
    for c in range(tr // n_act):
        et_ref[c * n_act:(c + 1) * n_act, :] = e_ref[:, c * n_act:(c + 1) * n_act].T
        wt_ref[c * n_act:(c + 1) * n_act, :] = w_ref[:, c * n_act:(c + 1) * n_act].T

    iota_rows = lax.broadcasted_iota(I32, (N_KEYS, n_act), 0)
    sub_iota = lax.broadcasted_iota(I32, (SUBLANES, N_KEYS), 0)

    def build(step, carry):
        base = pl.multiple_of(step * BUILD_UNROLL, BUILD_UNROLL)
        erows = et_ref[pl.ds(base, BUILD_UNROLL), :]
        wrows = wt_ref[pl.ds(base, BUILD_UNROLL), :]
        for u in range(BUILD_UNROLL):
            erow = erows[u:u + 1, :]
            wrow = wrows[u:u + 1, :]
            pt = jnp.where(iota_rows == (erow >> 7), wrow, 0.0).astype(BF16)
            qt = jnp.where(iota_rows == (erow & (N_KEYS - 1)), 1.0, 0.0).astype(BF16)
            gs_ref[u * N_KEYS:(u + 1) * N_KEYS, :] = _dot_nt(pt, qt)
        for gi in range(BUILD_UNROLL // SUBLANES):
            grp = step * (BUILD_UNROLL // SUBLANES) + gi
            for k in range(N_KEYS // SUBLANES):
                tiles = []
                for t in range(SUBLANES):
                    row0 = (gi * SUBLANES + t) * N_KEYS + k * SUBLANES
                    tiles.append(gs_ref[row0:row0 + SUBLANES, :])
                for dist in (4, 2, 1):
                    take_hi = (sub_iota & dist) != 0
                    for t in range(SUBLANES):
                        if t & dist:
                            continue
                        lo, hi = tiles[t], tiles[t + dist]
                        tiles[t] = jnp.where(take_hi, pltpu.roll(hi, dist, 0), lo)
                        tiles[t + dist] = jnp.where(take_hi, hi, pltpu.roll(lo, SUBLANES - dist, 0))
                for s in range(SUBLANES):
                    g_ref[grp, k * SUBLANES + s] = tiles[s]
        return carry

    lax.fori_loop(0, tr // BUILD_UNROLL, build, 0)


def peer_route(q2d, keys_bf16, *, seq, tr=256):
    n, d = q2d.shape
    tr = min(tr, seq)
    nhp = 2 * PEER_HEADS
    n_act = PEER_HEADS * PEER_TOPK
    kern = functools.partial(_peer_route_kernel, tr=tr)
    return pl.pallas_call(
        kern,
        grid=(n // tr,),
        in_specs=[
            pl.BlockSpec((tr, d), lambda i: (i, 0)),
            _resident(keys_bf16.shape, lambda i: (0, 0, 0)),
        ],
        out_specs=pl.BlockSpec((tr // SUBLANES, N_KEYS, SUBLANES, N_KEYS), lambda i: (i, 0, 0, 0)),
        out_shape=jax.ShapeDtypeStruct((n // SUBLANES, N_KEYS, SUBLANES, N_KEYS), F32),
        scratch_shapes=[
            pltpu.VMEM((BUILD_UNROLL * N_KEYS, N_KEYS), F32),
            pltpu.VMEM((nhp, N_KEYS, tr), F32),
            pltpu.VMEM((nhp, PEER_TOPK, tr), F32),
            pltpu.VMEM((nhp, PEER_TOPK, tr), I32),
            pltpu.VMEM((n_act, tr), I32),
            pltpu.VMEM((n_act, tr), F32),
            pltpu.VMEM((tr, n_act), I32),
            pltpu.VMEM((tr, n_act), F32),
        ],
        compiler_params=_params(("arbitrary",)),
        name="peer_route",
    )(q2d, keys_bf16)


DENSE_CHUNK = 256


def _peer_dense_kernel(h_ref, g_ref, u_ref, v_ref, o_ref, act_ref):
    j = pl.program_id(1)
    rows = g_ref.shape[1]
    tm = h_ref.shape[0]

    @pl.when(j == 0)
    def _():
        o_ref[...] = jnp.zeros_like(o_ref)

    per = DENSE_CHUNK // N_KEYS
    for c in range(rows // per):
        ccols = slice(c * DENSE_CHUNK, (c + 1) * DENSE_CHUNK)
        z = _dot_nt(h_ref[...], u_ref[ccols, :])
        for r in range(per):
            row = c * per + r
            gate = g_ref[:, row, :, :].reshape(tm, N_KEYS)
            act_ref[:, row * N_KEYS:(row + 1) * N_KEYS] = (
                _gelu(z[:, r * N_KEYS:(r + 1) * N_KEYS]) * gate).astype(BF16)
    half = act_ref.shape[1] // 2
    o_ref[...] += jnp.dot(act_ref[:, :half], v_ref[:half, :], preferred_element_type=F32)
    o_ref[...] += jnp.dot(act_ref[:, half:], v_ref[half:, :], preferred_element_type=F32)


def peer_dense(h2d, g3d, u_bf16, v_bf16, *, layer, seq, tm=1024, te=1024):
    n, d = h2d.shape
    n_exp = v_bf16.shape[1]
    tm = min(tm, seq)
    return pl.pallas_call(
        _peer_dense_kernel,
        grid=(n // tm, n_exp // te),
        in_specs=[
            pl.BlockSpec((tm, d), lambda i, j: (i, 0), pipeline_mode=pl.Buffered(1)),
            pl.BlockSpec((tm // SUBLANES, te // N_KEYS, SUBLANES, N_KEYS), lambda i, j: (i, j, 0, 0)),
            pl.BlockSpec((None, te, d), lambda i, j: (layer, j, 0)),
            pl.BlockSpec((None, te, d), lambda i, j: (layer, j, 0)),
        ],
        out_specs=pl.BlockSpec((tm, d), lambda i, j: (i, 0)),
        out_shape=jax.ShapeDtypeStruct((n, d), F32),
        scratch_shapes=[pltpu.VMEM((tm, te), BF16)],
        compiler_params=_params(("arbitrary", "arbitrary")),
        name="peer_dense",
    )(h2d, g3d, u_bf16, v_bf16)


def _residual_ln_kernel(x_ref, y_ref, mod_ref, g_ref, b_ref, o_ref, *, alpha, gate_row):
    gate = mod_ref[gate_row:gate_row + 1, :]
    r = alpha * x_ref[...] + (1.0 + gate) * y_ref[...]
    o_ref[...] = _layer_norm(r, g_ref[...], b_ref[...])


def residual_ln(x2d, y2d, mod_l, ln_g, ln_b, *, seq, alpha, gate_row, tm=512):
    n, d = x2d.shape
    tm = min(tm, seq)
    blocks_per_batch = seq // tm
    kern = functools.partial(_residual_ln_kernel, alpha=alpha, gate_row=gate_row)
    return pl.pallas_call(
        kern,
        grid=(n // tm,),
        in_specs=[
            pl.BlockSpec((tm, d), lambda i: (i, 0)),
            pl.BlockSpec((tm, d), lambda i: (i, 0)),
            pl.BlockSpec((None, 6, d), lambda i: (i // blocks_per_batch, 0, 0)),
            _resident((1, d), lambda i: (0, 0)),
            _resident((1, d), lambda i: (0, 0)),
        ],
        out_specs=pl.BlockSpec((tm, d), lambda i: (i, 0)),
        out_shape=jax.ShapeDtypeStruct((n, d), F32),
        compiler_params=_params(("arbitrary",)),
        name="residual_ln",
    )(x2d, y2d, mod_l, ln_g, ln_b)


def kernel(x, c, ada_w, ada_b, ln_g, ln_b, peer_wq, peer_keys, peer_u, peer_v, ev_w_in, ev_w_pool,
           ev_pool_scale, ev_conv_w, ev_conv_b, ev_w_a, ev_b_a, ev_w_x, ev_b_x, ev_lam, ev_w_out,
           od_w_in, od_b_f, od_w_out):
    batch, seq, d = x.shape
    depth = ada_w.shape[0]
    n = batch * seq
    alpha = (2.0 * depth) ** 0.25
    d_mix = od_w_out.shape[1]

    c_pad = jnp.zeros((8, d), F32).at[:batch].set(c)
    mod = ada_mod(c_pad, ada_w, ada_b)[:, :batch].reshape(depth, batch, 6, d)

    n_exp = peer_u.shape[1]
    u_bf = cast_bf16(peer_u.reshape(depth * n_exp, d)).reshape(depth, n_exp, d)
    v_bf = cast_bf16(peer_v.reshape(depth * n_exp, d)).reshape(depth, n_exp, d)

    xf = x.reshape(n, d)
    for l in range(depth):
        mod_l = mod[l]
        g0 = ln_g[l, 0].reshape(1, d)
        b0 = ln_b[l, 0].reshape(1, d)
        g1 = ln_g[l, 1].reshape(1, d)
        b1 = ln_b[l, 1].reshape(1, d)
        if l % 2 == 0:
            e = l // 2
            z = mod_matmul(xf, mod_l, ev_w_in[e].astype(BF16), seq=seq, shift_row=0, scale_row=1,
                           out_dtype=F32)
            mixed = even_mix(
                z.reshape(batch, seq, -1), ev_w_pool[e].astype(BF16), ev_pool_scale[e].reshape(1, -1),
                ev_conv_w[e], ev_conv_b[e].reshape(1, -1), ev_w_a[e].astype(BF16),
                ev_b_a[e].reshape(1, -1), ev_w_x[e].astype(BF16), ev_b_x[e].reshape(1, -1),
                ev_lam[e].reshape(1, -1))
            xf = proj_ln(mixed.reshape(n, -1), ev_w_out[e].astype(BF16), xf, mod_l, g0, b0,
                         seq=seq, alpha=alpha, gate_row=2)
        else:
            o = l // 2
            w_in = od_w_in[o]
            w_f = jnp.zeros((d, LANES), F32).at[:, :N_HEADS_ATTN].set(w_in[:, 3 * d_mix:])
            wf_hi = w_f.astype(BF16)
            wf_lo = (w_f - wf_hi.astype(F32)).astype(BF16)
            b_f = jnp.zeros((1, LANES), F32).at[0, :N_HEADS_ATTN].set(od_b_f[o])
            w_qv = jnp.concatenate([w_in[:, :d_mix], w_in[:, 2 * d_mix:3 * d_mix]], axis=1).astype(BF16)
            w_kt = w_in[:, d_mix:2 * d_mix].T.astype(BF16)
            qv, k_t, f_cum = odd_in(xf, mod_l, w_qv, w_kt, wf_hi, wf_lo, b_f, seq=seq)
            f_keys = f_cum[:, :N_HEADS_ATTN].reshape(batch, seq, N_HEADS_ATTN)
            f_keys = jnp.transpose(f_keys, (0, 2, 1)).reshape(batch * N_HEADS_ATTN, 1, seq)
            attn = attention(qv, k_t, f_keys, batch=batch, seq=seq)
            xf = proj_ln(attn, od_w_out[o].astype(BF16), xf, mod_l, g0, b0,
                         seq=seq, alpha=alpha, gate_row=2)

        keys = peer_keys[l].reshape(2 * PEER_HEADS, N_KEYS, -1).astype(BF16)
        q, h2 = mod_matmul(xf, mod_l, peer_wq[l].astype(BF16), seq=seq, shift_row=3, scale_row=4,
                           out_dtype=BF16, emit_h=True)
        gmat = peer_route(q, keys, seq=seq)
        y = peer_dense(h2, gmat, u_bf, v_bf, layer=l, seq=seq)
        xf = residual_ln(xf, y, mod_l, g1, b1, seq=seq, alpha=alpha, gate_row=5)
    return xf.reshape(batch, seq, d)
```

```python
import functools
import math

import jax
import jax.numpy as jnp
from jax import lax
from jax.experimental import pallas as pl
from jax.experimental.pallas import tpu as pltpu

F32 = jnp.float32
BF16 = jnp.bfloat16
I32 = jnp.int32

LN_EPS = 1e-5
POOL_WINDOWS = (2, 4, 8, 16)
CONV_WIDTH = 4
LRU_BLOCKS = 8
LRU_C = 8.0
N_HEADS_ATTN = 16
PEER_HEADS = 8
PEER_TOPK = 16
N_KEYS = 128

V7X_VMEM_BYTES = 64 * 1024 * 1024
VMEM_LIMIT = V7X_VMEM_BYTES * 7 // 8
SUBLANES = 8
LANES = 128
NEG_INF = float("-inf")
LOG2E = math.log2(math.e)


def _params(sem):
    return pltpu.CompilerParams(dimension_semantics=sem, vmem_limit_bytes=VMEM_LIMIT)


def _resident(shape, index_map):
    return pl.BlockSpec(shape, index_map, pipeline_mode=pl.Buffered(1))


def _gelu(x):
    c = math.sqrt(2.0 / math.pi)
    return 0.5 * x * (1.0 + jnp.tanh(c * (x + 0.044715 * (x * x * x))))


def _log_sigmoid(x):
    return jnp.minimum(x, 0.0) - jnp.log1p(jnp.exp(-jnp.abs(x)))


def _sigmoid(x):
    return 0.5 * jnp.tanh(0.5 * x) + 0.5


def _layer_norm(r, g, b):
    mu = jnp.mean(r, axis=-1, keepdims=True)
    d = r - mu
    var = jnp.mean(d * d, axis=-1, keepdims=True)
    return d * lax.rsqrt(var + LN_EPS) * g + b


def _dot_nt(a, b):
    return lax.dot_general(a, b, (((1,), (1,)), ((), ())), preferred_element_type=F32)


def _cast_kernel(w_ref, o_ref):
    o_ref[...] = w_ref[...].astype(o_ref.dtype)


def cast_bf16(w2d, *, tm=1024):
    rows, cols = w2d.shape
    return pl.pallas_call(
        _cast_kernel,
        grid=(rows // tm,),
        in_specs=[pl.BlockSpec((tm, cols), lambda i: (i, 0))],
        out_specs=pl.BlockSpec((tm, cols), lambda i: (i, 0)),
        out_shape=jax.ShapeDtypeStruct((rows, cols), BF16),
        compiler_params=_params(("arbitrary",)),
        name="cast_bf16",
    )(w2d)


def _ada_kernel(c_ref, w_ref, b_ref, o_ref):
    c = c_ref[...]
    ca = c * _sigmoid(c)
    o_ref[...] = jnp.dot(ca, w_ref[...], preferred_element_type=F32) + b_ref[...]


def ada_mod(c_pad, ada_w, ada_b):
    depth, d, n6 = ada_w.shape
    rows = c_pad.shape[0]
    tn = 1024
    return pl.pallas_call(
        _ada_kernel,
        grid=(depth, n6 // tn),
        in_specs=[
            pl.BlockSpec((rows, d), lambda l, j: (0, 0)),
            pl.BlockSpec((None, d, tn), lambda l, j: (l, 0, j)),
            pl.BlockSpec((None, 1, tn), lambda l, j: (l, 0, j)),
        ],
        out_specs=pl.BlockSpec((None, rows, tn), lambda l, j: (l, 0, j)),
        out_shape=jax.ShapeDtypeStruct((depth, rows, n6), F32),
        compiler_params=_params(("arbitrary", "arbitrary")),
        name="ada_mod",
    )(c_pad, ada_w, ada_b.reshape(depth, 1, n6))


def _mod_matmul_kernel(x_ref, mod_ref, w_ref, o_ref, h_ref, *, shift_row, scale_row):
    @pl.when(pl.program_id(1) == 0)
    def _():
        sh = mod_ref[shift_row:shift_row + 1, :]
        sc = mod_ref[scale_row:scale_row + 1, :]
        h_ref[...] = (x_ref[...] * (1.0 + sc) + sh).astype(BF16)

    o_ref[...] = jnp.dot(h_ref[...], w_ref[...], preferred_element_type=F32).astype(o_ref.dtype)


def mod_matmul(x2d, mod_l, w_bf16, *, seq, shift_row, scale_row, out_dtype, emit_h=False,
               tm=1024, tn=1024):
    n, d = x2d.shape
    nout = w_bf16.shape[1]
    tm = min(tm, seq)
    blocks_per_batch = seq // tm
    kern = functools.partial(_mod_matmul_kernel, shift_row=shift_row, scale_row=scale_row)
    out_specs = [pl.BlockSpec((tm, tn), lambda i, j: (i, j))]
    out_shape = [jax.ShapeDtypeStruct((n, nout), out_dtype)]
    scratch = []
    if emit_h:
        out_specs.append(pl.BlockSpec((tm, d), lambda i, j: (i, 0)))
        out_shape.append(jax.ShapeDtypeStruct((n, d), BF16))
    else:
        scratch.append(pltpu.VMEM((tm, d), BF16))
    res = pl.pallas_call(
        kern,
        grid=(n // tm, nout // tn),
        in_specs=[
            pl.BlockSpec((tm, d), lambda i, j: (i, 0)),
            pl.BlockSpec((None, 6, d), lambda i, j: (i // blocks_per_batch, 0, 0)),
            pl.BlockSpec((d, tn), lambda i, j: (0, j)),
        ],
        out_specs=out_specs,
        out_shape=out_shape,
        scratch_shapes=scratch,
        compiler_params=_params(("arbitrary", "arbitrary")),
        name="mod_matmul",
    )(x2d, mod_l, w_bf16)
    return res if emit_h else res[0]


def _cumsum_rows(x):
    rows = x.shape[0]
    ridx = lax.broadcasted_iota(I32, x.shape, 0)
    d = 1
    while d < rows:
        x = x + jnp.where(ridx >= d, pltpu.roll(x, d, 0), 0.0)
        d *= 2
    return x


def _odd_in_kernel(x_ref, mod_ref, w_ref, wkt_ref, wfh_ref, wfl_ref, bf_ref, o_ref, kt_ref, f_ref,
                   h_ref, carry_ref, *, blocks_per_batch, nb, q_scale):
    i = pl.program_id(0)
    j = pl.program_id(1)

    @pl.when(j == 0)
    def _():
        sh = mod_ref[0:1, :]
        sc = mod_ref[1:2, :]
        h = x_ref[...] * (1.0 + sc) + sh
        h_hi = h.astype(BF16)
        h_ref[...] = h_hi
        h_lo = (h - h_hi.astype(F32)).astype(BF16)
        zf = (jnp.dot(h_hi, wfh_ref[...], preferred_element_type=F32)
              + jnp.dot(h_lo, wfh_ref[...], preferred_element_type=F32)
              + jnp.dot(h_hi, wfl_ref[...], preferred_element_type=F32))
        logf = _log_sigmoid(zf + bf_ref[...])

        @pl.when(i % blocks_per_batch == 0)
        def _():
            carry_ref[...] = jnp.zeros_like(carry_ref)

        cs = _cumsum_rows(logf) + carry_ref[0:1, :]
        f_ref[...] = cs
        carry_ref[...] = jnp.broadcast_to(cs[cs.shape[0] - 1:, :], carry_ref.shape)

    is_key = jnp.logical_and(j >= nb, j < 2 * nb)

    @pl.when(jnp.logical_not(is_key))
    def _():
        z = jnp.dot(h_ref[...], w_ref[...], preferred_element_type=F32)
        scale = jnp.where(j < nb, q_scale, 1.0)
        o_ref[...] = (z * scale).astype(o_ref.dtype)

    @pl.when(is_key)
    def _():
        kt_ref[...] = _dot_nt(wkt_ref[...], h_ref[...]).astype(kt_ref.dtype)


def odd_in(x2d, mod_l, w_qv, w_kt, wf_hi, wf_lo, b_f_pad, *, seq, tm=1024, tn=1024):
    n, d = x2d.shape
    d_mix = w_kt.shape[0]
    tm = min(tm, seq)
    blocks_per_batch = seq // tm
    head_dim = d_mix // N_HEADS_ATTN
    nb = d_mix // tn
    kern = functools.partial(_odd_in_kernel, blocks_per_batch=blocks_per_batch, nb=nb,
                             q_scale=head_dim ** -0.5 * LOG2E)

    def qv_block(j):
        return jnp.where(j < nb, j, jnp.where(j < 2 * nb, nb - 1, j - nb))

    def k_block(j):
        return jnp.clip(j - nb, 0, nb - 1)

    return pl.pallas_call(
        kern,
        grid=(n // tm, 3 * nb),
        in_specs=[
            pl.BlockSpec((tm, d), lambda i, j: (i, 0)),
            pl.BlockSpec((None, 6, d), lambda i, j: (i // blocks_per_batch, 0, 0)),
            pl.BlockSpec((d, tn), lambda i, j: (0, qv_block(j))),
            pl.BlockSpec((tn, d), lambda i, j: (k_block(j), 0)),
            _resident((d, LANES), lambda i, j: (0, 0)),
            _resident((d, LANES), lambda i, j: (0, 0)),
            _resident((1, LANES), lambda i, j: (0, 0)),
        ],
        out_specs=[
            pl.BlockSpec((tm, tn), lambda i, j: (i, qv_block(j))),
            pl.BlockSpec((tn, tm), lambda i, j: (k_block(j), i)),
            pl.BlockSpec((tm, LANES), lambda i, j: (i, 0)),
        ],
        out_shape=[
            jax.ShapeDtypeStruct((n, 2 * d_mix), BF16),
            jax.ShapeDtypeStruct((d_mix, n), BF16),
            jax.ShapeDtypeStruct((n, LANES), F32),
        ],
        scratch_shapes=[pltpu.VMEM((tm, d), BF16), pltpu.VMEM((SUBLANES, LANES), F32)],
        compiler_params=_params(("arbitrary", "arbitrary")),
        name="odd_in",
    )(x2d, mod_l, w_qv, w_kt, wf_hi, wf_lo, b_f_pad)


def _attn_kernel(q_ref, kt_ref, v_ref, fk_ref, o_ref, sa_ref, sb_ref, vaug_ref, pb_ref, *, tq, tk):
    qi = pl.program_id(2)
    f_ref0 = fk_ref[0:1, pl.ds(pl.multiple_of(qi * tq, tq), tk)][:, 0:1]
    dh = q_ref.shape[1]

    @pl.when(qi == 0)
    def _():
        vaug_ref[:, 0:dh] = v_ref[...]
        vaug_ref[:, dh:2 * dh] = jnp.ones((v_ref.shape[0], dh), BF16)
    per_q = tq // tk
    n_full = qi * per_q

    def scores(c, s_ref):
        start = pl.multiple_of(c * tk, tk)
        s = jnp.dot(q_ref[...], kt_ref[:, pl.ds(start, tk)], preferred_element_type=F32)
        s_ref[...] = s + (f_ref0 - fk_ref[0:1, pl.ds(start, tk)]) * LOG2E

    def values(c, p):
        start = pl.multiple_of(c * tk, tk)
        return jnp.dot(p, vaug_ref[pl.ds(start, tk), :], preferred_element_type=F32)

    def softmax_step(s_ref, m, diag_offset=None):
        if diag_offset is not None:
            row = lax.broadcasted_iota(I32, (tq, tk), 0)
            col = lax.broadcasted_iota(I32, (tq, tk), 1) + diag_offset
            s_ref[...] = jnp.where(col <= row, s_ref[...], NEG_INF)
        m_new = jnp.maximum(m, jnp.max(s_ref[...], axis=-1, keepdims=True))
        alpha = jnp.exp2(m - m_new)
        p = jnp.exp2(s_ref[...] - m_new).astype(BF16)
        return m_new, alpha, p

    def update(c, s_ref, carry, diag_offset=None):
        m, acc = carry
        m, alpha, p = softmax_step(s_ref, m, diag_offset)
        return m, alpha * acc + values(c, p)

    pb_ref[...] = jnp.zeros(pb_ref.shape, BF16)
    scores(0, sa_ref)

    def group(c0, count, carry):
        m, acc = carry
        acc = acc + values(jnp.maximum(c0 - 1, 0), pb_ref[...])
        bufs = (sa_ref, sb_ref)
        for t in range(count):
            scores(c0 + t + 1, bufs[(t + 1) % 2])
            m, alpha, p = softmax_step(bufs[t % 2], m)
            if t + 1 < count:
                acc = alpha * acc + values(c0 + t, p)
            else:
                pb_ref[...] = p
                acc = alpha * acc
        return m, acc

    init = (jnp.full((tq, 1), NEG_INF, F32), jnp.zeros((tq, 2 * dh), F32))
    carry = init
    done = 0
    for count in (8, 4, 2):
        steps = (n_full - done) // count
        carry = lax.fori_loop(
            0, steps, lambda p, cr, base=done, count=count: group(base + count * p, count, cr), carry)
        done = done + count * steps
    pending = jnp.maximum(done - 1, 0)

    def flush(carry):
        m, acc = carry
        return m, acc + values(pending, pb_ref[...])

    def diagonal(carry, bufs):
        for dchunk in range(per_q):
            if dchunk + 1 < per_q:
                scores(n_full + dchunk + 1, bufs[(dchunk + 1) % 2])
            carry = update(n_full + dchunk, bufs[dchunk % 2], carry, diag_offset=dchunk * tk)
        return carry

    def even_tail(carry):
        return diagonal(flush(carry), (sa_ref, sb_ref))

    def odd_tail(carry):
        scores(n_full, sb_ref)
        carry = update(n_full - 1, sa_ref, flush(carry))
        return diagonal(carry, (sb_ref, sa_ref))

    if per_q % 2 == 0:
        m, acc = even_tail(carry)
    else:
        m, acc = lax.cond(n_full % 2 == 1, odd_tail, even_tail, carry)
    o_ref[...] = (acc[:, 0:dh] / acc[:, dh:dh + 1]).astype(o_ref.dtype)


def attention(qv, k_t, f_keys, *, batch, seq, tq=512, tk=512):
    n = qv.shape[0]
    d_mix = qv.shape[1] // 2
    heads = N_HEADS_ATTN
    dh = d_mix // heads
    tq = min(tq, seq)
    tk = min(tk, tq)
    nq = seq // tq
    kern = functools.partial(_attn_kernel, tq=tq, tk=tk)
    return pl.pallas_call(
        kern,
        grid=(batch, heads, nq),
        in_specs=[
            pl.BlockSpec((tq, dh), lambda b, h, i: (b * nq + i, h)),
            pl.BlockSpec((dh, seq), lambda b, h, i: (h, b)),
            pl.BlockSpec((seq, dh), lambda b, h, i: (b, heads + h)),
            pl.BlockSpec((None, 1, seq), lambda b, h, i: (b * heads + h, 0, 0)),
        ],
        out_specs=pl.BlockSpec((tq, dh), lambda b, h, i: (b * nq + i, h)),
        out_shape=jax.ShapeDtypeStruct((n, d_mix), BF16),
        scratch_shapes=[pltpu.VMEM((tq, tk), F32), pltpu.VMEM((tq, tk), F32),
                        pltpu.VMEM((seq, 2 * dh), BF16), pltpu.VMEM((tq, tk), BF16)],
        compiler_params=_params(("arbitrary", "arbitrary", "arbitrary")),
        name="attention",
    )(qv, k_t, qv, f_keys)


HALO = 16


def _even_mix_kernel(z_ref, wpool_ref, pscale_ref, convw_ref, convb_ref, wa_ref, ba_ref,
                     wx_ref, bx_ref, lam_ref, o_ref, pbuf, cbuf, hstate, *, ts, d_pool, d_lru):
    sb = pl.program_id(1)
    group = d_pool // len(POOL_WINDOWS)
    blk = d_lru // LRU_BLOCKS

    @pl.when(sb == 0)
    def _():
        pbuf[0:HALO, :] = jnp.zeros((HALO, d_pool), F32)
        cbuf[0:HALO, :] = jnp.zeros((HALO, d_lru), F32)
        hstate[...] = jnp.zeros_like(hstate)

    up = z_ref[:, 0:d_pool]
    pbuf[HALO:HALO + ts, :] = up
    pos = sb * ts + lax.broadcasted_iota(I32, (ts, group), 0)
    ya = []
    for g, w in enumerate(POOL_WINDOWS):
        lo, hi = g * group, (g + 1) * group
        need = w - 1
        cur = pbuf[HALO - need:HALO + ts, lo:hi]
        span = 1
        while span < w:
            rows = cur.shape[0] - span
            cur = cur[span:span + rows, :] + cur[0:rows, :]
            span *= 2
        cnt = jnp.minimum(pos + 1, w).astype(F32)
        pooled = cur / cnt - up[:, lo:hi]
        y = jnp.dot(pooled.astype(BF16), wpool_ref[g], preferred_element_type=F32)
        ya.append(y * pscale_ref[0:1, lo:hi])
    o_ref[:, 0:d_pool] = jnp.concatenate(ya, axis=-1).astype(o_ref.dtype)
    pbuf[0:HALO, :] = pbuf[ts:ts + HALO, :]

    ul = z_ref[:, d_pool:d_pool + d_lru]
    ug = z_ref[:, d_pool + d_lru:d_pool + 2 * d_lru]
    cbuf[HALO:HALO + ts, :] = ul
    xc = jnp.broadcast_to(convb_ref[0:1, :], (ts, d_lru))
    for k in range(CONV_WIDTH):
        off = HALO - (CONV_WIDTH - 1) + k
        xc = xc + cbuf[off:off + ts, :] * convw_ref[k:k + 1, :]
    cbuf[0:HALO, :] = cbuf[ts:ts + HALO, :]

    xb = xc.astype(BF16)
    ra, ia = [], []
    for hb in range(LRU_BLOCKS):
        xs = xb[:, hb * blk:(hb + 1) * blk]
        ra.append(jnp.dot(xs, wa_ref[hb], preferred_element_type=F32))
        ia.append(jnp.dot(xs, wx_ref[hb], preferred_element_type=F32))
    r_gate = _sigmoid(jnp.concatenate(ra, axis=-1) + ba_ref[0:1, :])
    i_gate = _sigmoid(jnp.concatenate(ia, axis=-1) + bx_ref[0:1, :])
    log_a = LRU_C * r_gate * _log_sigmoid(lam_ref[0:1, :])
    a = jnp.exp(log_a)
    m2 = -jnp.tanh(log_a) * (1.0 + a * a)
    mult = jnp.where(m2 > 0.0, m2 * lax.rsqrt(m2), 0.0)
    b = mult * i_gate * xc

    n_grp = ts // SUBLANES
    a = a.reshape(n_grp, SUBLANES, d_lru)
    b = b.reshape(n_grp, SUBLANES, d_lru)
    rsub = lax.broadcasted_iota(I32, (n_grp, SUBLANES, d_lru), 1)
    d = 1
    while d < SUBLANES:
        keep = rsub >= d
        a_sh = jnp.where(keep, pltpu.roll(a, d, 1), 1.0)
        b_sh = jnp.where(keep, pltpu.roll(b, d, 1), 0.0)
        b = a * b_sh + b
        a = a * a_sh
        d *= 2
    state = hstate[0:1, :]
    groups = []
    for v in range(n_grp):
        hv = b[v] + a[v] * state
        state = hv[SUBLANES - 1:SUBLANES, :]
        groups.append(hv)
    h = jnp.concatenate(groups, axis=0)
    hstate[...] = jnp.broadcast_to(state, hstate.shape)
    o_ref[:, d_pool:d_pool + d_lru] = (h * _gelu(ug)).astype(o_ref.dtype)


def even_mix(z3d, w_pool, pool_scale, conv_w, conv_b, w_a, b_a, w_x, b_x, lam, *, ts=256):
    batch, seq, dz = z3d.shape
    d_pool = pool_scale.shape[-1]
    d_lru = lam.shape[-1]
    ts = min(ts, seq)
    group = d_pool // len(POOL_WINDOWS)
    blk = d_lru // LRU_BLOCKS
    kern = functools.partial(_even_mix_kernel, ts=ts, d_pool=d_pool, d_lru=d_lru)
    const2 = lambda b, s: (0, 0)
    const3 = lambda b, s: (0, 0, 0)
    return pl.pallas_call(
        kern,
        grid=(batch, seq // ts),
        in_specs=[
            pl.BlockSpec((None, ts, dz), lambda b, s: (b, s, 0)),
            _resident((len(POOL_WINDOWS), group, group), const3),
            _resident((1, d_pool), const2),
            _resident((CONV_WIDTH, d_lru), const2),
            _resident((1, d_lru), const2),
            _resident((LRU_BLOCKS, blk, blk), const3),
            _resident((1, d_lru), const2),
            _resident((LRU_BLOCKS, blk, blk), const3),
            _resident((1, d_lru), const2),
            _resident((1, d_lru), const2),
        ],
        out_specs=pl.BlockSpec((None, ts, d_pool + d_lru), lambda b, s: (b, s, 0)),
        out_shape=jax.ShapeDtypeStruct((batch, seq, d_pool + d_lru), BF16),
        scratch_shapes=[
            pltpu.VMEM((HALO + ts, d_pool), F32),
            pltpu.VMEM((HALO + ts, d_lru), F32),
            pltpu.VMEM((8, d_lru), F32),
        ],
        compiler_params=_params(("arbitrary", "arbitrary")),
        name="even_mix",
    )(z3d, w_pool, pool_scale, conv_w, conv_b, w_a, b_a, w_x, b_x, lam)


def _proj_ln_kernel(a_ref, w_ref, x_ref, mod_ref, g_ref, b_ref, o_ref, *, alpha, gate_row):
    y = jnp.dot(a_ref[...], w_ref[...], preferred_element_type=F32)
    gate = mod_ref[gate_row:gate_row + 1, :]
    r = alpha * x_ref[...] + (1.0 + gate) * y
    o_ref[...] = _layer_norm(r, g_ref[...], b_ref[...])


def proj_ln(a2d, w_bf16, x2d, mod_l, ln_g, ln_b, *, seq, alpha, gate_row, tm=512):
    n, d = x2d.shape
    k = a2d.shape[1]
    tm = min(tm, seq)
    blocks_per_batch = seq // tm
    kern = functools.partial(_proj_ln_kernel, alpha=alpha, gate_row=gate_row)
    return pl.pallas_call(
        kern,
        grid=(n // tm,),
        in_specs=[
            pl.BlockSpec((tm, k), lambda i: (i, 0)),
            _resident((k, d), lambda i: (0, 0)),
            pl.BlockSpec((tm, d), lambda i: (i, 0)),
            pl.BlockSpec((None, 6, d), lambda i: (i // blocks_per_batch, 0, 0)),
            _resident((1, d), lambda i: (0, 0)),
            _resident((1, d), lambda i: (0, 0)),
        ],
        out_specs=pl.BlockSpec((tm, d), lambda i: (i, 0)),
        out_shape=jax.ShapeDtypeStruct((n, d), F32),
        compiler_params=_params(("arbitrary",)),
        name="proj_ln",
    )(a2d, w_bf16, x2d, mod_l, ln_g, ln_b)


def _top16_rows(s, kidx, sentinel):
    t = s.shape[1]
    r16 = lax.broadcasted_iota(I32, (PEER_TOPK, t), 0)
    vals = jnp.zeros((PEER_TOPK, t), F32)
    idxs = jnp.zeros((PEER_TOPK, t), I32)
    for r in range(PEER_TOPK):
        m = jnp.max(s, axis=0, keepdims=True)
        am = jnp.min(jnp.where(s == m, kidx, sentinel), axis=0, keepdims=True)
        s = jnp.where(kidx == am, NEG_INF, s)
        vals = jnp.where(r16 == r, m, vals)
        idxs = jnp.where(r16 == r, am, idxs)
    return vals, idxs


def _take_rows16(table, sel):
    out = jnp.zeros(sel.shape, table.dtype)
    for s in range(PEER_TOPK):
        out = jnp.where(sel == s, table[s:s + 1, :], out)
    return out


TOPK_PER_STEP = 4
BUILD_UNROLL = 64


def _peer_route_kernel(q_ref, keys_ref, g_ref,
                       gs_ref, sc_ref, val_ref, idx_ref, e_ref, w_ref, et_ref, wt_ref, *, tr):
    nhp = 2 * PEER_HEADS
    for hp in range(nhp):
        sc_ref[hp] = _dot_nt(keys_ref[hp], q_ref[:, hp * N_KEYS:(hp + 1) * N_KEYS])

    kidx = lax.broadcasted_iota(I32, (N_KEYS, tr), 0)

    def first_topk(step, carry):
        for sub in range(TOPK_PER_STEP):
            hp = TOPK_PER_STEP * step + sub
            vals, idxs = _top16_rows(sc_ref[hp], kidx, N_KEYS)
            val_ref[hp] = vals
            idx_ref[hp] = idxs
        return carry

    lax.fori_loop(0, nhp // TOPK_PER_STEP, first_topk, 0)

    i16 = lax.broadcasted_iota(I32, (16, tr), 0)
    i8 = lax.broadcasted_iota(I32, (8, tr), 0)
    flat = jnp.concatenate([
        i16 * 16, i8 * 16 + 1, i8 * 16 + 2, i8 * 16 + 3,
        i16, 16 + i8, 32 + i8,
    ], axis=0)
    valid = jnp.concatenate([
        i16 < 16, i8 < 8, i8 < 5, i8 < 4,
        i16 >= 4, i8 >= 4, i8 == 4,
    ], axis=0)

    def second_topk(step, carry):
        for sub in range(TOPK_PER_STEP):
            hd = TOPK_PER_STEP * step + sub
            s1 = val_ref[2 * hd]
            s2 = val_ref[2 * hd + 1]
            cand = jnp.concatenate([
                s1 + s2[0:1, :], s1[0:8, :] + s2[1:2, :], s1[0:8, :] + s2[2:3, :],
                s1[0:8, :] + s2[3:4, :],
                s1[0:1, :] + s2, s1[1:2, :] + s2[0:8, :], s1[2:3, :] + s2[0:8, :],
            ], axis=0)
            cand = jnp.where(valid, cand, NEG_INF)
            top, fsel = _top16_rows(cand, flat, PEER_TOPK * PEER_TOPK)
            e = jnp.exp(top - top[0:1, :])
            gate = e / jnp.sum(e, axis=0, keepdims=True)
            a = _take_rows16(idx_ref[2 * hd], fsel >> 4)
            b = _take_rows16(idx_ref[2 * hd + 1], fsel & (PEER_TOPK - 1))
            row0 = pl.multiple_of(hd * PEER_TOPK, PEER_TOPK)
            e_ref[pl.ds(row0, PEER_TOPK), :] = a * N_KEYS + b
            w_ref[pl.ds(row0, PEER_TOPK), :] = gate
        return carry

    lax.fori_loop(0, PEER_HEADS // TOPK_PER_STEP, second_topk, 0)

    n_act = PEER_HEADS * PEER_TOPK
    for c in range(tr // n_act):
        et_ref[c * n_act:(c + 1) * n_act, :] = e_ref[:, c * n_act:(c + 1) * n_act].T
        wt_ref[c * n_act:(c + 1) * n_act, :] = w_ref[:, c * n_act:(c + 1) * n_act].T

    iota_rows = lax.broadcasted_iota(I32, (N_KEYS, n_act), 0)
    sub_iota = lax.broadcasted_iota(I32, (SUBLANES, N_KEYS), 0)

    def build(step, carry):
        base = pl.multiple_of(step * BUILD_UNROLL, BUILD_UNROLL)
        erows = et_ref[pl.ds(base, BUILD_UNROLL), :]
        wrows = wt_ref[pl.ds(base, BUILD_UNROLL), :]
        for u in range(BUILD_UNROLL):
            erow = erows[u:u + 1, :]
            wrow = wrows[u:u + 1, :]
            pt = jnp.where(iota_rows == (erow >> 7), wrow, 0.0).astype(BF16)
            qt = jnp.where(iota_rows == (erow & (N_KEYS - 1)), 1.0, 0.0).astype(BF16)
            gs_ref[u * N_KEYS:(u + 1) * N_KEYS, :] = _dot_nt(pt, qt)
        for gi in range(BUILD_UNROLL // SUBLANES):
            grp = step * (BUILD_UNROLL // SUBLANES) + gi
            for k in range(N_KEYS // SUBLANES):
                tiles = []
                for t in range(SUBLANES):
                    row0 = (gi * SUBLANES + t) * N_KEYS + k * SUBLANES
                    tiles.append(gs_ref[row0:row0 + SUBLANES, :])
                for dist in (4, 2, 1):
                    take_hi = (sub_iota & dist) != 0
                    for t in range(SUBLANES):
                        if t & dist:
                            continue
                        lo, hi = tiles[t], tiles[t + dist]
                        tiles[t] = jnp.where(take_hi, pltpu.roll(hi, dist, 0), lo)
                        tiles[t + dist] = jnp.where(take_hi, hi, pltpu.roll(lo, SUBLANES - dist, 0))
                for s in range(SUBLANES):
                    g_ref[grp, k * SUBLANES + s] = tiles[s]
        return carry

    lax.fori_loop(0, tr // BUILD_UNROLL, build, 0)


def peer_route(q2d, keys_bf16, *, seq, tr=256):
    n, d = q2d.shape
    tr = min(tr, seq)
    nhp = 2 * PEER_HEADS
    n_act = PEER_HEADS * PEER_TOPK
    kern = functools.partial(_peer_route_kernel, tr=tr)
    return pl.pallas_call(
        kern,
        grid=(n // tr,),
        in_specs=[
            pl.BlockSpec((tr, d), lambda i: (i, 0)),
            _resident(keys_bf16.shape, lambda i: (0, 0, 0)),
        ],
        out_specs=pl.BlockSpec((tr // SUBLANES, N_KEYS, SUBLANES, N_KEYS), lambda i: (i, 0, 0, 0)),
        out_shape=jax.ShapeDtypeStruct((n // SUBLANES, N_KEYS, SUBLANES, N_KEYS), F32),
        scratch_shapes=[
            pltpu.VMEM((BUILD_UNROLL * N_KEYS, N_KEYS), F32),
            pltpu.VMEM((nhp, N_KEYS, tr), F32),
            pltpu.VMEM((nhp, PEER_TOPK, tr), F32),
            pltpu.VMEM((nhp, PEER_TOPK, tr), I32),
            pltpu.VMEM((n_act, tr), I32),
            pltpu.VMEM((n_act, tr), F32),
            pltpu.VMEM((tr, n_act), I32),
            pltpu.VMEM((tr, n_act), F32),
        ],
        compiler_params=_params(("arbitrary",)),
        name="peer_route",
    )(q2d, keys_bf16)


DENSE_CHUNK = 256


def _peer_dense_kernel(h_ref, g_ref, u_ref, v_ref, o_ref, act_ref):
    j = pl.program_id(1)
    rows = g_ref.shape[1]
    tm = h_ref.shape[0]

    @pl.when(j == 0)
    def _():
        o_ref[...] = jnp.zeros_like(o_ref)

    per = DENSE_CHUNK // N_KEYS
    for c in range(rows // per):
        ccols = slice(c * DENSE_CHUNK, (c + 1) * DENSE_CHUNK)
        z = _dot_nt(h_ref[...], u_ref[ccols, :])
        for r in range(per):
            row = c * per + r
            gate = g_ref[:, row, :, :].reshape(tm, N_KEYS)
            act_ref[:, row * N_KEYS:(row + 1) * N_KEYS] = (
                _gelu(z[:, r * N_KEYS:(r + 1) * N_KEYS]) * gate).astype(BF16)
    half = act_ref.shape[1] // 2
    o_ref[...] += jnp.dot(act_ref[:, :half], v_ref[:half, :], preferred_element_type=F32)
    o_ref[...] += jnp.dot(act_ref[:, half:], v_ref[half:, :], preferred_element_type=F32)


def peer_dense(h2d, g3d, u_bf16, v_bf16, *, layer, seq, tm=1024, te=1024):
    n, d = h2d.shape
    n_exp = v_bf16.shape[1]
    tm = min(tm, seq)
    return pl.pallas_call(
        _peer_dense_kernel,
        grid=(n // tm, n_exp // te),
        in_specs=[
            pl.BlockSpec((tm, d), lambda i, j: (i, 0), pipeline_mode=pl.Buffered(1)),
            pl.BlockSpec((tm // SUBLANES, te // N_KEYS, SUBLANES, N_KEYS), lambda i, j: (i, j, 0, 0)),
            pl.BlockSpec((None, te, d), lambda i, j: (layer, j, 0)),
            pl.BlockSpec((None, te, d), lambda i, j: (layer, j, 0)),
        ],
        out_specs=pl.BlockSpec((tm, d), lambda i, j: (i, 0)),
        out_shape=jax.ShapeDtypeStruct((n, d), F32),
        scratch_shapes=[pltpu.VMEM((tm, te), BF16)],
        compiler_params=_params(("arbitrary", "arbitrary")),
        name="peer_dense",
    )(h2d, g3d, u_bf16, v_bf16)


def _residual_ln_kernel(x_ref, y_ref, mod_ref, g_ref, b_ref, o_ref, *, alpha, gate_row):
    gate = mod_ref[gate_row:gate_row + 1, :]
    r = alpha * x_ref[...] + (1.0 + gate) * y_ref[...]
    o_ref[...] = _layer_norm(r, g_ref[...], b_ref[...])


def residual_ln(x2d, y2d, mod_l, ln_g, ln_b, *, seq, alpha, gate_row, tm=512):
    n, d = x2d.shape
    tm = min(tm, seq)
    blocks_per_batch = seq // tm
    kern = functools.partial(_residual_ln_kernel, alpha=alpha, gate_row=gate_row)
    return pl.pallas_call(
        kern,
        grid=(n // tm,),
        in_specs=[
            pl.BlockSpec((tm, d), lambda i: (i, 0)),
            pl.BlockSpec((tm, d), lambda i: (i, 0)),
            pl.BlockSpec((None, 6, d), lambda i: (i // blocks_per_batch, 0, 0)),
            _resident((1, d), lambda i: (0, 0)),
            _resident((1, d), lambda i: (0, 0)),
        ],
        out_specs=pl.BlockSpec((tm, d), lambda i: (i, 0)),
        out_shape=jax.ShapeDtypeStruct((n, d), F32),
        compiler_params=_params(("arbitrary",)),
        name="residual_ln",
    )(x2d, y2d, mod_l, ln_g, ln_b)


def kernel(x, c, ada_w, ada_b, ln_g, ln_b, peer_wq, peer_keys, peer_u, peer_v, ev_w_in, ev_w_pool,
           ev_pool_scale, ev_conv_w, ev_conv_b, ev_w_a, ev_b_a, ev_w_x, ev_b_x, ev_lam, ev_w_out,
           od_w_in, od_b_f, od_w_out):
    batch, seq, d = x.shape
    depth = ada_w.shape[0]
    n = batch * seq
    alpha = (2.0 * depth) ** 0.25
    d_mix = od_w_out.shape[1]

    c_pad = jnp.zeros((8, d), F32).at[:batch].set(c)
    mod = ada_mod(c_pad, ada_w, ada_b)[:, :batch].reshape(depth, batch, 6, d)

    n_exp = peer_u.shape[1]
    u_bf = cast_bf16(peer_u.reshape(depth * n_exp, d)).reshape(depth, n_exp, d)
    v_bf = cast_bf16(peer_v.reshape(depth * n_exp, d)).reshape(depth, n_exp, d)

    xf = x.reshape(n, d)
    for l in range(depth):
        mod_l = mod[l]
        g0 = ln_g[l, 0].reshape(1, d)
        b0 = ln_b[l, 0].reshape(1, d)
        g1 = ln_g[l, 1].reshape(1, d)
        b1 = ln_b[l, 1].reshape(1, d)
        if l % 2 == 0:
            e = l // 2
            z = mod_matmul(xf, mod_l, ev_w_in[e].astype(BF16), seq=seq, shift_row=0, scale_row=1,
                           out_dtype=F32)
            mixed = even_mix(
                z.reshape(batch, seq, -1), ev_w_pool[e].astype(BF16), ev_pool_scale[e].reshape(1, -1),
                ev_conv_w[e], ev_conv_b[e].reshape(1, -1), ev_w_a[e].astype(BF16),
                ev_b_a[e].reshape(1, -1), ev_w_x[e].astype(BF16), ev_b_x[e].reshape(1, -1),
                ev_lam[e].reshape(1, -1))
            xf = proj_ln(mixed.reshape(n, -1), ev_w_out[e].astype(BF16), xf, mod_l, g0, b0,
                         seq=seq, alpha=alpha, gate_row=2)
        else:
            o = l // 2
            w_in = od_w_in[o]
            w_f = jnp.zeros((d, LANES), F32).at[:, :N_HEADS_ATTN].set(w_in[:, 3 * d_mix:])
            wf_hi = w_f.astype(BF16)
            wf_lo = (w_f - wf_hi.astype(F32)).astype(BF16)
            b_f = jnp.zeros((1, LANES), F32).at[0, :N_HEADS_ATTN].set(od_b_f[o])
            w_qv = jnp.concatenate([w_in[:, :d_mix], w_in[:, 2 * d_mix:3 * d_mix]], axis=1).astype(BF16)
            w_kt = w_in[:, d_mix:2 * d_mix].T.astype(BF16)
            qv, k_t, f_cum = odd_in(xf, mod_l, w_qv, w_kt, wf_hi, wf_lo, b_f, seq=seq)
            f_keys = f_cum[:, :N_HEADS_ATTN].reshape(batch, seq, N_HEADS_ATTN)
            f_keys = jnp.transpose(f_keys, (0, 2, 1)).reshape(batch * N_HEADS_ATTN, 1, seq)
            attn = attention(qv, k_t, f_keys, batch=batch, seq=seq)
            xf = proj_ln(attn, od_w_out[o].astype(BF16), xf, mod_l, g0, b0,
                         seq=seq, alpha=alpha, gate_row=2)

        keys = peer_keys[l].reshape(2 * PEER_HEADS, N_KEYS, -1).astype(BF16)
        q, h2 = mod_matmul(xf, mod_l, peer_wq[l].astype(BF16), seq=seq, shift_row=3, scale_row=4,
                           out_dtype=BF16, emit_h=True)
        gmat = peer_route(q, keys, seq=seq)
        y = peer_dense(h2, gmat, u_bf, v_bf, layer=l, seq=seq)
        xf = residual_ln(xf, y, mod_l, g1, b1, seq=seq, alpha=alpha, gate_row=5)
    return xf.reshape(batch, seq, d)
```

```python
import functools
import math

import jax
import jax.numpy as jnp
from jax import lax
from jax.experimental import pallas as pl
from jax.experimental.pallas import tpu as pltpu

F32 = jnp.float32
BF16 = jnp.bfloat16
I32 = jnp.int32

LN_EPS = 1e-5
POOL_WINDOWS = (2, 4, 8, 16)
CONV_WIDTH = 4
LRU_BLOCKS = 8
LRU_C = 8.0
N_HEADS_ATTN = 16
PEER_HEADS = 8
PEER_TOPK = 16
N_KEYS = 128

V7X_VMEM_BYTES = 64 * 1024 * 1024
VMEM_LIMIT = V7X_VMEM_BYTES * 7 // 8
SUBLANES = 8
LANES = 128
NEG_INF = float("-inf")
LOG2E = math.log2(math.e)


def _params(sem):
    return pltpu.CompilerParams(dimension_semantics=sem, vmem_limit_bytes=VMEM_LIMIT)


def _resident(shape, index_map):
    return pl.BlockSpec(shape, index_map, pipeline_mode=pl.Buffered(1))


def _gelu(x):
    c = math.sqrt(2.0 / math.pi)
    return 0.5 * x * (1.0 + jnp.tanh(c * (x + 0.044715 * (x * x * x))))


def _log_sigmoid(x):
    return jnp.minimum(x, 0.0) - jnp.log1p(jnp.exp(-jnp.abs(x)))


def _sigmoid(x):
    return 0.5 * jnp.tanh(0.5 * x) + 0.5


def _layer_norm(r, g, b):
    mu = jnp.mean(r, axis=-1, keepdims=True)
    d = r - mu
    var = jnp.mean(d * d, axis=-1, keepdims=True)
    return d * lax.rsqrt(var + LN_EPS) * g + b


def _dot_nt(a, b):
    return lax.dot_general(a, b, (((1,), (1,)), ((), ())), preferred_element_type=F32)


def _cast_kernel(w_ref, o_ref):
    o_ref[...] = w_ref[...].astype(o_ref.dtype)


def cast_bf16(w2d, *, tm=1024):
    rows, cols = w2d.shape
    return pl.pallas_call(
        _cast_kernel,
        grid=(rows // tm,),
        in_specs=[pl.BlockSpec((tm, cols), lambda i: (i, 0))],
        out_specs=pl.BlockSpec((tm, cols), lambda i: (i, 0)),
        out_shape=jax.ShapeDtypeStruct((rows, cols), BF16),
        compiler_params=_params(("arbitrary",)),
        name="cast_bf16",
    )(w2d)


def _ada_kernel(c_ref, w_ref, b_ref, o_ref):
    c = c_ref[...]
    ca = c * _sigmoid(c)
    o_ref[...] = jnp.dot(ca, w_ref[...], preferred_element_type=F32) + b_ref[...]


def ada_mod(c_pad, ada_w, ada_b):
    depth, d, n6 = ada_w.shape
    rows = c_pad.shape[0]
    tn = 1024
    return pl.pallas_call(
        _ada_kernel,
        grid=(depth, n6 // tn),
        in_specs=[
            pl.BlockSpec((rows, d), lambda l, j: (0, 0)),
            pl.BlockSpec((None, d, tn), lambda l, j: (l, 0, j)),
            pl.BlockSpec((None, 1, tn), lambda l, j: (l, 0, j)),
        ],
        out_specs=pl.BlockSpec((None, rows, tn), lambda l, j: (l, 0, j)),
        out_shape=jax.ShapeDtypeStruct((depth, rows, n6), F32),
        compiler_params=_params(("arbitrary", "arbitrary")),
        name="ada_mod",
    )(c_pad, ada_w, ada_b.reshape(depth, 1, n6))


def _mod_matmul_kernel(x_ref, mod_ref, w_ref, o_ref, h_ref, *, shift_row, scale_row):
    @pl.when(pl.program_id(1) == 0)
    def _():
        sh = mod_ref[shift_row:shift_row + 1, :]
        sc = mod_ref[scale_row:scale_row + 1, :]
        h_ref[...] = (x_ref[...] * (1.0 + sc) + sh).astype(BF16)

    o_ref[...] = jnp.dot(h_ref[...], w_ref[...], preferred_element_type=F32).astype(o_ref.dtype)


def mod_matmul(x2d, mod_l, w_bf16, *, seq, shift_row, scale_row, out_dtype, emit_h=False,
               tm=1024, tn=1024):
    n, d = x2d.shape
    nout = w_bf16.shape[1]
    tm = min(tm, seq)
    blocks_per_batch = seq // tm
    kern = functools.partial(_mod_matmul_kernel, shift_row=shift_row, scale_row=scale_row)
    out_specs = [pl.BlockSpec((tm, tn), lambda i, j: (i, j))]
    out_shape = [jax.ShapeDtypeStruct((n, nout), out_dtype)]
    scratch = []
    if emit_h:
        out_specs.append(pl.BlockSpec((tm, d), lambda i, j: (i, 0)))
        out_shape.append(jax.ShapeDtypeStruct((n, d), BF16))
    else:
        scratch.append(pltpu.VMEM((tm, d), BF16))
    res = pl.pallas_call(
        kern,
        grid=(n // tm, nout // tn),
        in_specs=[
            pl.BlockSpec((tm, d), lambda i, j: (i, 0)),
            pl.BlockSpec((None, 6, d), lambda i, j: (i // blocks_per_batch, 0, 0)),
            pl.BlockSpec((d, tn), lambda i, j: (0, j)),
        ],
        out_specs=out_specs,
        out_shape=out_shape,
        scratch_shapes=scratch,
        compiler_params=_params(("arbitrary", "arbitrary")),
        name="mod_matmul",
    )(x2d, mod_l, w_bf16)
    return res if emit_h else res[0]


def _cumsum_rows(x):
    rows = x.shape[0]
    ridx = lax.broadcasted_iota(I32, x.shape, 0)
    d = 1
    while d < rows:
        x = x + jnp.where(ridx >= d, pltpu.roll(x, d, 0), 0.0)
        d *= 2
    return x


def _odd_in_kernel(x_ref, mod_ref, w_ref, wkt_ref, wfh_ref, wfl_ref, bf_ref, o_ref, kt_ref, f_ref,
                   h_ref, carry_ref, *, blocks_per_batch, nb, q_scale):
    i = pl.program_id(0)
    j = pl.program_id(1)

    @pl.when(j == 0)
    def _():
        sh = mod_ref[0:1, :]
        sc = mod_ref[1:2, :]
        h = x_ref[...] * (1.0 + sc) + sh
        h_hi = h.astype(BF16)
        h_ref[...] = h_hi
        h_lo = (h - h_hi.astype(F32)).astype(BF16)
        zf = (jnp.dot(h_hi, wfh_ref[...], preferred_element_type=F32)
              + jnp.dot(h_lo, wfh_ref[...], preferred_element_type=F32)
              + jnp.dot(h_hi, wfl_ref[...], preferred_element_type=F32))
        logf = _log_sigmoid(zf + bf_ref[...])

        @pl.when(i % blocks_per_batch == 0)
        def _():
            carry_ref[...] = jnp.zeros_like(carry_ref)

        cs = _cumsum_rows(logf) + carry_ref[0:1, :]
        f_ref[...] = cs
        carry_ref[...] = jnp.broadcast_to(cs[cs.shape[0] - 1:, :], carry_ref.shape)

    is_key = jnp.logical_and(j >= nb, j < 2 * nb)

    @pl.when(jnp.logical_not(is_key))
    def _():
        z = jnp.dot(h_ref[...], w_ref[...], preferred_element_type=F32)
        scale = jnp.where(j < nb, q_scale, 1.0)
        o_ref[...] = (z * scale).astype(o_ref.dtype)

    @pl.when(is_key)
    def _():
        kt_ref[...] = _dot_nt(wkt_ref[...], h_ref[...]).astype(kt_ref.dtype)


def odd_in(x2d, mod_l, w_qv, w_kt, wf_hi, wf_lo, b_f_pad, *, seq, tm=1024, tn=1024):
    n, d = x2d.shape
    d_mix = w_kt.shape[0]
    tm = min(tm, seq)
    blocks_per_batch = seq // tm
    head_dim = d_mix // N_HEADS_ATTN
    nb = d_mix // tn
    kern = functools.partial(_odd_in_kernel, blocks_per_batch=blocks_per_batch, nb=nb,
                             q_scale=head_dim ** -0.5 * LOG2E)

    def qv_block(j):
        return jnp.where(j < nb, j, jnp.where(j < 2 * nb, nb - 1, j - nb))

    def k_block(j):
        return jnp.clip(j - nb, 0, nb - 1)

    return pl.pallas_call(
        kern,
        grid=(n // tm, 3 * nb),
        in_specs=[
            pl.BlockSpec((tm, d), lambda i, j: (i, 0)),
            pl.BlockSpec((None, 6, d), lambda i, j: (i // blocks_per_batch, 0, 0)),
            pl.BlockSpec((d, tn), lambda i, j: (0, qv_block(j))),
            pl.BlockSpec((tn, d), lambda i, j: (k_block(j), 0)),
            _resident((d, LANES), lambda i, j: (0, 0)),
            _resident((d, LANES), lambda i, j: (0, 0)),
            _resident((1, LANES), lambda i, j: (0, 0)),
        ],
        out_specs=[
            pl.BlockSpec((tm, tn), lambda i, j: (i, qv_block(j))),
            pl.BlockSpec((tn, tm), lambda i, j: (k_block(j), i)),
            pl.BlockSpec((tm, LANES), lambda i, j: (i, 0)),
        ],
        out_shape=[
            jax.ShapeDtypeStruct((n, 2 * d_mix), BF16),
            jax.ShapeDtypeStruct((d_mix, n), BF16),
            jax.ShapeDtypeStruct((n, LANES), F32),
        ],
        scratch_shapes=[pltpu.VMEM((tm, d), BF16), pltpu.VMEM((SUBLANES, LANES), F32)],
        compiler_params=_params(("arbitrary", "arbitrary")),
        name="odd_in",
    )(x2d, mod_l, w_qv, w_kt, wf_hi, wf_lo, b_f_pad)


def _attn_kernel(q_ref, kt_ref, v_ref, fk_ref, o_ref, sa_ref, sb_ref, vaug_ref, pb_ref, *, tq, tk):
    qi = pl.program_id(2)
    f_ref0 = fk_ref[0:1, pl.ds(pl.multiple_of(qi * tq, tq), tk)][:, 0:1]
    dh = q_ref.shape[1]

    @pl.when(qi == 0)
    def _():
        vaug_ref[:, 0:dh] = v_ref[...]
        vaug_ref[:, dh:2 * dh] = jnp.ones((v_ref.shape[0], dh), BF16)
    per_q = tq // tk
    n_full = qi * per_q

    def scores(c, s_ref):
        start = pl.multiple_of(c * tk, tk)
        s = jnp.dot(q_ref[...], kt_ref[:, pl.ds(start, tk)], preferred_element_type=F32)
        s_ref[...] = s + (f_ref0 - fk_ref[0:1, pl.ds(start, tk)]) * LOG2E

    def values(c, p):
        start = pl.multiple_of(c * tk, tk)
        return jnp.dot(p, vaug_ref[pl.ds(start, tk), :], preferred_element_type=F32)

    def softmax_step(s_ref, m, diag_offset=None):
        if diag_offset is not None:
            row = lax.broadcasted_iota(I32, (tq, tk), 0)
            col = lax.broadcasted_iota(I32, (tq, tk), 1) + diag_offset
            s_ref[...] = jnp.where(col <= row, s_ref[...], NEG_INF)
        m_new = jnp.maximum(m, jnp.max(s_ref[...], axis=-1, keepdims=True))
        alpha = jnp.exp2(m - m_new)
        p = jnp.exp2(s_ref[...] - m_new).astype(BF16)
        return m_new, alpha, p

    def update(c, s_ref, carry, diag_offset=None):
        m, acc = carry
        m, alpha, p = softmax_step(s_ref, m, diag_offset)
        return m, alpha * acc + values(c, p)

    pb_ref[...] = jnp.zeros(pb_ref.shape, BF16)
    scores(0, sa_ref)

    def group(c0, count, carry):
        m, acc = carry
        acc = acc + values(jnp.maximum(c0 - 1, 0), pb_ref[...])
        bufs = (sa_ref, sb_ref)
        for t in range(count):
            scores(c0 + t + 1, bufs[(t + 1) % 2])
            m, alpha, p = softmax_step(bufs[t % 2], m)
            if t + 1 < count:
                acc = alpha * acc + values(c0 + t, p)
            else:
                pb_ref[...] = p
                acc = alpha * acc
        return m, acc

    init = (jnp.full((tq, 1), NEG_INF, F32), jnp.zeros((tq, 2 * dh), F32))
    carry = init
    done = 0
    for count in (8, 4, 2):
        steps = (n_full - done) // count
        carry = lax.fori_loop(
            0, steps, lambda p, cr, base=done, count=count: group(base + count * p, count, cr), carry)
        done = done + count * steps
    pending = jnp.maximum(done - 1, 0)

    def flush(carry):
        m, acc = carry
        return m, acc + values(pending, pb_ref[...])

    def diagonal(carry, bufs):
        for dchunk in range(per_q):
            if dchunk + 1 < per_q:
                scores(n_full + dchunk + 1, bufs[(dchunk + 1) % 2])
            carry = update(n_full + dchunk, bufs[dchunk % 2], carry, diag_offset=dchunk * tk)
        return carry

    def even_tail(carry):
        return diagonal(flush(carry), (sa_ref, sb_ref))

    def odd_tail(carry):
        scores(n_full, sb_ref)
        carry = update(n_full - 1, sa_ref, flush(carry))
        return diagonal(carry, (sb_ref, sa_ref))

    if per_q % 2 == 0:
        m, acc = even_tail(carry)
    else:
        m, acc = lax.cond(n_full % 2 == 1, odd_tail, even_tail, carry)
    o_ref[...] = (acc[:, 0:dh] / acc[:, dh:dh + 1]).astype(o_ref.dtype)


def attention(qv, k_t, f_keys, *, batch, seq, tq=512, tk=512):
    n = qv.shape[0]
    d_mix = qv.shape[1] // 2
    heads = N_HEADS_ATTN
    dh = d_mix // heads
    tq = min(tq, seq)
    tk = min(tk, tq)
    nq = seq // tq
    kern = functools.partial(_attn_kernel, tq=tq, tk=tk)
    return pl.pallas_call(
        kern,
        grid=(batch, heads, nq),
        in_specs=[
            pl.BlockSpec((tq, dh), lambda b, h, i: (b * nq + i, h)),
            pl.BlockSpec((dh, seq), lambda b, h, i: (h, b)),
            pl.BlockSpec((seq, dh), lambda b, h, i: (b, heads + h)),
            pl.BlockSpec((None, 1, seq), lambda b, h, i: (b * heads + h, 0, 0)),
        ],
        out_specs=pl.BlockSpec((tq, dh), lambda b, h, i: (b * nq + i, h)),
        out_shape=jax.ShapeDtypeStruct((n, d_mix), BF16),
        scratch_shapes=[pltpu.VMEM((tq, tk), F32), pltpu.VMEM((tq, tk), F32),
                        pltpu.VMEM((seq, 2 * dh), BF16), pltpu.VMEM((tq, tk), BF16)],
        compiler_params=_params(("arbitrary", "arbitrary", "arbitrary")),
        name="attention",
    )(qv, k_t, qv, f_keys)


HALO = 16


def _even_mix_kernel(z_ref, wpool_ref, pscale_ref, convw_ref, convb_ref, wa_ref, ba_ref,
                     wx_ref, bx_ref, lam_ref, o_ref, pbuf, cbuf, hstate, *, ts, d_pool, d_lru):
    sb = pl.program_id(1)
    group = d_pool // len(POOL_WINDOWS)
    blk = d_lru // LRU_BLOCKS

    @pl.when(sb == 0)
    def _():
        pbuf[0:HALO, :] = jnp.zeros((HALO, d_pool), F32)
        cbuf[0:HALO, :] = jnp.zeros((HALO, d_lru), F32)
        hstate[...] = jnp.zeros_like(hstate)

    up = z_ref[:, 0:d_pool]
    pbuf[HALO:HALO + ts, :] = up
    pos = sb * ts + lax.broadcasted_iota(I32, (ts, group), 0)
    ya = []
    for g, w in enumerate(POOL_WINDOWS):
        lo, hi = g * group, (g + 1) * group
        need = w - 1
        cur = pbuf[HALO - need:HALO + ts, lo:hi]
        span = 1
        while span < w:
            rows = cur.shape[0] - span
            cur = cur[span:span + rows, :] + cur[0:rows, :]
            span *= 2
        cnt = jnp.minimum(pos + 1, w).astype(F32)
        pooled = cur / cnt - up[:, lo:hi]
        y = jnp.dot(pooled.astype(BF16), wpool_ref[g], preferred_element_type=F32)
        ya.append(y * pscale_ref[0:1, lo:hi])
    o_ref[:, 0:d_pool] = jnp.concatenate(ya, axis=-1).astype(o_ref.dtype)
    pbuf[0:HALO, :] = pbuf[ts:ts + HALO, :]

    ul = z_ref[:, d_pool:d_pool + d_lru]
    ug = z_ref[:, d_pool + d_lru:d_pool + 2 * d_lru]
    cbuf[HALO:HALO + ts, :] = ul
    xc = jnp.broadcast_to(convb_ref[0:1, :], (ts, d_lru))
    for k in range(CONV_WIDTH):
        off = HALO - (CONV_WIDTH - 1) + k
        xc = xc + cbuf[off:off + ts, :] * convw_ref[k:k + 1, :]
    cbuf[0:HALO, :] = cbuf[ts:ts + HALO, :]

    xb = xc.astype(BF16)
    ra, ia = [], []
    for hb in range(LRU_BLOCKS):
        xs = xb[:, hb * blk:(hb + 1) * blk]
        ra.append(jnp.dot(xs, wa_ref[hb], preferred_element_type=F32))
        ia.append(jnp.dot(xs, wx_ref[hb], preferred_element_type=F32))
    r_gate = _sigmoid(jnp.concatenate(ra, axis=-1) + ba_ref[0:1, :])
    i_gate = _sigmoid(jnp.concatenate(ia, axis=-1) + bx_ref[0:1, :])
    log_a = LRU_C * r_gate * _log_sigmoid(lam_ref[0:1, :])
    a = jnp.exp(log_a)
    m2 = -jnp.tanh(log_a) * (1.0 + a * a)
    mult = jnp.where(m2 > 0.0, m2 * lax.rsqrt(m2), 0.0)
    b = mult * i_gate * xc

    n_grp = ts // SUBLANES
    a = a.reshape(n_grp, SUBLANES, d_lru)
    b = b.reshape(n_grp, SUBLANES, d_lru)
    rsub = lax.broadcasted_iota(I32, (n_grp, SUBLANES, d_lru), 1)
    d = 1
    while d < SUBLANES:
        keep = rsub >= d
        a_sh = jnp.where(keep, pltpu.roll(a, d, 1), 1.0)
        b_sh = jnp.where(keep, pltpu.roll(b, d, 1), 0.0)
        b = a * b_sh + b
        a = a * a_sh
        d *= 2
    state = hstate[0:1, :]
    groups = []
    for v in range(n_grp):
        hv = b[v] + a[v] * state
        state = hv[SUBLANES - 1:SUBLANES, :]
        groups.append(hv)
    h = jnp.concatenate(groups, axis=0)
    hstate[...] = jnp.broadcast_to(state, hstate.shape)
    o_ref[:, d_pool:d_pool + d_lru] = (h * _gelu(ug)).astype(o_ref.dtype)


def even_mix(z3d, w_pool, pool_scale, conv_w, conv_b, w_a, b_a, w_x, b_x, lam, *, ts=256):
    batch, seq, dz = z3d.shape
    d_pool = pool_scale.shape[-1]
    d_lru = lam.shape[-1]
    ts = min(ts, seq)
    group = d_pool // len(POOL_WINDOWS)
    blk = d_lru // LRU_BLOCKS
    kern = functools.partial(_even_mix_kernel, ts=ts, d_pool=d_pool, d_lru=d_lru)
    const2 = lambda b, s: (0, 0)
    const3 = lambda b, s: (0, 0, 0)
    return pl.pallas_call(
        kern,
        grid=(batch, seq // ts),
        in_specs=[
            pl.BlockSpec((None, ts, dz), lambda b, s: (b, s, 0)),
            _resident((len(POOL_WINDOWS), group, group), const3),
            _resident((1, d_pool), const2),
            _resident((CONV_WIDTH, d_lru), const2),
            _resident((1, d_lru), const2),
            _resident((LRU_BLOCKS, blk, blk), const3),
            _resident((1, d_lru), const2),
            _resident((LRU_BLOCKS, blk, blk), const3),
            _resident((1, d_lru), const2),
            _resident((1, d_lru), const2),
        ],
        out_specs=pl.BlockSpec((None, ts, d_pool + d_lru), lambda b, s: (b, s, 0)),
        out_shape=jax.ShapeDtypeStruct((batch, seq, d_pool + d_lru), BF16),
        scratch_shapes=[
            pltpu.VMEM((HALO + ts, d_pool), F32),
            pltpu.VMEM((HALO + ts, d_lru), F32),
            pltpu.VMEM((8, d_lru), F32),
        ],
        compiler_params=_params(("arbitrary", "arbitrary")),
        name="even_mix",
    )(z3d, w_pool, pool_scale, conv_w, conv_b, w_a, b_a, w_x, b_x, lam)


def _proj_ln_kernel(a_ref, w_ref, x_ref, mod_ref, g_ref, b_ref, o_ref, *, alpha, gate_row):
    y = jnp.dot(a_ref[...], w_ref[...], preferred_element_type=F32)
    gate = mod_ref[gate_row:gate_row + 1, :]
    r = alpha * x_ref[...] + (1.0 + gate) * y
    o_ref[...] = _layer_norm(r, g_ref[...], b_ref[...])


def proj_ln(a2d, w_bf16, x2d, mod_l, ln_g, ln_b, *, seq, alpha, gate_row, tm=512):
    n, d = x2d.shape
    k = a2d.shape[1]
    tm = min(tm, seq)
    blocks_per_batch = seq // tm
    kern = functools.partial(_proj_ln_kernel, alpha=alpha, gate_row=gate_row)
    return pl.pallas_call(
        kern,
        grid=(n // tm,),
        in_specs=[
            pl.BlockSpec((tm, k), lambda i: (i, 0)),
            _resident((k, d), lambda i: (0, 0)),
            pl.BlockSpec((tm, d), lambda i: (i, 0)),
            pl.BlockSpec((None, 6, d), lambda i: (i // blocks_per_batch, 0, 0)),
            _resident((1, d), lambda i: (0, 0)),
            _resident((1, d), lambda i: (0, 0)),
        ],
        out_specs=pl.BlockSpec((tm, d), lambda i: (i, 0)),
        out_shape=jax.ShapeDtypeStruct((n, d), F32),
        compiler_params=_params(("arbitrary",)),
        name="proj_ln",
    )(a2d, w_bf16, x2d, mod_l, ln_g, ln_b)


def _top16_rows(s, kidx, sentinel):
    t = s.shape[1]
    r16 = lax.broadcasted_iota(I32, (PEER_TOPK, t), 0)
    vals = jnp.zeros((PEER_TOPK, t), F32)
    idxs = jnp.zeros((PEER_TOPK, t), I32)
    for r in range(PEER_TOPK):
        m = jnp.max(s, axis=0, keepdims=True)
        am = jnp.min(jnp.where(s == m, kidx, sentinel), axis=0, keepdims=True)
        s = jnp.where(kidx == am, NEG_INF, s)
        vals = jnp.where(r16 == r, m, vals)
        idxs = jnp.where(r16 == r, am, idxs)
    return vals, idxs


def _take_rows16(table, sel):
    out = jnp.zeros(sel.shape, table.dtype)
    for s in range(PEER_TOPK):
        out = jnp.where(sel == s, table[s:s + 1, :], out)
    return out


TOPK_PER_STEP = 4
BUILD_UNROLL = 64


def _peer_route_kernel(q_ref, keys_ref, g_ref,
                       gs_ref, sc_ref, val_ref, idx_ref, e_ref, w_ref, et_ref, wt_ref, *, tr):
    nhp = 2 * PEER_HEADS
    for hp in range(nhp):
        sc_ref[hp] = _dot_nt(keys_ref[hp], q_ref[:, hp * N_KEYS:(hp + 1) * N_KEYS])

    kidx = lax.broadcasted_iota(I32, (N_KEYS, tr), 0)

    def first_topk(step, carry):
        for sub in range(TOPK_PER_STEP):
            hp = TOPK_PER_STEP * step + sub
            vals, idxs = _top16_rows(sc_ref[hp], kidx, N_KEYS)
            val_ref[hp] = vals
            idx_ref[hp] = idxs
        return carry

    lax.fori_loop(0, nhp // TOPK_PER_STEP, first_topk, 0)

    i16 = lax.broadcasted_iota(I32, (16, tr), 0)
    i8 = lax.broadcasted_iota(I32, (8, tr), 0)
    flat = jnp.concatenate([
        i16 * 16, i8 * 16 + 1, i8 * 16 + 2, i8 * 16 + 3,
        i16, 16 + i8, 32 + i8,
    ], axis=0)
    valid = jnp.concatenate([
        i16 < 16, i8 < 8, i8 < 5, i8 < 4,
        i16 >= 4, i8 >= 4, i8 == 4,
    ], axis=0)

    def second_topk(step, carry):
        for sub in range(TOPK_PER_STEP):
            hd = TOPK_PER_STEP * step + sub
            s1 = val_ref[2 * hd]
            s2 = val_ref[2 * hd + 1]
            cand = jnp.concatenate([
                s1 + s2[0:1, :], s1[0:8, :] + s2[1:2, :], s1[0:8, :] + s2[2:3, :],
                s1[0:8, :] + s2[3:4, :],
                s1[0:1, :] + s2, s1[1:2, :] + s2[0:8, :], s1[2:3, :] + s2[0:8, :],
            ], axis=0)
            cand = jnp.where(valid, cand, NEG_INF)
            top, fsel = _top16_rows(cand, flat, PEER_TOPK * PEER_TOPK)
            e = jnp.exp(top - top[0:1, :])
            gate = e / jnp.sum(e, axis=0, keepdims=True)
            a = _take_rows16(idx_ref[2 * hd], fsel >> 4)
            b = _take_rows16(idx_ref[2 * hd + 1], fsel & (PEER_TOPK - 1))
            row0 = pl.multiple_of(hd * PEER_TOPK, PEER_TOPK)
            e_ref[pl.ds(row0, PEER_TOPK), :] = a * N_KEYS + b
            w_ref[pl.ds(row0, PEER_TOPK), :] = gate
        return carry

    lax.fori_loop(0, PEER_HEADS // TOPK_PER_STEP, second_topk, 0)

    n_act = PEER_HEADS * PEER_TOPK
    for c in range(tr // n_act):
        et_ref[c * n_act:(c + 1) * n_act, :] = e_ref[:, c * n_act:(c + 1) * n_act].T
        wt_ref[c * n_act:(c + 1) * n_act, :] = w_ref[:, c * n_act:(c + 1) * n_act].T

    iota_rows = lax.broadcasted_iota(I32, (N_KEYS, n_act), 0)
    sub_iota = lax.broadcasted_iota(I32, (SUBLANES, N_KEYS), 0)

    def build(step, carry):
        base = pl.multiple_of(step * BUILD_UNROLL, BUILD_UNROLL)
        erows = et_ref[pl.ds(base, BUILD_UNROLL), :]
        wrows = wt_ref[pl.ds(base, BUILD_UNROLL), :]
        for u in range(BUILD_UNROLL):
            erow = erows[u:u + 1, :]
            wrow = wrows[u:u + 1, :]
            pt = jnp.where(iota_rows == (erow >> 7), wrow, 0.0).astype(BF16)
            qt = jnp.where(iota_rows == (erow & (N_KEYS - 1)), 1.0, 0.0).astype(BF16)
            gs_ref[u * N_KEYS:(u + 1) * N_KEYS, :] = _dot_nt(pt, qt)
        for gi in range(BUILD_UNROLL // SUBLANES):
            grp = step * (BUILD_UNROLL // SUBLANES) + gi
            for k in range(N_KEYS // SUBLANES):
                tiles = []
                for t in range(SUBLANES):
                    row0 = (gi * SUBLANES + t) * N_KEYS + k * SUBLANES
                    tiles.append(gs_ref[row0:row0 + SUBLANES, :])
                for dist in (4, 2, 1):
                    take_hi = (sub_iota & dist) != 0
                    for t in range(SUBLANES):
                        if t & dist:
                            continue
                        lo, hi = tiles[t], tiles[t + dist]
                        tiles[t] = jnp.where(take_hi, pltpu.roll(hi, dist, 0), lo)
                        tiles[t + dist] = jnp.where(take_hi, hi, pltpu.roll(lo, SUBLANES - dist, 0))
                for s in range(SUBLANES):
                    g_ref[grp, k * SUBLANES + s] = tiles[s]
        return carry

    lax.fori_loop(0, tr // BUILD_UNROLL, build, 0)


def peer_route(q2d, keys_bf16, *, seq, tr=256):
    n, d = q2d.shape
    tr = min(tr, seq)
    nhp = 2 * PEER_HEADS
    n_act = PEER_HEADS * PEER_TOPK
    kern = functools.partial(_peer_route_kernel, tr=tr)
    return pl.pallas_call(
        kern,
        grid=(n // tr,),
        in_specs=[
            pl.BlockSpec((tr, d), lambda i: (i, 0)),
            _resident(keys_bf16.shape, lambda i: (0, 0, 0)),
        ],
        out_specs=pl.BlockSpec((tr // SUBLANES, N_KEYS, SUBLANES, N_KEYS), lambda i: (i, 0, 0, 0)),
        out_shape=jax.ShapeDtypeStruct((n // SUBLANES, N_KEYS, SUBLANES, N_KEYS), F32),
        scratch_shapes=[
            pltpu.VMEM((BUILD_UNROLL * N_KEYS, N_KEYS), F32),
            pltpu.VMEM((nhp, N_KEYS, tr), F32),
            pltpu.VMEM((nhp, PEER_TOPK, tr), F32),
            pltpu.VMEM((nhp, PEER_TOPK, tr), I32),
            pltpu.VMEM((n_act, tr), I32),
            pltpu.VMEM((n_act, tr), F32),
            pltpu.VMEM((tr, n_act), I32),
            pltpu.VMEM((tr, n_act), F32),
        ],
        compiler_params=_params(("arbitrary",)),
        name="peer_route",
    )(q2d, keys_bf16)


DENSE_CHUNK = 256


def _peer_dense_kernel(h_ref, g_ref, u_ref, v_ref, o_ref, act_ref):
    j = pl.program_id(1)
    rows = g_ref.shape[1]
    tm = h_ref.shape[0]

    @pl.when(j == 0)
    def _():
        o_ref[...] = jnp.zeros_like(o_ref)

    per = DENSE_CHUNK // N_KEYS
    for c in range(rows // per):
        ccols = slice(c * DENSE_CHUNK, (c + 1) * DENSE_CHUNK)
        z = _dot_nt(h_ref[...], u_ref[ccols, :])
        for r in range(per):
            row = c * per + r
            gate = g_ref[:, row, :, :].reshape(tm, N_KEYS)
            act_ref[:, row * N_KEYS:(row + 1) * N_KEYS] = (
                _gelu(z[:, r * N_KEYS:(r + 1) * N_KEYS]) * gate).astype(BF16)
    half = act_ref.shape[1] // 2
    o_ref[...] += jnp.dot(act_ref[:, :half], v_ref[:half, :], preferred_element_type=F32)
    o_ref[...] += jnp.dot(act_ref[:, half:], v_ref[half:, :], preferred_element_type=F32)


def peer_dense(h2d, g3d, u_bf16, v_bf16, *, layer, seq, tm=1024, te=1024):
    n, d = h2d.shape
    n_exp = v_bf16.shape[1]
    tm = min(tm, seq)
    return pl.pallas_call(
        _peer_dense_kernel,
        grid=(n // tm, n_exp // te),
        in_specs=[
            pl.BlockSpec((tm, d), lambda i, j: (i, 0), pipeline_mode=pl.Buffered(1)),
            pl.BlockSpec((tm // SUBLANES, te // N_KEYS, SUBLANES, N_KEYS), lambda i, j: (i, j, 0, 0)),
            pl.BlockSpec((None, te, d), lambda i, j: (layer, j, 0)),
            pl.BlockSpec((None, te, d), lambda i, j: (layer, j, 0)),
        ],
        out_specs=pl.BlockSpec((tm, d), lambda i, j: (i, 0)),
        out_shape=jax.ShapeDtypeStruct((n, d), F32),
        scratch_shapes=[pltpu.VMEM((tm, te), BF16)],
        compiler_params=_params(("arbitrary", "arbitrary")),
        name="peer_dense",
    )(h2d, g3d, u_bf16, v_bf16)


def _residual_ln_kernel(x_ref, y_ref, mod_ref, g_ref, b_ref, o_ref, *, alpha, gate_row):
    gate = mod_ref[gate_row:gate_row + 1, :]
    r = alpha * x_ref[...] + (1.0 + gate) * y_ref[...]
    o_ref[...] = _layer_norm(r, g_ref[...], b_ref[...])


def residual_ln(x2d, y2d, mod_l, ln_g, ln_b, *, seq, alpha, gate_row, tm=512):
    n, d = x2d.shape
    tm = min(tm, seq)
    blocks_per_batch = seq // tm
    kern = functools.partial(_residual_ln_kernel, alpha=alpha, gate_row=gate_row)
    return pl.pallas_call(
        kern,
        grid=(n // tm,),
        in_specs=[
            pl.BlockSpec((tm, d), lambda i: (i, 0)),
            pl.BlockSpec((tm, d), lambda i: (i, 0)),
            pl.BlockSpec((None, 6, d), lambda i: (i // blocks_per_batch, 0, 0)),
            _resident((1, d), lambda i: (0, 0)),
            _resident((1, d), lambda i: (0, 0)),
        ],
        out_specs=pl.BlockSpec((tm, d), lambda i: (i, 0)),
        out_shape=jax.ShapeDtypeStruct((n, d), F32),
        compiler_params=_params(("arbitrary",)),
        name="residual_ln",
    )(x2d, y2d, mod_l, ln_g, ln_b)


def kernel(x, c, ada_w, ada_b, ln_g, ln_b, peer_wq, peer_keys, peer_u, peer_v, ev_w_in, ev_w_pool,
           ev_pool_scale, ev_conv_w, ev_conv_b, ev_w_a, ev_b_a, ev_w_x, ev_b_x, ev_lam, ev_w_out,
           od_w_in, od_b_f, od_w_out):
    batch, seq, d = x.shape
    depth = ada_w.shape[0]
    n = batch * seq
    alpha = (2.0 * depth) ** 0.25
    d_mix = od_w_out.shape[1]

    c_pad = jnp.zeros((8, d), F32).at[:batch].set(c)
    mod = ada_mod(c_pad, ada_w, ada_b)[:, :batch].reshape(depth, batch, 6, d)

    n_exp = peer_u.shape[1]
    u_bf = cast_bf16(peer_u.reshape(depth * n_exp, d)).reshape(depth, n_exp, d)
    v_bf = cast_bf16(peer_v.reshape(depth * n_exp, d)).reshape(depth, n_exp, d)

    xf = x.reshape(n, d)
    for l in range(depth):
        mod_l = mod[l]
        g0 = ln_g[l, 0].reshape(1, d)
        b0 = ln_b[l, 0].reshape(1, d)
        g1 = ln_g[l, 1].reshape(1, d)
        b1 = ln_b[l, 1].reshape(1, d)
        if l % 2 == 0:
            e = l // 2
            z = mod_matmul(xf, mod_l, ev_w_in[e].astype(BF16), seq=seq, shift_row=0, scale_row=1,
                           out_dtype=F32, tn=ev_w_in.shape[-1] // 2)
            mixed = even_mix(
                z.reshape(batch, seq, -1), ev_w_pool[e].astype(BF16), ev_pool_scale[e].reshape(1, -1),
                ev_conv_w[e], ev_conv_b[e].reshape(1, -1), ev_w_a[e].astype(BF16),
                ev_b_a[e].reshape(1, -1), ev_w_x[e].astype(BF16), ev_b_x[e].reshape(1, -1),
                ev_lam[e].reshape(1, -1))
            xf = proj_ln(mixed.reshape(n, -1), ev_w_out[e].astype(BF16), xf, mod_l, g0, b0,
                         seq=seq, alpha=alpha, gate_row=2)
        else:
            o = l // 2
            w_in = od_w_in[o]
            w_f = jnp.zeros((d, LANES), F32).at[:, :N_HEADS_ATTN].set(w_in[:, 3 * d_mix:])
            wf_hi = w_f.astype(BF16)
            wf_lo = (w_f - wf_hi.astype(F32)).astype(BF16)
            b_f = jnp.zeros((1, LANES), F32).at[0, :N_HEADS_ATTN].set(od_b_f[o])
            w_qv = jnp.concatenate([w_in[:, :d_mix], w_in[:, 2 * d_mix:3 * d_mix]], axis=1).astype(BF16)
            w_kt = w_in[:, d_mix:2 * d_mix].T.astype(BF16)
            qv, k_t, f_cum = odd_in(xf, mod_l, w_qv, w_kt, wf_hi, wf_lo, b_f, seq=seq)
            f_keys = f_cum[:, :N_HEADS_ATTN].reshape(batch, seq, N_HEADS_ATTN)
            f_keys = jnp.transpose(f_keys, (0, 2, 1)).reshape(batch * N_HEADS_ATTN, 1, seq)
            attn = attention(qv, k_t, f_keys, batch=batch, seq=seq)
            xf = proj_ln(attn, od_w_out[o].astype(BF16), xf, mod_l, g0, b0,
                         seq=seq, alpha=alpha, gate_row=2)

        keys = peer_keys[l].reshape(2 * PEER_HEADS, N_KEYS, -1).astype(BF16)
        q, h2 = mod_matmul(xf, mod_l, peer_wq[l].astype(BF16), seq=seq, shift_row=3, scale_row=4,
                           out_dtype=BF16, emit_h=True, tn=peer_wq.shape[-1])
        gmat = peer_route(q, keys, seq=seq)
        y = peer_dense(h2, gmat, u_bf, v_bf, layer=l, seq=seq)
        xf = residual_ln(xf, y, mod_l, g1, b1, seq=seq, alpha=alpha, gate_row=5)
    return xf.reshape(batch, seq, d)
```

```python
import functools
import math

import jax
import jax.numpy as jnp
from jax import lax
from jax.experimental import pallas as pl
from jax.experimental.pallas import tpu as pltpu

F32 = jnp.float32
BF16 = jnp.bfloat16
I32 = jnp.int32

LN_EPS = 1e-5
POOL_WINDOWS = (2, 4, 8, 16)
CONV_WIDTH = 4
LRU_BLOCKS = 8
LRU_C = 8.0
N_HEADS_ATTN = 16
PEER_HEADS = 8
PEER_TOPK = 16
N_KEYS = 128

V7X_VMEM_BYTES = 64 * 1024 * 1024
VMEM_LIMIT = V7X_VMEM_BYTES * 7 // 8
SUBLANES = 8
LANES = 128
NEG_INF = float("-inf")
LOG2E = math.log2(math.e)


def _params(sem):
    return pltpu.CompilerParams(dimension_semantics=sem, vmem_limit_bytes=VMEM_LIMIT)


def _resident(shape, index_map):
    return pl.BlockSpec(shape, index_map, pipeline_mode=pl.Buffered(1))


def _gelu(x):
    c = math.sqrt(2.0 / math.pi)
    return 0.5 * x * (1.0 + jnp.tanh(c * (x + 0.044715 * (x * x * x))))


def _log_sigmoid(x):
    return jnp.minimum(x, 0.0) - jnp.log1p(jnp.exp(-jnp.abs(x)))


def _sigmoid(x):
    return 0.5 * jnp.tanh(0.5 * x) + 0.5


def _layer_norm(r, g, b):
    mu = jnp.mean(r, axis=-1, keepdims=True)
    d = r - mu
    var = jnp.mean(d * d, axis=-1, keepdims=True)
    return d * lax.rsqrt(var + LN_EPS) * g + b


def _dot_nt(a, b):
    return lax.dot_general(a, b, (((1,), (1,)), ((), ())), preferred_element_type=F32)


def _cast_kernel(w_ref, o_ref):
    o_ref[...] = w_ref[...].astype(o_ref.dtype)


def cast_bf16(w2d, *, tm=1024):
    rows, cols = w2d.shape
    return pl.pallas_call(
        _cast_kernel,
        grid=(rows // tm,),
        in_specs=[pl.BlockSpec((tm, cols), lambda i: (i, 0))],
        out_specs=pl.BlockSpec((tm, cols), lambda i: (i, 0)),
        out_shape=jax.ShapeDtypeStruct((rows, cols), BF16),
        compiler_params=_params(("arbitrary",)),
        name="cast_bf16",
    )(w2d)


def _ada_kernel(c_ref, w_ref, b_ref, o_ref):
    c = c_ref[...]
    ca = c * _sigmoid(c)
    o_ref[...] = jnp.dot(ca, w_ref[...], preferred_element_type=F32) + b_ref[...]


def ada_mod(c_pad, ada_w, ada_b):
    depth, d, n6 = ada_w.shape
    rows = c_pad.shape[0]
    tn = 1024
    return pl.pallas_call(
        _ada_kernel,
        grid=(depth, n6 // tn),
        in_specs=[
            pl.BlockSpec((rows, d), lambda l, j: (0, 0)),
            pl.BlockSpec((None, d, tn), lambda l, j: (l, 0, j)),
            pl.BlockSpec((None, 1, tn), lambda l, j: (l, 0, j)),
        ],
        out_specs=pl.BlockSpec((None, rows, tn), lambda l, j: (l, 0, j)),
        out_shape=jax.ShapeDtypeStruct((depth, rows, n6), F32),
        compiler_params=_params(("arbitrary", "arbitrary")),
        name="ada_mod",
    )(c_pad, ada_w, ada_b.reshape(depth, 1, n6))


def _mod_matmul_kernel(x_ref, mod_ref, w_ref, o_ref, h_ref, *, shift_row, scale_row):
    @pl.when(pl.program_id(1) == 0)
    def _():
        sh = mod_ref[shift_row:shift_row + 1, :]
        sc = mod_ref[scale_row:scale_row + 1, :]
        h_ref[...] = (x_ref[...] * (1.0 + sc) + sh).astype(BF16)

    o_ref[...] = jnp.dot(h_ref[...], w_ref[...], preferred_element_type=F32).astype(o_ref.dtype)


def mod_matmul(x2d, mod_l, w_bf16, *, seq, shift_row, scale_row, out_dtype, emit_h=False,
               tm=1024, tn=1024):
    n, d = x2d.shape
    nout = w_bf16.shape[1]
    tm = min(tm, seq)
    blocks_per_batch = seq // tm
    kern = functools.partial(_mod_matmul_kernel, shift_row=shift_row, scale_row=scale_row)
    out_specs = [pl.BlockSpec((tm, tn), lambda i, j: (i, j))]
    out_shape = [jax.ShapeDtypeStruct((n, nout), out_dtype)]
    scratch = []
    if emit_h:
        out_specs.append(pl.BlockSpec((tm, d), lambda i, j: (i, 0)))
        out_shape.append(jax.ShapeDtypeStruct((n, d), BF16))
    else:
        scratch.append(pltpu.VMEM((tm, d), BF16))
    res = pl.pallas_call(
        kern,
        grid=(n // tm, nout // tn),
        in_specs=[
            pl.BlockSpec((tm, d), lambda i, j: (i, 0)),
            pl.BlockSpec((None, 6, d), lambda i, j: (i // blocks_per_batch, 0, 0)),
            pl.BlockSpec((d, tn), lambda i, j: (0, j)),
        ],
        out_specs=out_specs,
        out_shape=out_shape,
        scratch_shapes=scratch,
        compiler_params=_params(("arbitrary", "arbitrary")),
        name="mod_matmul",
    )(x2d, mod_l, w_bf16)
    return res if emit_h else res[0]


def _cumsum_rows(x):
    rows = x.shape[0]
    ridx = lax.broadcasted_iota(I32, x.shape, 0)
    d = 1
    while d < rows:
        x = x + jnp.where(ridx >= d, pltpu.roll(x, d, 0), 0.0)
        d *= 2
    return x


def _odd_in_kernel(x_ref, y_ref, modp_ref, lng_ref, lnb_ref, mod_ref, w_ref, wkt_ref, wfh_ref, wfl_ref,
                   bf_ref, o_ref, kt_ref, f_ref, xn_ref, h_ref, carry_ref, *, blocks_per_batch, nb,
                   q_scale, alpha):
    i = pl.program_id(0)
    j = pl.program_id(1)

    @pl.when(j == 0)
    def _():
        gate = modp_ref[5:6, :]
        xn = _layer_norm(alpha * x_ref[...] + (1.0 + gate) * y_ref[...], lng_ref[...], lnb_ref[...])
        xn_ref[...] = xn
        sh = mod_ref[0:1, :]
        sc = mod_ref[1:2, :]
        h = xn * (1.0 + sc) + sh
        h_hi = h.astype(BF16)
        h_ref[...] = h_hi
        h_lo = (h - h_hi.astype(F32)).astype(BF16)
        zf = (jnp.dot(h_hi, wfh_ref[...], preferred_element_type=F32)
              + jnp.dot(h_lo, wfh_ref[...], preferred_element_type=F32)
              + jnp.dot(h_hi, wfl_ref[...], preferred_element_type=F32))
        logf = _log_sigmoid(zf + bf_ref[...])

        @pl.when(i % blocks_per_batch == 0)
        def _():
            carry_ref[...] = jnp.zeros_like(carry_ref)

        cs = _cumsum_rows(logf) + carry_ref[0:1, :]
        f_ref[...] = cs
        carry_ref[...] = jnp.broadcast_to(cs[cs.shape[0] - 1:, :], carry_ref.shape)

    is_key = jnp.logical_and(j >= nb, j < 2 * nb)

    @pl.when(jnp.logical_not(is_key))
    def _():
        z = jnp.dot(h_ref[...], w_ref[...], preferred_element_type=F32)
        scale = jnp.where(j < nb, q_scale, 1.0)
        o_ref[...] = (z * scale).astype(o_ref.dtype)

    @pl.when(is_key)
    def _():
        kt_ref[...] = _dot_nt(wkt_ref[...], h_ref[...]).astype(kt_ref.dtype)


def odd_in(x2d, y2d, mod_prev, ln_g, ln_b, mod_l, w_qv, w_kt, wf_hi, wf_lo, b_f_pad, *, seq, alpha,
           tm=512, tn=1024):
    n, d = x2d.shape
    d_mix = w_kt.shape[0]
    tm = min(tm, seq)
    blocks_per_batch = seq // tm
    head_dim = d_mix // N_HEADS_ATTN
    nb = d_mix // tn
    kern = functools.partial(_odd_in_kernel, blocks_per_batch=blocks_per_batch, nb=nb,
                             q_scale=head_dim ** -0.5 * LOG2E, alpha=alpha)

    def qv_block(j):
        return jnp.where(j < nb, j, jnp.where(j < 2 * nb, nb - 1, j - nb))

    def k_block(j):
        return jnp.clip(j - nb, 0, nb - 1)

    return pl.pallas_call(
        kern,
        grid=(n // tm, 3 * nb),
        in_specs=[
            pl.BlockSpec((tm, d), lambda i, j: (i, 0)),
            pl.BlockSpec((tm, d), lambda i, j: (i, 0)),
            pl.BlockSpec((None, 6, d), lambda i, j: (i // blocks_per_batch, 0, 0)),
            _resident((1, d), lambda i, j: (0, 0)),
            _resident((1, d), lambda i, j: (0, 0)),
            pl.BlockSpec((None, 6, d), lambda i, j: (i // blocks_per_batch, 0, 0)),
            pl.BlockSpec((d, tn), lambda i, j: (0, qv_block(j))),
            pl.BlockSpec((tn, d), lambda i, j: (k_block(j), 0)),
            _resident((d, LANES), lambda i, j: (0, 0)),
            _resident((d, LANES), lambda i, j: (0, 0)),
            _resident((1, LANES), lambda i, j: (0, 0)),
        ],
        out_specs=[
            pl.BlockSpec((tm, tn), lambda i, j: (i, qv_block(j))),
            pl.BlockSpec((tn, tm), lambda i, j: (k_block(j), i)),
            pl.BlockSpec((tm, LANES), lambda i, j: (i, 0)),
            pl.BlockSpec((tm, d), lambda i, j: (i, 0)),
        ],
        out_shape=[
            jax.ShapeDtypeStruct((n, 2 * d_mix), BF16),
            jax.ShapeDtypeStruct((d_mix, n), BF16),
            jax.ShapeDtypeStruct((n, LANES), F32),
            jax.ShapeDtypeStruct((n, d), F32),
        ],
        scratch_shapes=[pltpu.VMEM((tm, d), BF16), pltpu.VMEM((SUBLANES, LANES), F32)],
        compiler_params=_params(("arbitrary", "arbitrary")),
        name="odd_in",
    )(x2d, y2d, mod_prev, ln_g, ln_b, mod_l, w_qv, w_kt, wf_hi, wf_lo, b_f_pad)


def _attn_kernel(q_ref, kt_ref, v_ref, fk_ref, o_ref, sa_ref, sb_ref, vaug_ref, pb_ref, *, tq, tk):
    qi = pl.program_id(2)
    f_ref0 = fk_ref[0:1, pl.ds(pl.multiple_of(qi * tq, tq), tk)][:, 0:1]
    dh = q_ref.shape[1]

    @pl.when(qi == 0)
    def _():
        vaug_ref[:, 0:dh] = v_ref[...]
        vaug_ref[:, dh:2 * dh] = jnp.ones((v_ref.shape[0], dh), BF16)
    per_q = tq // tk
    n_full = qi * per_q

    def scores(c, s_ref):
        start = pl.multiple_of(c * tk, tk)
        s = jnp.dot(q_ref[...], kt_ref[:, pl.ds(start, tk)], preferred_element_type=F32)
        s_ref[...] = s + (f_ref0 - fk_ref[0:1, pl.ds(start, tk)]) * LOG2E

    def values(c, p):
        start = pl.multiple_of(c * tk, tk)
        return jnp.dot(p, vaug_ref[pl.ds(start, tk), :], preferred_element_type=F32)

    def softmax_step(s_ref, m, diag_offset=None):
        if diag_offset is not None:
            row = lax.broadcasted_iota(I32, (tq, tk), 0)
            col = lax.broadcasted_iota(I32, (tq, tk), 1) + diag_offset
            s_ref[...] = jnp.where(col <= row, s_ref[...], NEG_INF)
        m_new = jnp.maximum(m, jnp.max(s_ref[...], axis=-1, keepdims=True))
        alpha = jnp.exp2(m - m_new)
        p = jnp.exp2(s_ref[...] - m_new).astype(BF16)
        return m_new, alpha, p

    def update(c, s_ref, carry, diag_offset=None):
        m, acc = carry
        m, alpha, p = softmax_step(s_ref, m, diag_offset)
        return m, alpha * acc + values(c, p)

    pb_ref[...] = jnp.zeros(pb_ref.shape, BF16)
    scores(0, sa_ref)

    def group(c0, count, carry):
        m, acc = carry
        acc = acc + values(jnp.maximum(c0 - 1, 0), pb_ref[...])
        bufs = (sa_ref, sb_ref)
        for t in range(count):
            scores(c0 + t + 1, bufs[(t + 1) % 2])
            m, alpha, p = softmax_step(bufs[t % 2], m)
            if t + 1 < count:
                acc = alpha * acc + values(c0 + t, p)
            else:
                pb_ref[...] = p
                acc = alpha * acc
        return m, acc

    init = (jnp.full((tq, 1), NEG_INF, F32), jnp.zeros((tq, 2 * dh), F32))
    carry = init
    done = 0
    for count in (8, 4, 2):
        steps = (n_full - done) // count
        carry = lax.fori_loop(
            0, steps, lambda p, cr, base=done, count=count: group(base + count * p, count, cr), carry)
        done = done + count * steps
    pending = jnp.maximum(done - 1, 0)

    def flush(carry):
        m, acc = carry
        return m, acc + values(pending, pb_ref[...])

    def diagonal(carry, bufs):
        for dchunk in range(per_q):
            if dchunk + 1 < per_q:
                scores(n_full + dchunk + 1, bufs[(dchunk + 1) % 2])
            carry = update(n_full + dchunk, bufs[dchunk % 2], carry, diag_offset=dchunk * tk)
        return carry

    def even_tail(carry):
        return diagonal(flush(carry), (sa_ref, sb_ref))

    def odd_tail(carry):
        scores(n_full, sb_ref)
        carry = update(n_full - 1, sa_ref, flush(carry))
        return diagonal(carry, (sb_ref, sa_ref))

    if per_q % 2 == 0:
        m, acc = even_tail(carry)
    else:
        m, acc = lax.cond(n_full % 2 == 1, odd_tail, even_tail, carry)
    o_ref[...] = (acc[:, 0:dh] / acc[:, dh:dh + 1]).astype(o_ref.dtype)


def attention(qv, k_t, f_keys, *, batch, seq, tq=512, tk=512):
    n = qv.shape[0]
    d_mix = qv.shape[1] // 2
    heads = N_HEADS_ATTN
    dh = d_mix // heads
    tq = min(tq, seq)
    tk = min(tk, tq)
    nq = seq // tq
    kern = functools.partial(_attn_kernel, tq=tq, tk=tk)
    return pl.pallas_call(
        kern,
        grid=(batch, heads, nq),
        in_specs=[
            pl.BlockSpec((tq, dh), lambda b, h, i: (b * nq + i, h)),
            pl.BlockSpec((dh, seq), lambda b, h, i: (h, b)),
            pl.BlockSpec((seq, dh), lambda b, h, i: (b, heads + h)),
            pl.BlockSpec((None, 1, seq), lambda b, h, i: (b * heads + h, 0, 0)),
        ],
        out_specs=pl.BlockSpec((tq, dh), lambda b, h, i: (b * nq + i, h)),
        out_shape=jax.ShapeDtypeStruct((n, d_mix), BF16),
        scratch_shapes=[pltpu.VMEM((tq, tk), F32), pltpu.VMEM((tq, tk), F32),
                        pltpu.VMEM((seq, 2 * dh), BF16), pltpu.VMEM((tq, tk), BF16)],
        compiler_params=_params(("arbitrary", "arbitrary", "arbitrary")),
        name="attention",
    )(qv, k_t, qv, f_keys)


HALO = 16


def _even_mix_kernel(z_ref, wpool_ref, pscale_ref, convw_ref, convb_ref, wa_ref, ba_ref,
                     wx_ref, bx_ref, lam_ref, o_ref, pbuf, cbuf, hstate, *, ts, d_pool, d_lru):
    sb = pl.program_id(1)
    group = d_pool // len(POOL_WINDOWS)
    blk = d_lru // LRU_BLOCKS

    @pl.when(sb == 0)
    def _():
        pbuf[0:HALO, :] = jnp.zeros((HALO, d_pool), F32)
        cbuf[0:HALO, :] = jnp.zeros((HALO, d_lru), F32)
        hstate[...] = jnp.zeros_like(hstate)

    up = z_ref[:, 0:d_pool]
    pbuf[HALO:HALO + ts, :] = up
    pos = sb * ts + lax.broadcasted_iota(I32, (ts, group), 0)
    ya = []
    for g, w in enumerate(POOL_WINDOWS):
        lo, hi = g * group, (g + 1) * group
        need = w - 1
        cur = pbuf[HALO - need:HALO + ts, lo:hi]
        span = 1
        while span < w:
            rows = cur.shape[0] - span
            cur = cur[span:span + rows, :] + cur[0:rows, :]
            span *= 2
        cnt = jnp.minimum(pos + 1, w).astype(F32)
        pooled = cur / cnt - up[:, lo:hi]
        y = jnp.dot(pooled.astype(BF16), wpool_ref[g], preferred_element_type=F32)
        ya.append(y * pscale_ref[0:1, lo:hi])
    o_ref[:, 0:d_pool] = jnp.concatenate(ya, axis=-1).astype(o_ref.dtype)
    pbuf[0:HALO, :] = pbuf[ts:ts + HALO, :]

    ul = z_ref[:, d_pool:d_pool + d_lru]
    ug = z_ref[:, d_pool + d_lru:d_pool + 2 * d_lru]
    cbuf[HALO:HALO + ts, :] = ul
    xc = jnp.broadcast_to(convb_ref[0:1, :], (ts, d_lru))
    for k in range(CONV_WIDTH):
        off = HALO - (CONV_WIDTH - 1) + k
        xc = xc + cbuf[off:off + ts, :] * convw_ref[k:k + 1, :]
    cbuf[0:HALO, :] = cbuf[ts:ts + HALO, :]

    xb = xc.astype(BF16)
    ra, ia = [], []
    for hb in range(LRU_BLOCKS):
        xs = xb[:, hb * blk:(hb + 1) * blk]
        ra.append(jnp.dot(xs, wa_ref[hb], preferred_element_type=F32))
        ia.append(jnp.dot(xs, wx_ref[hb], preferred_element_type=F32))
    r_gate = _sigmoid(jnp.concatenate(ra, axis=-1) + ba_ref[0:1, :])
    i_gate = _sigmoid(jnp.concatenate(ia, axis=-1) + bx_ref[0:1, :])
    log_a = LRU_C * r_gate * _log_sigmoid(lam_ref[0:1, :])
    a = jnp.exp(log_a)
    m2 = -jnp.tanh(log_a) * (1.0 + a * a)
    mult = jnp.where(m2 > 0.0, m2 * lax.rsqrt(m2), 0.0)
    b = mult * i_gate * xc

    n_grp = ts // SUBLANES
    a = a.reshape(n_grp, SUBLANES, d_lru)
    b = b.reshape(n_grp, SUBLANES, d_lru)
    rsub = lax.broadcasted_iota(I32, (n_grp, SUBLANES, d_lru), 1)
    d = 1
    while d < SUBLANES:
        keep = rsub >= d
        a_sh = jnp.where(keep, pltpu.roll(a, d, 1), 1.0)
        b_sh = jnp.where(keep, pltpu.roll(b, d, 1), 0.0)
        b = a * b_sh + b
        a = a * a_sh
        d *= 2
    state = hstate[0:1, :]
    groups = []
    for v in range(n_grp):
        hv = b[v] + a[v] * state
        state = hv[SUBLANES - 1:SUBLANES, :]
        groups.append(hv)
    h = jnp.concatenate(groups, axis=0)
    hstate[...] = jnp.broadcast_to(state, hstate.shape)
    o_ref[:, d_pool:d_pool + d_lru] = (h * _gelu(ug)).astype(o_ref.dtype)


def even_mix(z3d, w_pool, pool_scale, conv_w, conv_b, w_a, b_a, w_x, b_x, lam, *, ts=256):
    batch, seq, dz = z3d.shape
    d_pool = pool_scale.shape[-1]
    d_lru = lam.shape[-1]
    ts = min(ts, seq)
    group = d_pool // len(POOL_WINDOWS)
    blk = d_lru // LRU_BLOCKS
    kern = functools.partial(_even_mix_kernel, ts=ts, d_pool=d_pool, d_lru=d_lru)
    const2 = lambda b, s: (0, 0)
    const3 = lambda b, s: (0, 0, 0)
    return pl.pallas_call(
        kern,
        grid=(batch, seq // ts),
        in_specs=[
            pl.BlockSpec((None, ts, dz), lambda b, s: (b, s, 0)),
            _resident((len(POOL_WINDOWS), group, group), const3),
            _resident((1, d_pool), const2),
            _resident((CONV_WIDTH, d_lru), const2),
            _resident((1, d_lru), const2),
            _resident((LRU_BLOCKS, blk, blk), const3),
            _resident((1, d_lru), const2),
            _resident((LRU_BLOCKS, blk, blk), const3),
            _resident((1, d_lru), const2),
            _resident((1, d_lru), const2),
        ],
        out_specs=pl.BlockSpec((None, ts, d_pool + d_lru), lambda b, s: (b, s, 0)),
        out_shape=jax.ShapeDtypeStruct((batch, seq, d_pool + d_lru), BF16),
        scratch_shapes=[
            pltpu.VMEM((HALO + ts, d_pool), F32),
            pltpu.VMEM((HALO + ts, d_lru), F32),
            pltpu.VMEM((8, d_lru), F32),
        ],
        compiler_params=_params(("arbitrary", "arbitrary")),
        name="even_mix",
    )(z3d, w_pool, pool_scale, conv_w, conv_b, w_a, b_a, w_x, b_x, lam)


def _proj_ln_kernel(a_ref, w_ref, x_ref, mod_ref, g_ref, b_ref, o_ref, *, alpha, gate_row):
    y = jnp.dot(a_ref[...], w_ref[...], preferred_element_type=F32)
    gate = mod_ref[gate_row:gate_row + 1, :]
    r = alpha * x_ref[...] + (1.0 + gate) * y
    o_ref[...] = _layer_norm(r, g_ref[...], b_ref[...])


def proj_ln(a2d, w_bf16, x2d, mod_l, ln_g, ln_b, *, seq, alpha, gate_row, tm=512):
    n, d = x2d.shape
    k = a2d.shape[1]
    tm = min(tm, seq)
    blocks_per_batch = seq // tm
    kern = functools.partial(_proj_ln_kernel, alpha=alpha, gate_row=gate_row)
    return pl.pallas_call(
        kern,
        grid=(n // tm,),
        in_specs=[
            pl.BlockSpec((tm, k), lambda i: (i, 0)),
            _resident((k, d), lambda i: (0, 0)),
            pl.BlockSpec((tm, d), lambda i: (i, 0)),
            pl.BlockSpec((None, 6, d), lambda i: (i // blocks_per_batch, 0, 0)),
            _resident((1, d), lambda i: (0, 0)),
            _resident((1, d), lambda i: (0, 0)),
        ],
        out_specs=pl.BlockSpec((tm, d), lambda i: (i, 0)),
        out_shape=jax.ShapeDtypeStruct((n, d), F32),
        compiler_params=_params(("arbitrary",)),
        name="proj_ln",
    )(a2d, w_bf16, x2d, mod_l, ln_g, ln_b)


def _top16_rows(s, kidx, sentinel):
    t = s.shape[1]
    r16 = lax.broadcasted_iota(I32, (PEER_TOPK, t), 0)
    vals = jnp.zeros((PEER_TOPK, t), F32)
    idxs = jnp.zeros((PEER_TOPK, t), I32)
    for r in range(PEER_TOPK):
        m = jnp.max(s, axis=0, keepdims=True)
        am = jnp.min(jnp.where(s == m, kidx, sentinel), axis=0, keepdims=True)
        s = jnp.where(kidx == am, NEG_INF, s)
        vals = jnp.where(r16 == r, m, vals)
        idxs = jnp.where(r16 == r, am, idxs)
    return vals, idxs


def _take_rows16(table, sel):
    out = jnp.zeros(sel.shape, table.dtype)
    for s in range(PEER_TOPK):
        out = jnp.where(sel == s, table[s:s + 1, :], out)
    return out


TOPK_PER_STEP = 4
BUILD_UNROLL = 64


def _peer_route_kernel(q_ref, keys_ref, g_ref,
                       gs_ref, sc_ref, val_ref, idx_ref, e_ref, w_ref, et_ref, wt_ref, *, tr):
    nhp = 2 * PEER_HEADS
    for hp in range(nhp):
        sc_ref[hp] = _dot_nt(keys_ref[hp], q_ref[:, hp * N_KEYS:(hp + 1) * N_KEYS])

    kidx = lax.broadcasted_iota(I32, (N_KEYS, tr), 0)

    def first_topk(step, carry):
        for sub in range(TOPK_PER_STEP):
            hp = TOPK_PER_STEP * step + sub
            vals, idxs = _top16_rows(sc_ref[hp], kidx, N_KEYS)
            val_ref[hp] = vals
            idx_ref[hp] = idxs
        return carry

    lax.fori_loop(0, nhp // TOPK_PER_STEP, first_topk, 0)

    i16 = lax.broadcasted_iota(I32, (16, tr), 0)
    i8 = lax.broadcasted_iota(I32, (8, tr), 0)
    flat = jnp.concatenate([
        i16 * 16, i8 * 16 + 1, i8 * 16 + 2, i8 * 16 + 3,
        i16, 16 + i8, 32 + i8,
    ], axis=0)
    valid = jnp.concatenate([
        i16 < 16, i8 < 8, i8 < 5, i8 < 4,
        i16 >= 4, i8 >= 4, i8 == 4,
    ], axis=0)

    def second_topk(step, carry):
        for sub in range(TOPK_PER_STEP):
            hd = TOPK_PER_STEP * step + sub
            s1 = val_ref[2 * hd]
            s2 = val_ref[2 * hd + 1]
            cand = jnp.concatenate([
                s1 + s2[0:1, :], s1[0:8, :] + s2[1:2, :], s1[0:8, :] + s2[2:3, :],
                s1[0:8, :] + s2[3:4, :],
                s1[0:1, :] + s2, s1[1:2, :] + s2[0:8, :], s1[2:3, :] + s2[0:8, :],
            ], axis=0)
            cand = jnp.where(valid, cand, NEG_INF)
            top, fsel = _top16_rows(cand, flat, PEER_TOPK * PEER_TOPK)
            e = jnp.exp(top - top[0:1, :])
            gate = e / jnp.sum(e, axis=0, keepdims=True)
            a = _take_rows16(idx_ref[2 * hd], fsel >> 4)
            b = _take_rows16(idx_ref[2 * hd + 1], fsel & (PEER_TOPK - 1))
            row0 = pl.multiple_of(hd * PEER_TOPK, PEER_TOPK)
            e_ref[pl.ds(row0, PEER_TOPK), :] = a * N_KEYS + b
            w_ref[pl.ds(row0, PEER_TOPK), :] = gate
        return carry

    lax.fori_loop(0, PEER_HEADS // TOPK_PER_STEP, second_topk, 0)

    n_act = PEER_HEADS * PEER_TOPK
    for c in range(tr // n_act):
        et_ref[c * n_act:(c + 1) * n_act, :] = e_ref[:, c * n_act:(c + 1) * n_act].T
        wt_ref[c * n_act:(c + 1) * n_act, :] = w_ref[:, c * n_act:(c + 1) * n_act].T

    iota_rows = lax.broadcasted_iota(I32, (N_KEYS, n_act), 0)
    sub_iota = lax.broadcasted_iota(I32, (SUBLANES, N_KEYS), 0)

    def build(step, carry):
        base = pl.multiple_of(step * BUILD_UNROLL, BUILD_UNROLL)
        erows = et_ref[pl.ds(base, BUILD_UNROLL), :]
        wrows = wt_ref[pl.ds(base, BUILD_UNROLL), :]
        for u in range(BUILD_UNROLL):
            erow = erows[u:u + 1, :]
            wrow = wrows[u:u + 1, :]
            pt = jnp.where(iota_rows == (erow >> 7), wrow, 0.0).astype(BF16)
            qt = jnp.where(iota_rows == (erow & (N_KEYS - 1)), 1.0, 0.0).astype(BF16)
            gs_ref[u * N_KEYS:(u + 1) * N_KEYS, :] = _dot_nt(pt, qt)
        for gi in range(BUILD_UNROLL // SUBLANES):
            grp = step * (BUILD_UNROLL // SUBLANES) + gi
            for k in range(N_KEYS // SUBLANES):
                tiles = []
                for t in range(SUBLANES):
                    row0 = (gi * SUBLANES + t) * N_KEYS + k * SUBLANES
                    tiles.append(gs_ref[row0:row0 + SUBLANES, :])
                for dist in (4, 2, 1):
                    take_hi = (sub_iota & dist) != 0
                    for t in range(SUBLANES):
                        if t & dist:
                            continue
                        lo, hi = tiles[t], tiles[t + dist]
                        tiles[t] = jnp.where(take_hi, pltpu.roll(hi, dist, 0), lo)
                        tiles[t + dist] = jnp.where(take_hi, hi, pltpu.roll(lo, SUBLANES - dist, 0))
                for s in range(SUBLANES):
                    g_ref[grp, k * SUBLANES + s] = tiles[s]
        return carry

    lax.fori_loop(0, tr // BUILD_UNROLL, build, 0)


def peer_route(q2d, keys_bf16, *, seq, tr=256):
    n, d = q2d.shape
    tr = min(tr, seq)
    nhp = 2 * PEER_HEADS
    n_act = PEER_HEADS * PEER_TOPK
    kern = functools.partial(_peer_route_kernel, tr=tr)
    return pl.pallas_call(
        kern,
        grid=(n // tr,),
        in_specs=[
            pl.BlockSpec((tr, d), lambda i: (i, 0)),
            _resident(keys_bf16.shape, lambda i: (0, 0, 0)),
        ],
        out_specs=pl.BlockSpec((tr // SUBLANES, N_KEYS, SUBLANES, N_KEYS), lambda i: (i, 0, 0, 0)),
        out_shape=jax.ShapeDtypeStruct((n // SUBLANES, N_KEYS, SUBLANES, N_KEYS), F32),
        scratch_shapes=[
            pltpu.VMEM((BUILD_UNROLL * N_KEYS, N_KEYS), F32),
            pltpu.VMEM((nhp, N_KEYS, tr), F32),
            pltpu.VMEM((nhp, PEER_TOPK, tr), F32),
            pltpu.VMEM((nhp, PEER_TOPK, tr), I32),
            pltpu.VMEM((n_act, tr), I32),
            pltpu.VMEM((n_act, tr), F32),
            pltpu.VMEM((tr, n_act), I32),
            pltpu.VMEM((tr, n_act), F32),
        ],
        compiler_params=_params(("arbitrary",)),
        name="peer_route",
    )(q2d, keys_bf16)


DENSE_CHUNK = 256


def _peer_dense_kernel(h_ref, g_ref, u_ref, v_ref, o_ref, act_ref):
    j = pl.program_id(1)
    rows = g_ref.shape[1]
    tm = h_ref.shape[0]

    @pl.when(j == 0)
    def _():
        o_ref[...] = jnp.zeros_like(o_ref)

    per = DENSE_CHUNK // N_KEYS
    for c in range(rows // per):
        ccols = slice(c * DENSE_CHUNK, (c + 1) * DENSE_CHUNK)
        z = _dot_nt(h_ref[...], u_ref[ccols, :])
        for r in range(per):
            row = c * per + r
            gate = g_ref[:, row, :, :].reshape(tm, N_KEYS)
            act_ref[:, row * N_KEYS:(row + 1) * N_KEYS] = (
                _gelu(z[:, r * N_KEYS:(r + 1) * N_KEYS]) * gate).astype(BF16)
    half = act_ref.shape[1] // 2
    o_ref[...] += jnp.dot(act_ref[:, :half], v_ref[:half, :], preferred_element_type=F32)
    o_ref[...] += jnp.dot(act_ref[:, half:], v_ref[half:, :], preferred_element_type=F32)


def peer_dense(h2d, g3d, u_bf16, v_bf16, *, layer, seq, tm=1024, te=1024):
    n, d = h2d.shape
    n_exp = v_bf16.shape[1]
    tm = min(tm, seq)
    return pl.pallas_call(
        _peer_dense_kernel,
        grid=(n // tm, n_exp // te),
        in_specs=[
            pl.BlockSpec((tm, d), lambda i, j: (i, 0), pipeline_mode=pl.Buffered(1)),
            pl.BlockSpec((tm // SUBLANES, te // N_KEYS, SUBLANES, N_KEYS), lambda i, j: (i, j, 0, 0)),
            pl.BlockSpec((None, te, d), lambda i, j: (layer, j, 0)),
            pl.BlockSpec((None, te, d), lambda i, j: (layer, j, 0)),
        ],
        out_specs=pl.BlockSpec((tm, d), lambda i, j: (i, 0)),
        out_shape=jax.ShapeDtypeStruct((n, d), F32),
        scratch_shapes=[pltpu.VMEM((tm, te), BF16)],
        compiler_params=_params(("arbitrary", "arbitrary")),
        name="peer_dense",
    )(h2d, g3d, u_bf16, v_bf16)


def _residual_ln_kernel(x_ref, y_ref, mod_ref, g_ref, b_ref, o_ref, *, alpha, gate_row):
    gate = mod_ref[gate_row:gate_row + 1, :]
    r = alpha * x_ref[...] + (1.0 + gate) * y_ref[...]
    o_ref[...] = _layer_norm(r, g_ref[...], b_ref[...])


def residual_ln(x2d, y2d, mod_l, ln_g, ln_b, *, seq, alpha, gate_row, tm=512):
    n, d = x2d.shape
    tm = min(tm, seq)
    blocks_per_batch = seq // tm
    kern = functools.partial(_residual_ln_kernel, alpha=alpha, gate_row=gate_row)
    return pl.pallas_call(
        kern,
        grid=(n // tm,),
        in_specs=[
            pl.BlockSpec((tm, d), lambda i: (i, 0)),
            pl.BlockSpec((tm, d), lambda i: (i, 0)),
            pl.BlockSpec((None, 6, d), lambda i: (i // blocks_per_batch, 0, 0)),
            _resident((1, d), lambda i: (0, 0)),
            _resident((1, d), lambda i: (0, 0)),
        ],
        out_specs=pl.BlockSpec((tm, d), lambda i: (i, 0)),
        out_shape=jax.ShapeDtypeStruct((n, d), F32),
        compiler_params=_params(("arbitrary",)),
        name="residual_ln",
    )(x2d, y2d, mod_l, ln_g, ln_b)


def kernel(x, c, ada_w, ada_b, ln_g, ln_b, peer_wq, peer_keys, peer_u, peer_v, ev_w_in, ev_w_pool,
           ev_pool_scale, ev_conv_w, ev_conv_b, ev_w_a, ev_b_a, ev_w_x, ev_b_x, ev_lam, ev_w_out,
           od_w_in, od_b_f, od_w_out):
    batch, seq, d = x.shape
    depth = ada_w.shape[0]
    n = batch * seq
    alpha = (2.0 * depth) ** 0.25
    d_mix = od_w_out.shape[1]

    c_pad = jnp.zeros((8, d), F32).at[:batch].set(c)
    mod = ada_mod(c_pad, ada_w, ada_b)[:, :batch].reshape(depth, batch, 6, d)

    n_exp = peer_u.shape[1]
    u_bf = cast_bf16(peer_u.reshape(depth * n_exp, d)).reshape(depth, n_exp, d)
    v_bf = cast_bf16(peer_v.reshape(depth * n_exp, d)).reshape(depth, n_exp, d)

    xf = x.reshape(n, d)
    for l in range(depth):
        mod_l = mod[l]
        g0 = ln_g[l, 0].reshape(1, d)
        b0 = ln_b[l, 0].reshape(1, d)
        g1 = ln_g[l, 1].reshape(1, d)
        b1 = ln_b[l, 1].reshape(1, d)
        if l % 2 == 0:
            e = l // 2
            z = mod_matmul(xf, mod_l, ev_w_in[e].astype(BF16), seq=seq, shift_row=0, scale_row=1,
                           out_dtype=F32, tn=ev_w_in.shape[-1] // 2)
            mixed = even_mix(
                z.reshape(batch, seq, -1), ev_w_pool[e].astype(BF16), ev_pool_scale[e].reshape(1, -1),
                ev_conv_w[e], ev_conv_b[e].reshape(1, -1), ev_w_a[e].astype(BF16),
                ev_b_a[e].reshape(1, -1), ev_w_x[e].astype(BF16), ev_b_x[e].reshape(1, -1),
                ev_lam[e].reshape(1, -1))
            xf = proj_ln(mixed.reshape(n, -1), ev_w_out[e].astype(BF16), xf, mod_l, g0, b0,
                         seq=seq, alpha=alpha, gate_row=2)
        else:
            o = l // 2
            w_in = od_w_in[o]
            w_f = jnp.zeros((d, LANES), F32).at[:, :N_HEADS_ATTN].set(w_in[:, 3 * d_mix:])
            wf_hi = w_f.astype(BF16)
            wf_lo = (w_f - wf_hi.astype(F32)).astype(BF16)
            b_f = jnp.zeros((1, LANES), F32).at[0, :N_HEADS_ATTN].set(od_b_f[o])
            w_qv = jnp.concatenate([w_in[:, :d_mix], w_in[:, 2 * d_mix:3 * d_mix]], axis=1).astype(BF16)
            w_kt = w_in[:, d_mix:2 * d_mix].T.astype(BF16)
            x_prev, y_prev, mod_prev, g_prev, b_prev = pending
            qv, k_t, f_cum, xf = odd_in(x_prev, y_prev, mod_prev, g_prev, b_prev, mod_l, w_qv, w_kt,
                                        wf_hi, wf_lo, b_f, seq=seq, alpha=alpha)
            f_keys = f_cum[:, :N_HEADS_ATTN].reshape(batch, seq, N_HEADS_ATTN)
            f_keys = jnp.transpose(f_keys, (0, 2, 1)).reshape(batch * N_HEADS_ATTN, 1, seq)
            attn = attention(qv, k_t, f_keys, batch=batch, seq=seq)
            xf = proj_ln(attn, od_w_out[o].astype(BF16), xf, mod_l, g0, b0,
                         seq=seq, alpha=alpha, gate_row=2)

        keys = peer_keys[l].reshape(2 * PEER_HEADS, N_KEYS, -1).astype(BF16)
        q, h2 = mod_matmul(xf, mod_l, peer_wq[l].astype(BF16), seq=seq, shift_row=3, scale_row=4,
                           out_dtype=BF16, emit_h=True, tn=peer_wq.shape[-1])
        gmat = peer_route(q, keys, seq=seq)
        y = peer_dense(h2, gmat, u_bf, v_bf, layer=l, seq=seq)
        if l + 1 < depth and (l + 1) % 2 == 1:
            pending = (xf, y, mod_l, g1, b1)
        else:
            xf = residual_ln(xf, y, mod_l, g1, b1, seq=seq, alpha=alpha, gate_row=5)
    return xf.reshape(batch, seq, d)
```

```python
import functools
import math

import jax
import jax.numpy as jnp
from jax import lax
from jax.experimental import pallas as pl
from jax.experimental.pallas import tpu as pltpu

F32 = jnp.float32
BF16 = jnp.bfloat16
I32 = jnp.int32

LN_EPS = 1e-5
POOL_WINDOWS = (2, 4, 8, 16)
CONV_WIDTH = 4
LRU_BLOCKS = 8
LRU_C = 8.0
N_HEADS_ATTN = 16
PEER_HEADS = 8
PEER_TOPK = 16
N_KEYS = 128

V7X_VMEM_BYTES = 64 * 1024 * 1024
VMEM_LIMIT = V7X_VMEM_BYTES * 7 // 8
SUBLANES = 8
LANES = 128
NEG_INF = float("-inf")
LOG2E = math.log2(math.e)


def _params(sem):
    return pltpu.CompilerParams(dimension_semantics=sem, vmem_limit_bytes=VMEM_LIMIT)


def _resident(shape, index_map):
    return pl.BlockSpec(shape, index_map, pipeline_mode=pl.Buffered(1))


def _gelu(x):
    c = math.sqrt(2.0 / math.pi)
    return 0.5 * x * (1.0 + jnp.tanh(c * (x + 0.044715 * (x * x * x))))


def _log_sigmoid(x):
    return jnp.minimum(x, 0.0) - jnp.log1p(jnp.exp(-jnp.abs(x)))


def _sigmoid(x):
    return 0.5 * jnp.tanh(0.5 * x) + 0.5


def _layer_norm(r, g, b):
    mu = jnp.mean(r, axis=-1, keepdims=True)
    d = r - mu
    var = jnp.mean(d * d, axis=-1, keepdims=True)
    return d * lax.rsqrt(var + LN_EPS) * g + b


def _dot_nt(a, b):
    return lax.dot_general(a, b, (((1,), (1,)), ((), ())), preferred_element_type=F32)


def _cast_kernel(w_ref, o_ref):
    o_ref[...] = w_ref[...].astype(o_ref.dtype)


def cast_bf16(w2d, *, tm=1024):
    rows, cols = w2d.shape
    return pl.pallas_call(
        _cast_kernel,
        grid=(rows // tm,),
        in_specs=[pl.BlockSpec((tm, cols), lambda i: (i, 0))],
        out_specs=pl.BlockSpec((tm, cols), lambda i: (i, 0)),
        out_shape=jax.ShapeDtypeStruct((rows, cols), BF16),
        compiler_params=_params(("arbitrary",)),
        name="cast_bf16",
    )(w2d)


def _ada_kernel(c_ref, w_ref, b_ref, o_ref):
    c = c_ref[...]
    ca = c * _sigmoid(c)
    o_ref[...] = jnp.dot(ca, w_ref[...], preferred_element_type=F32) + b_ref[...]


def ada_mod(c_pad, ada_w, ada_b):
    depth, d, n6 = ada_w.shape
    rows = c_pad.shape[0]
    tn = 1024
    return pl.pallas_call(
        _ada_kernel,
        grid=(depth, n6 // tn),
        in_specs=[
            pl.BlockSpec((rows, d), lambda l, j: (0, 0)),
            pl.BlockSpec((None, d, tn), lambda l, j: (l, 0, j)),
            pl.BlockSpec((None, 1, tn), lambda l, j: (l, 0, j)),
        ],
        out_specs=pl.BlockSpec((None, rows, tn), lambda l, j: (l, 0, j)),
        out_shape=jax.ShapeDtypeStruct((depth, rows, n6), F32),
        compiler_params=_params(("arbitrary", "arbitrary")),
        name="ada_mod",
    )(c_pad, ada_w, ada_b.reshape(depth, 1, n6))


def _mod_matmul_kernel(x_ref, mod_ref, w_ref, o_ref, h_ref, *, shift_row, scale_row):
    @pl.when(pl.program_id(1) == 0)
    def _():
        sh = mod_ref[shift_row:shift_row + 1, :]
        sc = mod_ref[scale_row:scale_row + 1, :]
        h_ref[...] = (x_ref[...] * (1.0 + sc) + sh).astype(BF16)

    o_ref[...] = jnp.dot(h_ref[...], w_ref[...], preferred_element_type=F32).astype(o_ref.dtype)


def mod_matmul(x2d, mod_l, w_bf16, *, seq, shift_row, scale_row, out_dtype, emit_h=False,
               tm=1024, tn=1024):
    n, d = x2d.shape
    nout = w_bf16.shape[1]
    tm = min(tm, seq)
    blocks_per_batch = seq // tm
    kern = functools.partial(_mod_matmul_kernel, shift_row=shift_row, scale_row=scale_row)
    out_specs = [pl.BlockSpec((tm, tn), lambda i, j: (i, j))]
    out_shape = [jax.ShapeDtypeStruct((n, nout), out_dtype)]
    scratch = []
    if emit_h:
        out_specs.append(pl.BlockSpec((tm, d), lambda i, j: (i, 0)))
        out_shape.append(jax.ShapeDtypeStruct((n, d), BF16))
    else:
        scratch.append(pltpu.VMEM((tm, d), BF16))
    res = pl.pallas_call(
        kern,
        grid=(n // tm, nout // tn),
        in_specs=[
            pl.BlockSpec((tm, d), lambda i, j: (i, 0)),
            pl.BlockSpec((None, 6, d), lambda i, j: (i // blocks_per_batch, 0, 0)),
            pl.BlockSpec((d, tn), lambda i, j: (0, j)),
        ],
        out_specs=out_specs,
        out_shape=out_shape,
        scratch_shapes=scratch,
        compiler_params=_params(("arbitrary", "arbitrary")),
        name="mod_matmul",
    )(x2d, mod_l, w_bf16)
    return res if emit_h else res[0]


def _cumsum_rows(x):
    rows = x.shape[0]
    ridx = lax.broadcasted_iota(I32, x.shape, 0)
    d = 1
    while d < rows:
        x = x + jnp.where(ridx >= d, pltpu.roll(x, d, 0), 0.0)
        d *= 2
    return x


def _odd_in_kernel(x_ref, mod_ref, w_ref, wkt_ref, wfh_ref, wfl_ref, bf_ref, o_ref, kt_ref, f_ref,
                   h_ref, carry_ref, *, blocks_per_batch, nb, q_scale):
    i = pl.program_id(0)
    j = pl.program_id(1)

    @pl.when(j == 0)
    def _():
        sh = mod_ref[0:1, :]
        sc = mod_ref[1:2, :]
        h = x_ref[...] * (1.0 + sc) + sh
        h_hi = h.astype(BF16)
        h_ref[...] = h_hi
        h_lo = (h - h_hi.astype(F32)).astype(BF16)
        zf = (jnp.dot(h_hi, wfh_ref[...], preferred_element_type=F32)
              + jnp.dot(h_lo, wfh_ref[...], preferred_element_type=F32)
              + jnp.dot(h_hi, wfl_ref[...], preferred_element_type=F32))
        logf = _log_sigmoid(zf + bf_ref[...])

        @pl.when(i % blocks_per_batch == 0)
        def _():
            carry_ref[...] = jnp.zeros_like(carry_ref)

        cs = _cumsum_rows(logf) + carry_ref[0:1, :]
        f_ref[...] = cs
        carry_ref[...] = jnp.broadcast_to(cs[cs.shape[0] - 1:, :], carry_ref.shape)

    is_key = jnp.logical_and(j >= nb, j < 2 * nb)

    @pl.when(jnp.logical_not(is_key))
    def _():
        z = jnp.dot(h_ref[...], w_ref[...], preferred_element_type=F32)
        scale = jnp.where(j < nb, q_scale, 1.0)
        o_ref[...] = (z * scale).astype(o_ref.dtype)

    @pl.when(is_key)
    def _():
        kt_ref[...] = _dot_nt(wkt_ref[...], h_ref[...]).astype(kt_ref.dtype)


def odd_in(x2d, mod_l, w_qv, w_kt, wf_hi, wf_lo, b_f_pad, *, seq, tm=1024, tn=1024):
    n, d = x2d.shape
    d_mix = w_kt.shape[0]
    tm = min(tm, seq)
    blocks_per_batch = seq // tm
    head_dim = d_mix // N_HEADS_ATTN
    nb = d_mix // tn
    kern = functools.partial(_odd_in_kernel, blocks_per_batch=blocks_per_batch, nb=nb,
                             q_scale=head_dim ** -0.5 * LOG2E)

    def qv_block(j):
        return jnp.where(j < nb, j, jnp.where(j < 2 * nb, nb - 1, j - nb))

    def k_block(j):
        return jnp.clip(j - nb, 0, nb - 1)

    return pl.pallas_call(
        kern,
        grid=(n // tm, 3 * nb),
        in_specs=[
            pl.BlockSpec((tm, d), lambda i, j: (i, 0)),
            pl.BlockSpec((None, 6, d), lambda i, j: (i // blocks_per_batch, 0, 0)),
            pl.BlockSpec((d, tn), lambda i, j: (0, qv_block(j))),
            pl.BlockSpec((tn, d), lambda i, j: (k_block(j), 0)),
            _resident((d, LANES), lambda i, j: (0, 0)),
            _resident((d, LANES), lambda i, j: (0, 0)),
            _resident((1, LANES), lambda i, j: (0, 0)),
        ],
        out_specs=[
            pl.BlockSpec((tm, tn), lambda i, j: (i, qv_block(j))),
            pl.BlockSpec((tn, tm), lambda i, j: (k_block(j), i)),
            pl.BlockSpec((tm, LANES), lambda i, j: (i, 0)),
        ],
        out_shape=[
            jax.ShapeDtypeStruct((n, 2 * d_mix), BF16),
            jax.ShapeDtypeStruct((d_mix, n), BF16),
            jax.ShapeDtypeStruct((n, LANES), F32),
        ],
        scratch_shapes=[pltpu.VMEM((tm, d), BF16), pltpu.VMEM((SUBLANES, LANES), F32)],
        compiler_params=_params(("arbitrary", "arbitrary")),
        name="odd_in",
    )(x2d, mod_l, w_qv, w_kt, wf_hi, wf_lo, b_f_pad)


def _attn_kernel(q_ref, kt_ref, v_ref, fk_ref, o_ref, sa_ref, sb_ref, vaug_ref, pb_ref, *, tq, tk):
    qi = pl.program_id(2)
    f_ref0 = fk_ref[0:1, pl.ds(pl.multiple_of(qi * tq, tq), tk)][:, 0:1]
    dh = q_ref.shape[1]

    @pl.when(qi == 0)
    def _():
        vaug_ref[:, 0:dh] = v_ref[...]
        vaug_ref[:, dh:2 * dh] = jnp.ones((v_ref.shape[0], dh), BF16)
    per_q = tq // tk
    n_full = qi * per_q

    def scores(c, s_ref):
        start = pl.multiple_of(c * tk, tk)
        s = jnp.dot(q_ref[...], kt_ref[:, pl.ds(start, tk)], preferred_element_type=F32)
        s_ref[...] = s + (f_ref0 - fk_ref[0:1, pl.ds(start, tk)]) * LOG2E

    def values(c, p):
        start = pl.multiple_of(c * tk, tk)
        return jnp.dot(p, vaug_ref[pl.ds(start, tk), :], preferred_element_type=F32)

    def softmax_step(s_ref, m, diag_offset=None):
        if diag_offset is not None:
            row = lax.broadcasted_iota(I32, (tq, tk), 0)
            col = lax.broadcasted_iota(I32, (tq, tk), 1) + diag_offset
            s_ref[...] = jnp.where(col <= row, s_ref[...], NEG_INF)
        m_new = jnp.maximum(m, jnp.max(s_ref[...], axis=-1, keepdims=True))
        alpha = jnp.exp2(m - m_new)
        p = jnp.exp2(s_ref[...] - m_new).astype(BF16)
        return m_new, alpha, p

    def update(c, s_ref, carry, diag_offset=None):
        m, acc = carry
        m, alpha, p = softmax_step(s_ref, m, diag_offset)
        return m, alpha * acc + values(c, p)

    pb_ref[...] = jnp.zeros(pb_ref.shape, BF16)
    scores(0, sa_ref)

    def group(c0, count, carry):
        m, acc = carry
        acc = acc + values(jnp.maximum(c0 - 1, 0), pb_ref[...])
        bufs = (sa_ref, sb_ref)
        for t in range(count):
            scores(c0 + t + 1, bufs[(t + 1) % 2])
            m, alpha, p = softmax_step(bufs[t % 2], m)
            if t + 1 < count:
                acc = alpha * acc + values(c0 + t, p)
            else:
                pb_ref[...] = p
                acc = alpha * acc
        return m, acc

    init = (jnp.full((tq, 1), NEG_INF, F32), jnp.zeros((tq, 2 * dh), F32))
    carry = init
    done = 0
    for count in (8, 4, 2):
        steps = (n_full - done) // count
        carry = lax.fori_loop(
            0, steps, lambda p, cr, base=done, count=count: group(base + count * p, count, cr), carry)
        done = done + count * steps
    pending = jnp.maximum(done - 1, 0)

    def flush(carry):
        m, acc = carry
        return m, acc + values(pending, pb_ref[...])

    def diagonal(carry, bufs):
        for dchunk in range(per_q):
            if dchunk + 1 < per_q:
                scores(n_full + dchunk + 1, bufs[(dchunk + 1) % 2])
            carry = update(n_full + dchunk, bufs[dchunk % 2], carry, diag_offset=dchunk * tk)
        return carry

    def even_tail(carry):
        return diagonal(flush(carry), (sa_ref, sb_ref))

    def odd_tail(carry):
        scores(n_full, sb_ref)
        carry = update(n_full - 1, sa_ref, flush(carry))
        return diagonal(carry, (sb_ref, sa_ref))

    if per_q % 2 == 0:
        m, acc = even_tail(carry)
    else:
        m, acc = lax.cond(n_full % 2 == 1, odd_tail, even_tail, carry)
    o_ref[...] = (acc[:, 0:dh] / acc[:, dh:dh + 1]).astype(o_ref.dtype)


def attention(qv, k_t, f_keys, *, batch, seq, tq=512, tk=512):
    n = qv.shape[0]
    d_mix = qv.shape[1] // 2
    heads = N_HEADS_ATTN
    dh = d_mix // heads
    tq = min(tq, seq)
    tk = min(tk, tq)
    nq = seq // tq
    kern = functools.partial(_attn_kernel, tq=tq, tk=tk)
    return pl.pallas_call(
        kern,
        grid=(batch, heads, nq),
        in_specs=[
            pl.BlockSpec((tq, dh), lambda b, h, i: (b * nq + i, h)),
            pl.BlockSpec((dh, seq), lambda b, h, i: (h, b)),
            pl.BlockSpec((seq, dh), lambda b, h, i: (b, heads + h)),
            pl.BlockSpec((None, 1, seq), lambda b, h, i: (b * heads + h, 0, 0)),
        ],
        out_specs=pl.BlockSpec((tq, dh), lambda b, h, i: (b * nq + i, h)),
        out_shape=jax.ShapeDtypeStruct((n, d_mix), BF16),
        scratch_shapes=[pltpu.VMEM((tq, tk), F32), pltpu.VMEM((tq, tk), F32),
                        pltpu.VMEM((seq, 2 * dh), BF16), pltpu.VMEM((tq, tk), BF16)],
        compiler_params=_params(("arbitrary", "arbitrary", "arbitrary")),
        name="attention",
    )(qv, k_t, qv, f_keys)


HALO = 16


def _even_mix_kernel(z_ref, wpool_ref, pscale_ref, convw_ref, convb_ref, wa_ref, ba_ref,
                     wx_ref, bx_ref, lam_ref, o_ref, pbuf, cbuf, hstate, *, ts, d_pool, d_lru):
    sb = pl.program_id(1)
    group = d_pool // len(POOL_WINDOWS)
    blk = d_lru // LRU_BLOCKS

    @pl.when(sb == 0)
    def _():
        pbuf[0:HALO, :] = jnp.zeros((HALO, d_pool), F32)
        cbuf[0:HALO, :] = jnp.zeros((HALO, d_lru), F32)
        hstate[...] = jnp.zeros_like(hstate)

    up = z_ref[:, 0:d_pool]
    pbuf[HALO:HALO + ts, :] = up
    pos = sb * ts + lax.broadcasted_iota(I32, (ts, group), 0)
    ya = []
    for g, w in enumerate(POOL_WINDOWS):
        lo, hi = g * group, (g + 1) * group
        need = w - 1
        cur = pbuf[HALO - need:HALO + ts, lo:hi]
        span = 1
        while span < w:
            rows = cur.shape[0] - span
            cur = cur[span:span + rows, :] + cur[0:rows, :]
            span *= 2
        cnt = jnp.minimum(pos + 1, w).astype(F32)
        pooled = cur / cnt - up[:, lo:hi]
        y = jnp.dot(pooled.astype(BF16), wpool_ref[g], preferred_element_type=F32)
        ya.append(y * pscale_ref[0:1, lo:hi])
    o_ref[:, 0:d_pool] = jnp.concatenate(ya, axis=-1).astype(o_ref.dtype)
    pbuf[0:HALO, :] = pbuf[ts:ts + HALO, :]

    ul = z_ref[:, d_pool:d_pool + d_lru]
    ug = z_ref[:, d_pool + d_lru:d_pool + 2 * d_lru]
    cbuf[HALO:HALO + ts, :] = ul
    xc = jnp.broadcast_to(convb_ref[0:1, :], (ts, d_lru))
    for k in range(CONV_WIDTH):
        off = HALO - (CONV_WIDTH - 1) + k
        xc = xc + cbuf[off:off + ts, :] * convw_ref[k:k + 1, :]
    cbuf[0:HALO, :] = cbuf[ts:ts + HALO, :]

    xb = xc.astype(BF16)
    ra, ia = [], []
    for hb in range(LRU_BLOCKS):
        xs = xb[:, hb * blk:(hb + 1) * blk]
        ra.append(jnp.dot(xs, wa_ref[hb], preferred_element_type=F32))
        ia.append(jnp.dot(xs, wx_ref[hb], preferred_element_type=F32))
    r_gate = _sigmoid(jnp.concatenate(ra, axis=-1) + ba_ref[0:1, :])
    i_gate = _sigmoid(jnp.concatenate(ia, axis=-1) + bx_ref[0:1, :])
    log_a = LRU_C * r_gate * _log_sigmoid(lam_ref[0:1, :])
    a = jnp.exp(log_a)
    m2 = -jnp.tanh(log_a) * (1.0 + a * a)
    mult = jnp.where(m2 > 0.0, m2 * lax.rsqrt(m2), 0.0)
    b = mult * i_gate * xc

    n_grp = ts // SUBLANES
    a = a.reshape(n_grp, SUBLANES, d_lru)
    b = b.reshape(n_grp, SUBLANES, d_lru)
    rsub = lax.broadcasted_iota(I32, (n_grp, SUBLANES, d_lru), 1)
    d = 1
    while d < SUBLANES:
        keep = rsub >= d
        a_sh = jnp.where(keep, pltpu.roll(a, d, 1), 1.0)
        b_sh = jnp.where(keep, pltpu.roll(b, d, 1), 0.0)
        b = a * b_sh + b
        a = a * a_sh
        d *= 2
    state = hstate[0:1, :]
    groups = []
    for v in range(n_grp):
        hv = b[v] + a[v] * state
        state = hv[SUBLANES - 1:SUBLANES, :]
        groups.append(hv)
    h = jnp.concatenate(groups, axis=0)
    hstate[...] = jnp.broadcast_to(state, hstate.shape)
    o_ref[:, d_pool:d_pool + d_lru] = (h * _gelu(ug)).astype(o_ref.dtype)


def even_mix(z3d, w_pool, pool_scale, conv_w, conv_b, w_a, b_a, w_x, b_x, lam, *, ts=256):
    batch, seq, dz = z3d.shape
    d_pool = pool_scale.shape[-1]
    d_lru = lam.shape[-1]
    ts = min(ts, seq)
    group = d_pool // len(POOL_WINDOWS)
    blk = d_lru // LRU_BLOCKS
    kern = functools.partial(_even_mix_kernel, ts=ts, d_pool=d_pool, d_lru=d_lru)
    const2 = lambda b, s: (0, 0)
    const3 = lambda b, s: (0, 0, 0)
    return pl.pallas_call(
        kern,
        grid=(batch, seq // ts),
        in_specs=[
            pl.BlockSpec((None, ts, dz), lambda b, s: (b, s, 0)),
            _resident((len(POOL_WINDOWS), group, group), const3),
            _resident((1, d_pool), const2),
            _resident((CONV_WIDTH, d_lru), const2),
            _resident((1, d_lru), const2),
            _resident((LRU_BLOCKS, blk, blk), const3),
            _resident((1, d_lru), const2),
            _resident((LRU_BLOCKS, blk, blk), const3),
            _resident((1, d_lru), const2),
            _resident((1, d_lru), const2),
        ],
        out_specs=pl.BlockSpec((None, ts, d_pool + d_lru), lambda b, s: (b, s, 0)),
        out_shape=jax.ShapeDtypeStruct((batch, seq, d_pool + d_lru), BF16),
        scratch_shapes=[
            pltpu.VMEM((HALO + ts, d_pool), F32),
            pltpu.VMEM((HALO + ts, d_lru), F32),
            pltpu.VMEM((8, d_lru), F32),
        ],
        compiler_params=_params(("arbitrary", "arbitrary")),
        name="even_mix",
    )(z3d, w_pool, pool_scale, conv_w, conv_b, w_a, b_a, w_x, b_x, lam)


def _proj_ln_kernel(a_ref, w_ref, x_ref, mod_ref, g_ref, b_ref, o_ref, *, alpha, gate_row):
    y = jnp.dot(a_ref[...], w_ref[...], preferred_element_type=F32)
    gate = mod_ref[gate_row:gate_row + 1, :]
    r = alpha * x_ref[...] + (1.0 + gate) * y
    o_ref[...] = _layer_norm(r, g_ref[...], b_ref[...])


def proj_ln(a2d, w_bf16, x2d, mod_l, ln_g, ln_b, *, seq, alpha, gate_row, tm=512):
    n, d = x2d.shape
    k = a2d.shape[1]
    tm = min(tm, seq)
    blocks_per_batch = seq // tm
    kern = functools.partial(_proj_ln_kernel, alpha=alpha, gate_row=gate_row)
    return pl.pallas_call(
        kern,
        grid=(n // tm,),
        in_specs=[
            pl.BlockSpec((tm, k), lambda i: (i, 0)),
            _resident((k, d), lambda i: (0, 0)),
            pl.BlockSpec((tm, d), lambda i: (i, 0)),
            pl.BlockSpec((None, 6, d), lambda i: (i // blocks_per_batch, 0, 0)),
            _resident((1, d), lambda i: (0, 0)),
            _resident((1, d), lambda i: (0, 0)),
        ],
        out_specs=pl.BlockSpec((tm, d), lambda i: (i, 0)),
        out_shape=jax.ShapeDtypeStruct((n, d), F32),
        compiler_params=_params(("arbitrary",)),
        name="proj_ln",
    )(a2d, w_bf16, x2d, mod_l, ln_g, ln_b)


def _top16_rows(s, kidx, sentinel):
    t = s.shape[1]
    r16 = lax.broadcasted_iota(I32, (PEER_TOPK, t), 0)
    vals = jnp.zeros((PEER_TOPK, t), F32)
    idxs = jnp.zeros((PEER_TOPK, t), I32)
    for r in range(PEER_TOPK):
        m = jnp.max(s, axis=0, keepdims=True)
        am = jnp.min(jnp.where(s == m, kidx, sentinel), axis=0, keepdims=True)
        s = jnp.where(kidx == am, NEG_INF, s)
        vals = jnp.where(r16 == r, m, vals)
        idxs = jnp.where(r16 == r, am, idxs)
    return vals, idxs


def _take_rows16(table, sel):
    out = jnp.zeros(sel.shape, table.dtype)
    for s in range(PEER_TOPK):
        out = jnp.where(sel == s, table[s:s + 1, :], out)
    return out


FIRST_TOPK_PER_STEP = 8
TOPK_PER_STEP = 4
BUILD_UNROLL = 64


def _peer_route_kernel(q_ref, keys_ref, g_ref,
                       gs_ref, sc_ref, val_ref, idx_ref, e_ref, w_ref, et_ref, wt_ref, *, tr):
    nhp = 2 * PEER_HEADS
    for hp in range(nhp):
        sc_ref[hp] = _dot_nt(keys_ref[hp], q_ref[:, hp * N_KEYS:(hp + 1) * N_KEYS])

    kidx = lax.broadcasted_iota(I32, (N_KEYS, tr), 0)

    def first_topk(step, carry):
        for sub in range(FIRST_TOPK_PER_STEP):
            hp = FIRST_TOPK_PER_STEP * step + sub
            vals, idxs = _top16_rows(sc_ref[hp], kidx, N_KEYS)
            val_ref[hp] = vals
            idx_ref[hp] = idxs
        return carry

    lax.fori_loop(0, nhp // FIRST_TOPK_PER_STEP, first_topk, 0)

    i16 = lax.broadcasted_iota(I32, (16, tr), 0)
    i8 = lax.broadcasted_iota(I32, (8, tr), 0)
    flat = jnp.concatenate([
        i16 * 16, i8 * 16 + 1, i8 * 16 + 2, i8 * 16 + 3,
        i16, 16 + i8, 32 + i8,
    ], axis=0)
    valid = jnp.concatenate([
        i16 < 16, i8 < 8, i8 < 5, i8 < 4,
        i16 >= 4, i8 >= 4, i8 == 4,
    ], axis=0)

    def second_topk(step, carry):
        for sub in range(TOPK_PER_STEP):
            hd = TOPK_PER_STEP * step + sub
            s1 = val_ref[2 * hd]
            s2 = val_ref[2 * hd + 1]
            cand = jnp.concatenate([
                s1 + s2[0:1, :], s1[0:8, :] + s2[1:2, :], s1[0:8, :] + s2[2:3, :],
                s1[0:8, :] + s2[3:4, :],
                s1[0:1, :] + s2, s1[1:2, :] + s2[0:8, :], s1[2:3, :] + s2[0:8, :],
            ], axis=0)
            cand = jnp.where(valid, cand, NEG_INF)
            top, fsel = _top16_rows(cand, flat, PEER_TOPK * PEER_TOPK)
            e = jnp.exp(top - top[0:1, :])
            gate = e / jnp.sum(e, axis=0, keepdims=True)
            a = _take_rows16(idx_ref[2 * hd], fsel >> 4)
            b = _take_rows16(idx_ref[2 * hd + 1], fsel & (PEER_TOPK - 1))
            row0 = pl.multiple_of(hd * PEER_TOPK, PEER_TOPK)
            e_ref[pl.ds(row0, PEER_TOPK), :] = a * N_KEYS + b
            w_ref[pl.ds(row0, PEER_TOPK), :] = gate
        return carry

    lax.fori_loop(0, PEER_HEADS // TOPK_PER_STEP, second_topk, 0)

    n_act = PEER_HEADS * PEER_TOPK
    for c in range(tr // n_act):
        et_ref[c * n_act:(c + 1) * n_act, :] = e_ref[:, c * n_act:(c + 1) * n_act].T
        wt_ref[c * n_act:(c + 1) * n_act, :] = w_ref[:, c * n_act:(c + 1) * n_act].T

    iota_rows = lax.broadcasted_iota(I32, (N_KEYS, n_act), 0)
    sub_iota = lax.broadcasted_iota(I32, (SUBLANES, N_KEYS), 0)

    def build(step, carry):
        base = pl.multiple_of(step * BUILD_UNROLL, BUILD_UNROLL)
        erows = et_ref[pl.ds(base, BUILD_UNROLL), :]
        wrows = wt_ref[pl.ds(base, BUILD_UNROLL), :]
        for u in range(BUILD_UNROLL):
            erow = erows[u:u + 1, :]
            wrow = wrows[u:u + 1, :]
            pt = jnp.where(iota_rows == (erow >> 7), wrow, 0.0).astype(BF16)
            qt = jnp.where(iota_rows == (erow & (N_KEYS - 1)), 1.0, 0.0).astype(BF16)
            gs_ref[u * N_KEYS:(u + 1) * N_KEYS, :] = _dot_nt(pt, qt)
        for gi in range(BUILD_UNROLL // SUBLANES):
            grp = step * (BUILD_UNROLL // SUBLANES) + gi
            for k in range(N_KEYS // SUBLANES):
                tiles = []
                for t in range(SUBLANES):
                    row0 = (gi * SUBLANES + t) * N_KEYS + k * SUBLANES
                    tiles.append(gs_ref[row0:row0 + SUBLANES, :])
                for dist in (4, 2, 1):
                    take_hi = (sub_iota & dist) != 0
                    for t in range(SUBLANES):
                        if t & dist:
                            continue
                        lo, hi = tiles[t], tiles[t + dist]
                        tiles[t] = jnp.where(take_hi, pltpu.roll(hi, dist, 0), lo)
                        tiles[t + dist] = jnp.where(take_hi, hi, pltpu.roll(lo, SUBLANES - dist, 0))
                for s in range(SUBLANES):
                    g_ref[grp, k * SUBLANES + s] = tiles[s]
        return carry

    lax.fori_loop(0, tr // BUILD_UNROLL, build, 0)


def peer_route(q2d, keys_bf16, *, seq, tr=256):
    n, d = q2d.shape
    tr = min(tr, seq)
    nhp = 2 * PEER_HEADS
    n_act = PEER_HEADS * PEER_TOPK
    kern = functools.partial(_peer_route_kernel, tr=tr)
    return pl.pallas_call(
        kern,
        grid=(n // tr,),
        in_specs=[
            pl.BlockSpec((tr, d), lambda i: (i, 0)),
            _resident(keys_bf16.shape, lambda i: (0, 0, 0)),
        ],
        out_specs=pl.BlockSpec((tr // SUBLANES, N_KEYS, SUBLANES, N_KEYS), lambda i: (i, 0, 0, 0)),
        out_shape=jax.ShapeDtypeStruct((n // SUBLANES, N_KEYS, SUBLANES, N_KEYS), F32),
        scratch_shapes=[
            pltpu.VMEM((BUILD_UNROLL * N_KEYS, N_KEYS), F32),
            pltpu.VMEM((nhp, N_KEYS, tr), F32),
            pltpu.VMEM((nhp, PEER_TOPK, tr), F32),
            pltpu.VMEM((nhp, PEER_TOPK, tr), I32),
            pltpu.VMEM((n_act, tr), I32),
            pltpu.VMEM((n_act, tr), F32),
            pltpu.VMEM((tr, n_act), I32),
            pltpu.VMEM((tr, n_act), F32),
        ],
        compiler_params=_params(("arbitrary",)),
        name="peer_route",
    )(q2d, keys_bf16)


DENSE_CHUNK = 256


def _peer_dense_kernel(h_ref, g_ref, u_ref, v_ref, o_ref, act_ref):
    j = pl.program_id(1)
    rows = g_ref.shape[1]
    tm = h_ref.shape[0]

    @pl.when(j == 0)
    def _():
        o_ref[...] = jnp.zeros_like(o_ref)

    per = DENSE_CHUNK // N_KEYS
    for c in range(rows // per):
        ccols = slice(c * DENSE_CHUNK, (c + 1) * DENSE_CHUNK)
        z = _dot_nt(h_ref[...], u_ref[ccols, :])
        for r in range(per):
            row = c * per + r
            gate = g_ref[:, row, :, :].reshape(tm, N_KEYS)
            act_ref[:, row * N_KEYS:(row + 1) * N_KEYS] = (
                _gelu(z[:, r * N_KEYS:(r + 1) * N_KEYS]) * gate).astype(BF16)
    half = act_ref.shape[1] // 2
    o_ref[...] += jnp.dot(act_ref[:, :half], v_ref[:half, :], preferred_element_type=F32)
    o_ref[...] += jnp.dot(act_ref[:, half:], v_ref[half:, :], preferred_element_type=F32)


def peer_dense(h2d, g3d, u_bf16, v_bf16, *, layer, seq, tm=1024, te=1024):
    n, d = h2d.shape
    n_exp = v_bf16.shape[1]
    tm = min(tm, seq)
    return pl.pallas_call(
        _peer_dense_kernel,
        grid=(n // tm, n_exp // te),
        in_specs=[
            pl.BlockSpec((tm, d), lambda i, j: (i, 0), pipeline_mode=pl.Buffered(1)),
            pl.BlockSpec((tm // SUBLANES, te // N_KEYS, SUBLANES, N_KEYS), lambda i, j: (i, j, 0, 0)),
            pl.BlockSpec((None, te, d), lambda i, j: (layer, j, 0)),
            pl.BlockSpec((None, te, d), lambda i, j: (layer, j, 0)),
        ],
        out_specs=pl.BlockSpec((tm, d), lambda i, j: (i, 0)),
        out_shape=jax.ShapeDtypeStruct((n, d), F32),
        scratch_shapes=[pltpu.VMEM((tm, te), BF16)],
        compiler_params=_params(("arbitrary", "arbitrary")),
        name="peer_dense",
    )(h2d, g3d, u_bf16, v_bf16)


def _residual_ln_kernel(x_ref, y_ref, mod_ref, g_ref, b_ref, o_ref, *, alpha, gate_row):
    gate = mod_ref[gate_row:gate_row + 1, :]
    r = alpha * x_ref[...] + (1.0 + gate) * y_ref[...]
    o_ref[...] = _layer_norm(r, g_ref[...], b_ref[...])


def residual_ln(x2d, y2d, mod_l, ln_g, ln_b, *, seq, alpha, gate_row, tm=512):
    n, d = x2d.shape
    tm = min(tm, seq)
    blocks_per_batch = seq // tm
    kern = functools.partial(_residual_ln_kernel, alpha=alpha, gate_row=gate_row)
    return pl.pallas_call(
        kern,
        grid=(n // tm,),
        in_specs=[
            pl.BlockSpec((tm, d), lambda i: (i, 0)),
            pl.BlockSpec((tm, d), lambda i: (i, 0)),
            pl.BlockSpec((None, 6, d), lambda i: (i // blocks_per_batch, 0, 0)),
            _resident((1, d), lambda i: (0, 0)),
            _resident((1, d), lambda i: (0, 0)),
        ],
        out_specs=pl.BlockSpec((tm, d), lambda i: (i, 0)),
        out_shape=jax.ShapeDtypeStruct((n, d), F32),
        compiler_params=_params(("arbitrary",)),
        name="residual_ln",
    )(x2d, y2d, mod_l, ln_g, ln_b)


def kernel(x, c, ada_w, ada_b, ln_g, ln_b, peer_wq, peer_keys, peer_u, peer_v, ev_w_in, ev_w_pool,
           ev_pool_scale, ev_conv_w, ev_conv_b, ev_w_a, ev_b_a, ev_w_x, ev_b_x, ev_lam, ev_w_out,
           od_w_in, od_b_f, od_w_out):
    batch, seq, d = x.shape
    depth = ada_w.shape[0]
    n = batch * seq
    alpha = (2.0 * depth) ** 0.25
    d_mix = od_w_out.shape[1]

    c_pad = jnp.zeros((8, d), F32).at[:batch].set(c)
    mod = ada_mod(c_pad, ada_w, ada_b)[:, :batch].reshape(depth, batch, 6, d)

    n_exp = peer_u.shape[1]
    u_bf = cast_bf16(peer_u.reshape(depth * n_exp, d)).reshape(depth, n_exp, d)
    v_bf = cast_bf16(peer_v.reshape(depth * n_exp, d)).reshape(depth, n_exp, d)

    xf = x.reshape(n, d)
    for l in range(depth):
        mod_l = mod[l]
        g0 = ln_g[l, 0].reshape(1, d)
        b0 = ln_b[l, 0].reshape(1, d)
        g1 = ln_g[l, 1].reshape(1, d)
        b1 = ln_b[l, 1].reshape(1, d)
        if l % 2 == 0:
            e = l // 2
            z = mod_matmul(xf, mod_l, ev_w_in[e].astype(BF16), seq=seq, shift_row=0, scale_row=1,
                           out_dtype=F32, tn=ev_w_in.shape[-1] // 2)
            mixed = even_mix(
                z.reshape(batch, seq, -1), ev_w_pool[e].astype(BF16), ev_pool_scale[e].reshape(1, -1),
                ev_conv_w[e], ev_conv_b[e].reshape(1, -1), ev_w_a[e].astype(BF16),
                ev_b_a[e].reshape(1, -1), ev_w_x[e].astype(BF16), ev_b_x[e].reshape(1, -1),
                ev_lam[e].reshape(1, -1))
            xf = proj_ln(mixed.reshape(n, -1), ev_w_out[e].astype(BF16), xf, mod_l, g0, b0,
                         seq=seq, alpha=alpha, gate_row=2)
        else:
            o = l // 2
            w_in = od_w_in[o]
            w_f = jnp.zeros((d, LANES), F32).at[:, :N_HEADS_ATTN].set(w_in[:, 3 * d_mix:])
            wf_hi = w_f.astype(BF16)
            wf_lo = (w_f - wf_hi.astype(F32)).astype(BF16)
            b_f = jnp.zeros((1, LANES), F32).at[0, :N_HEADS_ATTN].set(od_b_f[o])
            w_qv = jnp.concatenate([w_in[:, :d_mix], w_in[:, 2 * d_mix:3 * d_mix]], axis=1).astype(BF16)
            w_kt = w_in[:, d_mix:2 * d_mix].T.astype(BF16)
            qv, k_t, f_cum = odd_in(xf, mod_l, w_qv, w_kt, wf_hi, wf_lo, b_f, seq=seq)
            f_keys = f_cum[:, :N_HEADS_ATTN].reshape(batch, seq, N_HEADS_ATTN)
            f_keys = jnp.transpose(f_keys, (0, 2, 1)).reshape(batch * N_HEADS_ATTN, 1, seq)
            attn = attention(qv, k_t, f_keys, batch=batch, seq=seq)
            xf = proj_ln(attn, od_w_out[o].astype(BF16), xf, mod_l, g0, b0,
                         seq=seq, alpha=alpha, gate_row=2)

        keys = peer_keys[l].reshape(2 * PEER_HEADS, N_KEYS, -1).astype(BF16)
        q, h2 = mod_matmul(xf, mod_l, peer_wq[l].astype(BF16), seq=seq, shift_row=3, scale_row=4,
                           out_dtype=BF16, emit_h=True, tn=peer_wq.shape[-1])
        gmat = peer_route(q, keys, seq=seq)
        y = peer_dense(h2, gmat, u_bf, v_bf, layer=l, seq=seq)
        xf = residual_ln(xf, y, mod_l, g1, b1, seq=seq, alpha=alpha, gate_row=5)
    return xf.reshape(batch, seq, d)
```
